```python
import math
import jax, jax.numpy as jnp
from jax import lax
import numpy as np

D_MODEL = 1024
BATCH = 8
SEQ = 2048
DEPTH = 4

CTX_LEN = 256
GRID_W = 64
EPS = 1e-6
NEG_INF = -1e9

HY_DIM = 256
HY_ORDER = 2
HY_SHORT = 3
HY_BANDS = 8
HY_POS_DIM = 1 + 2 * HY_BANDS
HY_FILT_HID = 64
HY_DECAY_TARGET = 1e-2
HY_FAST = 0.3
HY_SLOW = 1.5

NA_HEADS = 4
NA_HEAD_DIM = 64
NA_DIM = NA_HEADS * NA_HEAD_DIM
NA_WIN_ROWS = 8
NA_WIN_COLS = 16
NA_QCOLS = 16
NA_KCOLS = NA_QCOLS + NA_WIN_COLS
NA_NCB = GRID_W // NA_QCOLS

MLA_HEADS = 8
MLA_Q_RANK = 256
MLA_KV_RANK = 128
MLA_NOPE = 64
MLA_ROPE = 32
MLA_V = 64
MLA_QK = MLA_NOPE + MLA_ROPE
MLA_DIM = MLA_HEADS * MLA_V
MLA_QBLOCK = 128
ROPE_BASE = 10000.0
ROPE_FREQS = MLA_ROPE // 4

MIX_DIM = HY_DIM + NA_DIM + MLA_DIM
D_FF = 4 * D_MODEL
IN_HY = 3 * HY_DIM
IN_NA = 3 * NA_DIM
IN_MLA = MLA_Q_RANK + MLA_KV_RANK + MLA_ROPE
IN_DIM = IN_HY + IN_NA + IN_MLA

kernel_name = "hybrid_hyena_na_mla_prefix_dit"

F32 = jnp.float32


def _rms(x, g):
    xf = x.astype(F32)
    y = xf * lax.rsqrt(jnp.mean(xf * xf, axis=-1, keepdims=True) + EPS)
    return (y * g.astype(F32)).astype(x.dtype)


def _modulation(cond, w_mod, b_mod):
    m = jax.nn.silu(cond) @ w_mod + b_mod
    return jnp.split(m[:, None, :], 6, axis=-1)


def _adaln(h, g, shift, scale):
    return _rms(h, g) * (1.0 + scale) + shift


def _heads(t, n):
    b, l, _ = t.shape
    return t.reshape(b, l, n, -1).transpose(0, 2, 1, 3)


def _merge(t):
    b, n, l, d = t.shape
    return t.transpose(0, 2, 1, 3).reshape(b, l, n * d)


def _dense_attend(q, k, v):
    s = jnp.einsum('bhqd,bhkd->bhqk', q, k).astype(F32) * (q.shape[-1] ** -0.5)
    p = jax.nn.softmax(s, axis=-1).astype(v.dtype)
    return jnp.einsum('bhqk,bhkd->bhqd', p, v)


def _short_conv(u, w, b):
    y = lax.conv_general_dilated(u, w[:, None, :].astype(u.dtype), window_strides=(1,), padding='SAME',
                                 dimension_numbers=('NWC', 'WIO', 'NWC'), feature_group_count=u.shape[-1])
    return y + b


def _hyena_filters(L, w1, b1, w2, b2, w3, b3, freq):
    t = jnp.linspace(0.0, 1.0, L, dtype=F32)[:, None]
    w = (2.0 * math.pi / L) * jnp.arange(L, dtype=F32)[:, None]
    bands = jnp.linspace(1e-4, HY_BANDS - 1, HY_BANDS, dtype=F32)
    z = jnp.concatenate([t, jnp.cos(bands * w), -jnp.sin(bands * w)], axis=-1)
    freq = freq.astype(F32)
    hdn = jnp.sin(freq[0] * (z @ w1.astype(F32) + b1.astype(F32)))
    hdn = jnp.sin(freq[1] * (hdn @ w2.astype(F32) + b2.astype(F32)))
    h = (hdn @ w3.astype(F32) + b3.astype(F32)).reshape(L, HY_ORDER, 2, HY_DIM)
    deltas = jnp.linspace(math.log(HY_DECAY_TARGET) / HY_SLOW, math.log(HY_DECAY_TARGET) / HY_FAST,
                          HY_DIM, dtype=F32)
    decay = jnp.exp(-t * jnp.abs(deltas))
    h = h * decay[:, None, None, :]
    return h / (jnp.sum(jnp.abs(h), axis=(0, 2), keepdims=True) + EPS)


def _bidir_fftconv(u, h, bias):
    L, C = u.shape[1], u.shape[2]
    k = jnp.concatenate([h[:, 0], jnp.zeros((1, C), h.dtype), h[:0:-1, 1]], axis=0)
    uf = u.astype(F32)
    y = jnp.fft.irfft(jnp.fft.rfft(uf, n=2 * L, axis=1) * jnp.fft.rfft(k, axis=0), n=2 * L, axis=1)[:, :L]
    return (y + uf * bias.astype(F32)).astype(u.dtype)


def _hyena(p, conv_w, conv_b, filt, bias):
    p = _short_conv(p, conv_w, conv_b)
    v, x1, x2 = jnp.split(p, 3, axis=-1)
    z = v
    for o, gate in enumerate((x1, x2)):
        z = gate * _bidir_fftconv(z, filt[:, o], bias[o])
    return z


def _na_qkv(p, g_q, g_k):
    q, k, v = jnp.split(p, 3, axis=-1)
    return _rms(_heads(q, NA_HEADS), g_q), _rms(_heads(k, NA_HEADS), g_k), _heads(v, NA_HEADS)


def _na_latent(q, k, v, kc, vc, rpb):
    B, H, S, dh = q.shape
    R = S // GRID_W
    wr = min(NA_WIN_ROWS, R)
    r = jnp.arange(R)
    row_idx = jnp.clip(r - NA_WIN_ROWS // 2, 0, R - wr)[:, None] + jnp.arange(wr)
    qcol = jnp.arange(GRID_W).reshape(NA_NCB, NA_QCOLS)
    bstart = jnp.clip(jnp.arange(NA_NCB) * NA_QCOLS - NA_WIN_COLS // 2, 0, GRID_W - NA_KCOLS)
    col_idx = bstart[:, None] + jnp.arange(NA_KCOLS)
    ri = row_idx[:, None, :, None]
    ci = col_idx[None, :, None, :]

    def gather(t):
        tg = t.reshape(B, H, R, GRID_W, t.shape[-1])[:, :, ri, ci]
        return tg.reshape(B, H, R, NA_NCB, wr * NA_KCOLS, t.shape[-1])

    kb, vb = gather(k), gather(v)
    qg = q.reshape(B, H, R, NA_NCB, NA_QCOLS, dh)
    cstart = jnp.clip(qcol - NA_WIN_COLS // 2, 0, GRID_W - NA_WIN_COLS)
    kcol = col_idx[:, None, :]
    valid = (kcol >= cstart[..., None]) & (kcol < cstart[..., None] + NA_WIN_COLS)
    dr = row_idx - r[:, None] + NA_WIN_ROWS - 1
    dc = jnp.clip(kcol - qcol[..., None] + NA_WIN_COLS - 1, 0, 2 * NA_WIN_COLS - 2)
    bias = rpb.astype(F32)[:, dr[:, None, None, :, None], dc[None, :, :, None, :]]
    bias = jnp.where(valid[None, None, :, :, None, :], bias, NEG_INF)
    bias = bias.reshape(H, R, NA_NCB, NA_QCOLS, wr * NA_KCOLS)
    scale = dh ** -0.5
    s_loc = jnp.einsum('bhrjqd,bhrjkd->bhrjqk', qg, kb).astype(F32) * scale + bias[None]
    s_ctx = jnp.einsum('bhrjqd,bhcd->bhrjqc', qg, kc).astype(F32) * scale
    n_loc = s_loc.shape[-1]
    p = jax.nn.softmax(jnp.concatenate([s_loc, s_ctx], axis=-1), axis=-1).astype(v.dtype)
    o = (jnp.einsum('bhrjqk,bhrjkd->bhrjqd', p[..., :n_loc], vb)
         + jnp.einsum('bhrjqc,bhcd->bhrjqd', p[..., n_loc:], vc))
    return o.reshape(B, H, S, dh)


def _axial_rope_tables(S):
    t = jnp.arange(S)
    pos = jnp.stack([t // GRID_W, t % GRID_W], axis=-1).astype(F32)
    inv = ROPE_BASE ** (-jnp.arange(ROPE_FREQS, dtype=F32) / ROPE_FREQS)
    ang = pos[:, :, None] * inv
    return jnp.cos(ang), jnp.sin(ang)


def _axial_rope(x, cos, sin):
    xs = x.reshape(x.shape[:-1] + (2, 2, ROPE_FREQS))
    x1, x2 = xs[..., 0, :], xs[..., 1, :]
    c = cos[:, None].astype(x.dtype)
    s = sin[:, None].astype(x.dtype)
    return jnp.stack([x1 * c - x2 * s, x2 * c + x1 * s], axis=-2).reshape(x.shape)


def _mla_qkv(p, g_qa, g_kva, w_q_up, w_kv_up, g_q, g_k, rope):
    B, L, _ = p.shape
    cq, ckv, kr = jnp.split(p, [MLA_Q_RANK, MLA_Q_RANK + MLA_KV_RANK], axis=-1)
    q = (_rms(cq, g_qa) @ w_q_up).reshape(B, L, MLA_HEADS, MLA_QK)
    kv = (_rms(ckv, g_kva) @ w_kv_up).reshape(B, L, MLA_HEADS, MLA_NOPE + MLA_V)
    k = jnp.concatenate([kv[..., :MLA_NOPE],
                         jnp.broadcast_to(kr[:, :, None, :], (B, L, MLA_HEADS, MLA_ROPE))], axis=-1)
    v = kv[..., MLA_NOPE:]
    q = _rms(q, g_q)
    k = _rms(k, g_k)
    if rope is not None:
        cos, sin = rope
        q = jnp.concatenate([q[..., :MLA_NOPE], _axial_rope(q[..., MLA_NOPE:], cos, sin)], axis=-1)
        k = jnp.concatenate([k[..., :MLA_NOPE], _axial_rope(k[..., MLA_NOPE:], cos, sin)], axis=-1)
    return q.transpose(0, 2, 1, 3), k.transpose(0, 2, 1, 3), v.transpose(0, 2, 1, 3)


def _mla_latent(q, k, v, kc, vc):
    B, H, S, dq = q.shape
    kall = jnp.concatenate([k, kc], axis=2)
    vall = jnp.concatenate([v, vc], axis=2)
    nb = S // MLA_QBLOCK
    qb = q.reshape(B, H, nb, MLA_QBLOCK, dq).transpose(2, 0, 1, 3, 4)
    scale = dq ** -0.5

    def block(qi):
        s = jnp.einsum('bhqd,bhkd->bhqk', qi, kall).astype(F32) * scale
        return jnp.einsum('bhqk,bhkd->bhqd', jax.nn.softmax(s, axis=-1).astype(vall.dtype), vall)

    o = lax.map(block, qb)
    return o.transpose(1, 2, 0, 3, 4).reshape(B, H, S, v.shape[-1])


def _sqrelu_mlp(h, w1, b1, w2, b2):
    a = jax.nn.relu(h @ w1 + b1)
    return (a * a) @ w2 + b2


def setup_inputs(seed: int = 0) -> dict:
    key = jax.random.key(seed)
    ks = iter(jax.random.split(key, 40))

    def nrm(shape, scale):
        return jax.random.normal(next(ks), shape, F32) * scale

    def gain(shape):
        return 1.0 + nrm(shape, 0.02)

    L = DEPTH
    return {
        "x": nrm((BATCH, SEQ, D_MODEL), 1.0),
        "c": nrm((BATCH, D_MODEL), 1.0),
        "ctx": nrm((BATCH, CTX_LEN, D_MODEL), 1.0),
        "c_ctx": nrm((D_MODEL,), 1.0),
        "w_mod": nrm((L, D_MODEL, 6 * D_MODEL), 0.5 * D_MODEL ** -0.5),
        "b_mod": nrm((L, 6 * D_MODEL), 0.02),
        "g_norm1": gain((L, D_MODEL)),
        "w_in": nrm((L, D_MODEL, IN_DIM), D_MODEL ** -0.5),
        "hy_conv_w": nrm((L, HY_SHORT, IN_HY), HY_SHORT ** -0.5),
        "hy_conv_b": nrm((L, IN_HY), 0.02),
        "hy_f_w1": nrm((L, HY_POS_DIM, HY_FILT_HID), HY_POS_DIM ** -0.5),
        "hy_f_b1": nrm((L, HY_FILT_HID), 0.02),
        "hy_f_w2": nrm((L, HY_FILT_HID, HY_FILT_HID), HY_FILT_HID ** -0.5),
        "hy_f_b2": nrm((L, HY_FILT_HID), 0.02),
        "hy_f_w3": nrm((L, HY_FILT_HID, HY_ORDER * 2 * HY_DIM), HY_FILT_HID ** -0.5),
        "hy_f_b3": nrm((L, HY_ORDER * 2 * HY_DIM), 0.02),
        "hy_freq": gain((L, 2, HY_FILT_HID)),
        "hy_bias": nrm((L, HY_ORDER, HY_DIM), 0.5),
        "na_g_q": gain((L, NA_HEAD_DIM)),
        "na_g_k": gain((L, NA_HEAD_DIM)),
        "na_rpb": nrm((L, NA_HEADS, 2 * NA_WIN_ROWS - 1, 2 * NA_WIN_COLS - 1), 0.02),
        "mla_g_qa": gain((L, MLA_Q_RANK)),
        "mla_g_kva": gain((L, MLA_KV_RANK)),
        "mla_w_q_up": nrm((L, MLA_Q_RANK, MLA_HEADS * MLA_QK), MLA_Q_RANK ** -0.5),
        "mla_w_kv_up": nrm((L, MLA_KV_RANK, MLA_HEADS * (MLA_NOPE + MLA_V)), MLA_KV_RANK ** -0.5),
        "mla_g_q": gain((L, MLA_QK)),
        "mla_g_k": gain((L, MLA_QK)),
        "w_out": nrm((L, MIX_DIM, D_MODEL), MIX_DIM ** -0.5),
        "g_norm2": gain((L, D_MODEL)),
        "w_ff1": nrm((L, D_MODEL, D_FF), D_MODEL ** -0.5),
        "b_ff1": nrm((L, D_FF), 0.02),
        "w_ff2": nrm((L, D_FF, D_MODEL), D_FF ** -0.5),
        "b_ff2": nrm((L, D_MODEL), 0.02),
    }


def reference(x, c, ctx, c_ctx, w_mod, b_mod, g_norm1, w_in, hy_conv_w, hy_conv_b, hy_f_w1, hy_f_b1,
              hy_f_w2, hy_f_b2, hy_f_w3, hy_f_b3, hy_freq, hy_bias, na_g_q, na_g_k, na_rpb, mla_g_qa,
              mla_g_kva, mla_w_q_up, mla_w_kv_up, mla_g_q, mla_g_k, w_out, g_norm2, w_ff1, b_ff1, w_ff2, b_ff2):
    S = x.shape[1]
    Lc = ctx.shape[1]
    rope = _axial_rope_tables(S)
    cx = ctx
    for i in range(DEPTH):
        last = i == DEPTH - 1
        mx = _modulation(c, w_mod[i], b_mod[i])
        mc = _modulation(c_ctx[None], w_mod[i], b_mod[i])
        px = _adaln(x, g_norm1[i], mx[0], mx[1]) @ w_in[i]
        pc = _adaln(cx, g_norm1[i], mc[0], mc[1]) @ w_in[i]
        px_hy, px_na, px_mla = jnp.split(px, [IN_HY, IN_HY + IN_NA], axis=-1)
        pc_hy, pc_na, pc_mla = jnp.split(pc, [IN_HY, IN_HY + IN_NA], axis=-1)

        qx, kx, vx = _na_qkv(px_na, na_g_q[i], na_g_k[i])
        qc, kc, vc = _na_qkv(pc_na, na_g_q[i], na_g_k[i])
        o_na = _na_latent(qx, kx, vx, kc, vc, na_rpb[i])

        mqx, mkx, mvx = _mla_qkv(px_mla, mla_g_qa[i], mla_g_kva[i], mla_w_q_up[i], mla_w_kv_up[i],
                                 mla_g_q[i], mla_g_k[i], rope)
        mqc, mkc, mvc = _mla_qkv(pc_mla, mla_g_qa[i], mla_g_kva[i], mla_w_q_up[i], mla_w_kv_up[i],
                                 mla_g_q[i], mla_g_k[i], None)
        o_mla = _mla_latent(mqx, mkx, mvx, mkc, mvc)

        filt_x = _hyena_filters(S, hy_f_w1[i], hy_f_b1[i], hy_f_w2[i], hy_f_b2[i], hy_f_w3[i], hy_f_b3[i], hy_freq[i])
        o_hy = _hyena(px_hy, hy_conv_w[i], hy_conv_b[i], filt_x, hy_bias[i])

        mix = jnp.concatenate([o_hy, _merge(o_na), _merge(o_mla)], axis=-1) @ w_out[i]
        x = x + mx[2] * mix
        x = x + mx[5] * _sqrelu_mlp(_adaln(x, g_norm2[i], mx[3], mx[4]), w_ff1[i], b_ff1[i], w_ff2[i], b_ff2[i])

        if not last:
            filt_c = _hyena_filters(Lc, hy_f_w1[i], hy_f_b1[i], hy_f_w2[i], hy_f_b2[i], hy_f_w3[i], hy_f_b3[i],
                                    hy_freq[i])
            oc_hy = _hyena(pc_hy, hy_conv_w[i], hy_conv_b[i], filt_c, hy_bias[i])
            oc_na = _dense_attend(qc, kc, vc)
            oc_mla = _dense_attend(mqc, mkc, mvc)
            mix_c = jnp.concatenate([oc_hy, _merge(oc_na), _merge(oc_mla)], axis=-1) @ w_out[i]
            cx = cx + mc[2] * mix_c
            cx = cx + mc[5] * _sqrelu_mlp(_adaln(cx, g_norm2[i], mc[3], mc[4]), w_ff1[i], b_ff1[i], w_ff2[i],
                                          b_ff2[i])
    return x
```

```python
import functools
import math

import jax
import jax.numpy as jnp
from jax import lax
from jax.experimental import pallas as pl
from jax.experimental.pallas import tpu as pltpu

F32 = jnp.float32
BF16 = jnp.bfloat16
HIGHEST = lax.Precision.HIGHEST

EPS = 1e-6
NEG_INF = -1e9
GRID_W = 64

HY_DIM = 256
HY_BANDS = 8
HY_DECAY_TARGET = 1e-2
HY_FAST = 0.3
HY_SLOW = 1.5

NA_HEADS = 4
NA_HEAD_DIM = 64
NA_DIM = NA_HEADS * NA_HEAD_DIM
NA_WIN_ROWS = 8
NA_WIN_COLS = 16

MLA_HEADS = 8
MLA_Q_RANK = 256
MLA_KV_RANK = 128
MLA_NOPE = 64
MLA_ROPE = 32
MLA_V = 64
MLA_QK = MLA_NOPE + MLA_ROPE
ROPE_BASE = 10000.0
ROPE_FREQS = MLA_ROPE // 4

VMEM_LIMIT_BYTES = 56 * 1024 * 1024


def _params(n_grid_dims):
    return pltpu.CompilerParams(dimension_semantics=("arbitrary",) * n_grid_dims,
                                vmem_limit_bytes=VMEM_LIMIT_BYTES)


def _tile(total, preferred):
    t = min(total, preferred)
    while total % t:
        t //= 2
    return t


def _resident(shape, index_map):
    return pl.BlockSpec(shape, index_map, pipeline_mode=pl.Buffered(1))


def _dot(a, b):
    return jnp.dot(a, b, preferred_element_type=F32)


def _dot_hi(a, b):
    return jnp.dot(a, b, preferred_element_type=F32, precision=HIGHEST)


def _dot_nt(a, b):
    return lax.dot_general(a, b, (((1,), (1,)), ((), ())), preferred_element_type=F32)


def _mod_kernel(cond_ref, w_ref, b_ref, o_ref):
    a = cond_ref[...]
    a = a / (1.0 + jnp.exp(-a))
    o_ref[0] = _dot_hi(a, w_ref[0]) + b_ref[0]


def _modulation(cond, w_mod, b_mod):
    depth, d, d6 = w_mod.shape
    n = cond.shape[0]
    tn = _tile(d6, 1536)
    return pl.pallas_call(
        _mod_kernel,
        grid=(depth, d6 // tn),
        in_specs=[pl.BlockSpec((n, d), lambda l, j: (0, 0)),
                  pl.BlockSpec((1, d, tn), lambda l, j: (l, 0, j)),
                  pl.BlockSpec((1, 1, tn), lambda l, j: (l, 0, j))],
        out_specs=pl.BlockSpec((1, n, tn), lambda l, j: (l, 0, j)),
        out_shape=jax.ShapeDtypeStruct((depth, n, d6), F32),
        compiler_params=_params(2),
    )(cond, w_mod, b_mod.reshape(depth, 1, d6))


def _adaln(x, g, shift, scale):
    y = x * lax.rsqrt(jnp.mean(x * x, axis=-1, keepdims=True) + EPS) * g
    return y * (1.0 + scale) + shift


def _inproj_kernel(x_ref, sh_ref, sc_ref, g_ref, w_hn_ref, w_m_ref, o_hn_ref, o_m_ref):
    h = _adaln(x_ref[0], g_ref[...], sh_ref[0, 0], sc_ref[0, 0]).astype(BF16)
    o_hn_ref[0] = _dot(h, w_hn_ref[...])
    o_m_ref[0] = _dot(h, w_m_ref[...])


def _inproj(x, mod, g, w_hn, w_m, tm):
    b, t, d = x.shape
    n_hn, n_m = w_hn.shape[1], w_m.shape[1]
    tm = _tile(t, tm)
    return pl.pallas_call(
        _inproj_kernel,
        grid=(b, t // tm),
        in_specs=[pl.BlockSpec((1, tm, d), lambda i, j: (i, j, 0)),
                  pl.BlockSpec((1, 1, 1, d), lambda i, j: (i, 0, 0, 0)),
                  pl.BlockSpec((1, 1, 1, d), lambda i, j: (i, 1, 0, 0)),
                  pl.BlockSpec((1, d), lambda i, j: (0, 0)),
                  _resident((d, n_hn), lambda i, j: (0, 0)),
                  _resident((d, n_m), lambda i, j: (0, 0))],
        out_specs=[pl.BlockSpec((1, tm, n_hn), lambda i, j: (i, j, 0)),
                   pl.BlockSpec((1, tm, n_m), lambda i, j: (i, j, 0))],
        out_shape=[jax.ShapeDtypeStruct((b, t, n_hn), F32),
                   jax.ShapeDtypeStruct((b, t, n_m), F32)],
        compiler_params=_params(2),
    )(x, mod, mod, g, w_hn, w_m)


def _sconv_kernel(p_ref, w_ref, b_ref, o_ref):
    u = p_ref[0]
    t = u.shape[0]
    row = lax.broadcasted_iota(jnp.int32, u.shape, 0)
    prev = jnp.where(row == 0, 0.0, pltpu.roll(u, 1, 0))
    nxt = jnp.where(row == t - 1, 0.0, pltpu.roll(u, t - 1, 0))
    w = w_ref[...]
    o_ref[0] = prev * w[0:1] + u * w[1:2] + nxt * w[2:3] + b_ref[...]


def _short_conv(p_hn, conv_w, conv_b):
    b, t, _ = p_hn.shape
    c = HY_DIM
    return pl.pallas_call(
        _sconv_kernel,
        grid=(b, 3),
        in_specs=[pl.BlockSpec((1, t, c), lambda i, j: (i, 0, j)),
                  pl.BlockSpec((3, c), lambda i, j: (0, j)),
                  pl.BlockSpec((1, c), lambda i, j: (0, j))],
        out_specs=pl.BlockSpec((1, t, c), lambda i, j: (i, 0, j)),
        out_shape=jax.ShapeDtypeStruct((b, t, 3 * c), F32),
        compiler_params=_params(2),
    )(p_hn, conv_w, conv_b)


def _dft_matrices(length):
    n = 2 * length
    f = jnp.arange(length, dtype=jnp.int32)[:, None]
    t = jnp.arange(length, dtype=jnp.int32)[None, :]
    ang = ((f * t) % n).astype(F32) * (2.0 * math.pi / n)
    cos_m = jnp.cos(ang)
    sin_m = jnp.sin(ang)
    nyq = (1 - 2 * (t % 2)).astype(F32)
    sin_m = jnp.where(f == 0, nyq, sin_m)
    fwd = jnp.concatenate([cos_m, sin_m], axis=0)
    wcol = jnp.where(jnp.arange(n) % length == 0, 1.0, 2.0).astype(F32) / n
    inv = fwd.T * wcol[None, :]
    return fwd.astype(BF16), inv.astype(BF16)


def _filt_kernel(z_ref, dec_ref, w1_ref, b1_ref, w2_ref, b2_ref, w3_ref, b3_ref, fr_ref, hsum_ref, hdiff_ref):
    z = z_ref[...]
    fr = fr_ref[0]
    a = jnp.sin(fr[0:1] * (_dot_hi(z, w1_ref[0]) + b1_ref[0]))
    a = jnp.sin(fr[1:2] * (_dot_hi(a, w2_ref[0]) + b2_ref[0]))
    h = _dot_hi(a, w3_ref[0]) + b3_ref[0]
    dec = dec_ref[...]
    c = HY_DIM
    row0 = lax.broadcasted_iota(jnp.int32, dec.shape, 0) == 0
    for o in range(2):
        hf = h[:, (2 * o) * c:(2 * o + 1) * c] * dec
        hb = h[:, (2 * o + 1) * c:(2 * o + 2) * c] * dec
        nrm = (jnp.sum(jnp.abs(hf), axis=0, keepdims=True)
               + jnp.sum(jnp.abs(hb), axis=0, keepdims=True) + EPS)
        hf = hf / nrm
        hb = jnp.where(row0, 0.0, hb / nrm)
        hsum_ref[0, :, o * c:(o + 1) * c] = hf + hb
        hdiff_ref[0, :, o * c:(o + 1) * c] = hf - hb


def _spec_kernel(hsum_ref, hdiff_ref, fwd_ref, hc_ref, hs_ref):
    length = hsum_ref.shape[1]
    hsum = hsum_ref[0]
    hdiff = hdiff_ref[0]

    def split_dot(m, v):
        hi = v.astype(BF16)
        lo = (v - hi.astype(F32)).astype(BF16)
        return _dot(m, hi) + _dot(m, lo)

    hc_ref[0] = split_dot(fwd_ref[:length], hsum)
    hs = split_dot(fwd_ref[length:], hdiff)
    row = lax.broadcasted_iota(jnp.int32, hsum.shape, 0)
    sign = (1 - 2 * (row % 2)).astype(F32)
    nyq = jnp.sum(sign * hsum, axis=0, keepdims=True)
    hs_ref[0] = jnp.where(row == 0, nyq, hs)


def _hyena_spectra(length, fwd, f_w1, f_b1, f_w2, f_b2, f_w3, f_b3, freq):
    depth = f_w1.shape[0]
    pos_dim = f_w1.shape[1]
    hid = f_w1.shape[2]
    kpad = 32
    t = jnp.linspace(0.0, 1.0, length, dtype=F32)[:, None]
    w = (2.0 * math.pi / length) * jnp.arange(length, dtype=F32)[:, None]
    bands = jnp.linspace(1e-4, HY_BANDS - 1, HY_BANDS, dtype=F32)
    z = jnp.concatenate([t, jnp.cos(bands * w), -jnp.sin(bands * w)], axis=-1)
    z = jnp.pad(z, ((0, 0), (0, kpad - pos_dim)))
    w1 = jnp.pad(f_w1, ((0, 0), (0, kpad - pos_dim), (0, 0)))
    deltas = jnp.linspace(math.log(HY_DECAY_TARGET) / HY_SLOW, math.log(HY_DECAY_TARGET) / HY_FAST,
                          HY_DIM, dtype=F32)
    decay = jnp.exp(-t * jnp.abs(deltas))
    c2 = 2 * HY_DIM
    c4 = 4 * HY_DIM
    lay3 = lambda l: (l, 0, 0)
    hsum, hdiff = pl.pallas_call(
        _filt_kernel,
        grid=(depth,),
        in_specs=[pl.BlockSpec((length, kpad), lambda l: (0, 0)),
                  pl.BlockSpec((length, HY_DIM), lambda l: (0, 0)),
                  pl.BlockSpec((1, kpad, hid), lay3),
                  pl.BlockSpec((1, 1, hid), lay3),
                  pl.BlockSpec((1, hid, hid), lay3),
                  pl.BlockSpec((1, 1, hid), lay3),
                  pl.BlockSpec((1, hid, c4), lay3),
                  pl.BlockSpec((1, 1, c4), lay3),
                  pl.BlockSpec((1, 2, hid), lay3)],
        out_specs=[pl.BlockSpec((1, length, c2), lay3),
                   pl.BlockSpec((1, length, c2), lay3)],
        out_shape=[jax.ShapeDtypeStruct((depth, length, c2), F32),
                   jax.ShapeDtypeStruct((depth, length, c2), F32)],
        compiler_params=_params(1),
    )(z, decay, w1, f_b1.reshape(depth, 1, hid), f_w2, f_b2.reshape(depth, 1, hid),
      f_w3, f_b3.reshape(depth, 1, c4), freq)
    c = HY_DIM
    blk = pl.BlockSpec((1, length, c), lambda l, o: (l, 0, o))
    return pl.pallas_call(
        _spec_kernel,
        grid=(depth, 2),
        in_specs=[blk, blk, _resident((2 * length, length), lambda l, o: (0, 0))],
        out_specs=[blk, blk],
        out_shape=[jax.ShapeDtypeStruct((depth, length, c2), F32),
                   jax.ShapeDtypeStruct((depth, length, c2), F32)],
        compiler_params=_params(2),
    )(hsum, hdiff, fwd)


def _hy_fwd_kernel(z_ref, fwd_ref, hc_ref, hs_ref, y_ref):
    length = z_ref.shape[1]
    x = _dot(fwd_ref[...], z_ref[0].astype(BF16))
    xc, xs = x[:length], x[length:]
    hc, hs = hc_ref[0], hs_ref[0]
    row0 = lax.broadcasted_iota(jnp.int32, xc.shape, 0) == 0
    ss = xs * hs
    y_ref[0, :length] = (xc * hc - jnp.where(row0, 0.0, ss)).astype(BF16)
    y_ref[0, length:] = jnp.where(row0, ss, xc * hs + xs * hc).astype(BF16)


def _hy_inv_kernel(y_ref, inv_ref, z_ref, gate_ref, bias_ref, o_ref):
    conv = _dot(inv_ref[...], y_ref[0])
    o_ref[0] = gate_ref[0] * (conv + z_ref[0] * bias_ref[0])


def _hyena_conv(z_arr, z_col, gate_arr, gate_col, fwd, inv, hc, hs, bias, layer, order):
    b, length, _ = z_arr.shape
    c = HY_DIM
    spec = pl.pallas_call(
        _hy_fwd_kernel,
        grid=(b,),
        in_specs=[pl.BlockSpec((1, length, c), lambda i: (i, 0, z_col)),
                  _resident((2 * length, length), lambda i: (0, 0)),
                  _resident((1, length, c), lambda i: (layer, 0, order)),
                  _resident((1, length, c), lambda i: (layer, 0, order))],
        out_specs=pl.BlockSpec((1, 2 * length, c), lambda i: (i, 0, 0)),
        out_shape=jax.ShapeDtypeStruct((b, 2 * length, c), BF16),
        compiler_params=_params(1),
    )(z_arr, fwd, hc, hs)
    return pl.pallas_call(
        _hy_inv_kernel,
        grid=(b,),
        in_specs=[pl.BlockSpec((1, 2 * length, c), lambda i: (i, 0, 0)),
                  _resident((length, 2 * length), lambda i: (0, 0)),
                  pl.BlockSpec((1, length, c), lambda i: (i, 0, z_col)),
                  pl.BlockSpec((1, length, c), lambda i: (i, 0, gate_col)),
                  pl.BlockSpec((1, 1, c), lambda i: (2 * layer + order, 0, 0))],
        out_specs=pl.BlockSpec((1, length, c), lambda i: (i, 0, 0)),
        out_shape=jax.ShapeDtypeStruct((b, length, c), F32),
        compiler_params=_params(1),
    )(spec, inv, z_arr, gate_arr, bias)


def _hyena(p_hn, conv_w, conv_b, fwd, inv, hc, hs, bias, layer):
    u = _short_conv(p_hn, conv_w, conv_b)
    z1 = _hyena_conv(u, 0, u, 1, fwd, inv, hc, hs, bias, layer, 0)
    return _hyena_conv(z1, 0, u, 2, fwd, inv, hc, hs, bias, layer, 1)


def _na_prep_kernel(q_ref, k_ref, v_ref, gq_ref, gk_ref, bd_ref, qo_ref, ko_ref, vo_ref):
    bd = bd_ref[...]

    def head_rms(x, g):
        ms = _dot_hi(x * x, bd)
        return x * lax.rsqrt(ms + EPS) * g

    q = (head_rms(q_ref[0], gq_ref[...]) * (NA_HEAD_DIM ** -0.5)).astype(BF16)
    k = head_rms(k_ref[0], gk_ref[...]).astype(BF16)
    v = v_ref[0].astype(BF16)
    for h in range(NA_HEADS):
        sl = slice(h * NA_HEAD_DIM, (h + 1) * NA_HEAD_DIM)
        qo_ref[0, h] = q[:, sl]
        ko_ref[0, h] = k[:, sl]
        vo_ref[0, h] = v[:, sl]


def _na_prep(p_hn, g_q, g_k, tm):
    b, t, _ = p_hn.shape
    tm = _tile(t, tm)
    gq = jnp.tile(g_q, NA_HEADS)[None, :]
    gk = jnp.tile(g_k, NA_HEADS)[None, :]
    head = jnp.arange(NA_DIM) // NA_HEAD_DIM
    bd = (head[:, None] == head[None, :]).astype(F32) / NA_HEAD_DIM
    out = jax.ShapeDtypeStruct((b, NA_HEADS, t, NA_HEAD_DIM), BF16)
    ospec = pl.BlockSpec((1, NA_HEADS, tm, NA_HEAD_DIM), lambda i, j: (i, 0, j, 0))
    return pl.pallas_call(
        _na_prep_kernel,
        grid=(b, t // tm),
        in_specs=[pl.BlockSpec((1, tm, NA_DIM), lambda i, j: (i, j, 3)),
                  pl.BlockSpec((1, tm, NA_DIM), lambda i, j: (i, j, 4)),
                  pl.BlockSpec((1, tm, NA_DIM), lambda i, j: (i, j, 5)),
                  pl.BlockSpec((1, NA_DIM), lambda i, j: (0, 0)),
                  pl.BlockSpec((1, NA_DIM), lambda i, j: (0, 0)),
                  pl.BlockSpec((NA_DIM, NA_DIM), lambda i, j: (0, 0))],
        out_specs=[ospec, ospec, ospec],
        out_shape=[out, out, out],
        compiler_params=_params(2),
    )(p_hn, p_hn, p_hn, gq, gk, bd)


def _na_bias_table(rpb, wr):
    depth, heads = rpb.shape[:2]
    oi = jnp.arange(NA_WIN_ROWS)[:, None, None, None]
    qc = jnp.arange(GRID_W)[None, :, None, None]
    ki = jnp.arange(wr)[None, None, :, None]
    kc = jnp.arange(GRID_W)[None, None, None, :]
    dr = jnp.clip(ki - oi + NA_WIN_ROWS - 1, 0, 2 * NA_WIN_ROWS - 2)
    dc = jnp.clip(kc - qc + NA_WIN_COLS - 1, 0, 2 * NA_WIN_COLS - 2)
    cstart = jnp.clip(qc - NA_WIN_COLS // 2, 0, GRID_W - NA_WIN_COLS)
    valid = (kc >= cstart) & (kc < cstart + NA_WIN_COLS)
    tab = rpb.astype(F32)[:, :, dr, dc]
    tab = jnp.where(valid[None, None], tab, NEG_INF)
    return tab.reshape(depth, heads, NA_WIN_ROWS, GRID_W, wr * GRID_W)


def _na_kernel(q_ref, k_ref, v_ref, kc_ref, vc_ref, b_ref, o_ref, *, rows, wr):
    r = pl.program_id(1)
    row0 = jnp.clip(r - NA_WIN_ROWS // 2, 0, rows - wr)
    start = pl.multiple_of(row0 * GRID_W, GRID_W)
    outs = []
    for h in range(NA_HEADS):
        q = q_ref[0, h]
        kl = k_ref[0, h, pl.ds(start, wr * GRID_W), :]
        vl = v_ref[0, h, pl.ds(start, wr * GRID_W), :]
        s1 = _dot_nt(q, kl) + b_ref[0, h, 0]
        s2 = _dot_nt(q, kc_ref[0, h])
        m = jnp.maximum(jnp.max(s1, axis=-1, keepdims=True), jnp.max(s2, axis=-1, keepdims=True))
        p1 = jnp.exp(s1 - m)
        p2 = jnp.exp(s2 - m)
        den = jnp.sum(p1, axis=-1, keepdims=True) + jnp.sum(p2, axis=-1, keepdims=True)
        o = _dot(p1.astype(BF16), vl) + _dot(p2.astype(BF16), vc_ref[0, h])
        outs.append(o / den)
    o_ref[0] = jnp.concatenate(outs, axis=-1)


def _na_latent(q, k, v, kc, vc, bias_tab, layer):
    b, heads, s, dh = q.shape
    ctx = kc.shape[2]
    rows = s // GRID_W
    wr = min(NA_WIN_ROWS, rows)

    def bias_index(i, r):
        row0 = jnp.clip(r - NA_WIN_ROWS // 2, 0, rows - wr)
        return (layer, 0, r - row0, 0, 0)

    full = lambda i, r: (i, 0, 0, 0)
    return pl.pallas_call(
        functools.partial(_na_kernel, rows=rows, wr=wr),
        grid=(b, rows),
        in_specs=[pl.BlockSpec((1, heads, GRID_W, dh), lambda i, r: (i, 0, r, 0)),
                  pl.BlockSpec((1, heads, s, dh), full),
                  pl.BlockSpec((1, heads, s, dh), full),
                  pl.BlockSpec((1, heads, ctx, dh), full),
                  pl.BlockSpec((1, heads, ctx, dh), full),
                  pl.BlockSpec((1, heads, 1, GRID_W, wr * GRID_W), bias_index)],
        out_specs=pl.BlockSpec((1, GRID_W, heads * dh), lambda i, r: (i, r, 0)),
        out_shape=jax.ShapeDtypeStruct((b, s, heads * dh), F32),
        compiler_params=_params(2),
    )(q, k, v, kc, vc, bias_tab)


def _attn_kernel(*refs, heads, two_sets):
    if two_sets:
        q_ref, k1_ref, v1_ref, k2_ref, v2_ref, o_ref = refs
    else:
        q_ref, k1_ref, v1_ref, o_ref = refs
    outs = []
    for h in range(heads):
        q = q_ref[0, h]
        s1 = _dot_nt(q, k1_ref[0, h])
        m = jnp.max(s1, axis=-1, keepdims=True)
        if two_sets:
            s2 = _dot_nt(q, k2_ref[0, h])
            m = jnp.maximum(m, jnp.max(s2, axis=-1, keepdims=True))
        p1 = jnp.exp(s1 - m)
        den = jnp.sum(p1, axis=-1, keepdims=True)
        o = _dot(p1.astype(BF16), v1_ref[0, h])
        if two_sets:
            p2 = jnp.exp(s2 - m)
            den = den + jnp.sum(p2, axis=-1, keepdims=True)
            o = o + _dot(p2.astype(BF16), v2_ref[0, h])
        outs.append(o / den)
    o_ref[0] = jnp.concatenate(outs, axis=-1)


def _attention(q, k1, v1, k2=None, v2=None, tq=512):
    b, heads, t, dq = q.shape
    dv = v1.shape[-1]
    tq = _tile(t, tq)
    two_sets = k2 is not None
    full = lambda i, j: (i, 0, 0, 0)
    in_specs = [pl.BlockSpec((1, heads, tq, dq), lambda i, j: (i, 0, j, 0)),
                pl.BlockSpec((1,) + k1.shape[1:], full),
                pl.BlockSpec((1,) + v1.shape[1:], full)]
    args = [q, k1, v1]
    if two_sets:
        in_specs += [pl.BlockSpec((1,) + k2.shape[1:], full),
                     pl.BlockSpec((1,) + v2.shape[1:], full)]
        args += [k2, v2]
    return pl.pallas_call(
        functools.partial(_attn_kernel, heads=heads, two_sets=two_sets),
        grid=(b, t // tq),
        in_specs=in_specs,
        out_specs=pl.BlockSpec((1, tq, heads * dv), lambda i, j: (i, j, 0)),
        out_shape=jax.ShapeDtypeStruct((b, t, heads * dv), F32),
        compiler_params=_params(2),
    )(*args)


def _mla_prep_kernel(*refs, rope):
    if rope:
        (p_ref, gqa_ref, gkva_ref, wqn_ref, wqr_ref, wqs_ref, wkn_ref, wv_ref,
         gq_ref, gk_ref, cos_ref, sin_ref, qo_ref, ko_ref, vo_ref) = refs
        cos_t, sin_t = cos_ref[...], sin_ref[...]
    else:
        (p_ref, gqa_ref, gkva_ref, wqn_ref, wqr_ref, wqs_ref, wkn_ref, wv_ref,
         gq_ref, gk_ref, qo_ref, ko_ref, vo_ref) = refs
    p = p_ref[0]
    a, b_, c = MLA_Q_RANK, MLA_Q_RANK + MLA_KV_RANK, MLA_Q_RANK + MLA_KV_RANK + MLA_ROPE
    cq, ckv, kr, krs = p[:, :a], p[:, a:b_], p[:, b_:c], p[:, c:c + MLA_ROPE]

    def rms(x, g):
        return x * lax.rsqrt(jnp.mean(x * x, axis=-1, keepdims=True) + EPS) * g

    cqn = rms(cq, gqa_ref[...]).astype(BF16)
    ckvn = rms(ckv, gkva_ref[...]).astype(BF16)
    gq, gk = gq_ref[...], gk_ref[...]
    gqn, gqr, gqs = gq[0:1], gq[1:2, :MLA_ROPE], gq[2:3, :MLA_ROPE]
    gkn, gkr, gks = gk[0:1], gk[1:2, :MLA_ROPE], gk[2:3, :MLA_ROPE]

    def finish(nope, rp, rps, g_n, g_r, g_s, post_scale):
        ss = jnp.sum(nope * nope, axis=-1, keepdims=True) + jnp.sum(rp * rp, axis=-1, keepdims=True)
        r = lax.rsqrt(ss * (1.0 / MLA_QK) + EPS) * post_scale
        x_r = rp * r * g_r
        if rope:
            x_r = x_r * cos_t + (rps * r * g_s) * sin_t
        return jnp.concatenate([nope * r * g_n, x_r], axis=-1).astype(BF16)

    for h in range(MLA_HEADS):
        qo_ref[0, h] = finish(_dot(cqn, wqn_ref[h]), _dot(cqn, wqr_ref[h]), _dot(cqn, wqs_ref[h]),
                              gqn, gqr, gqs, MLA_QK ** -0.5)
        ko_ref[0, h] = finish(_dot(ckvn, wkn_ref[h]), kr, krs, gkn, gkr, gks, 1.0)
        vo_ref[0, h] = _dot(ckvn, wv_ref[h]).astype(BF16)


_ROPE_SWAP = tuple(list(range(8, 16)) + list(range(0, 8)) + list(range(24, 32)) + list(range(16, 24)))


def _rope_tables(s):
    t = jnp.arange(s)
    pos = jnp.stack([t // GRID_W, t % GRID_W], axis=-1).astype(F32)
    inv = ROPE_BASE ** (-jnp.arange(ROPE_FREQS, dtype=F32) / ROPE_FREQS)
    ang = pos[:, :, None] * inv
    cos, sin = jnp.cos(ang), jnp.sin(ang)
    cos_t = jnp.concatenate([cos[:, 0], cos[:, 0], cos[:, 1], cos[:, 1]], axis=-1)
    sin_t = jnp.concatenate([-sin[:, 0], sin[:, 0], -sin[:, 1], sin[:, 1]], axis=-1)
    return cos_t, sin_t


def _mla_weights(w_q_up, w_kv_up, g_q, g_k):
    swap = jnp.array(_ROPE_SWAP)
    wq = w_q_up.reshape(MLA_Q_RANK, MLA_HEADS, MLA_QK).transpose(1, 0, 2).astype(BF16)
    wkv = w_kv_up.reshape(MLA_KV_RANK, MLA_HEADS, MLA_NOPE + MLA_V).transpose(1, 0, 2).astype(BF16)
    wqn, wqr = wq[..., :MLA_NOPE], wq[..., MLA_NOPE:]
    wqs = wqr[..., swap]
    wkn, wv = wkv[..., :MLA_NOPE], wkv[..., MLA_NOPE:]

    def gains(g):
        gr = g[MLA_NOPE:]
        pad = jnp.zeros((MLA_NOPE - MLA_ROPE,), F32)
        return jnp.stack([g[:MLA_NOPE], jnp.concatenate([gr, pad]), jnp.concatenate([gr[swap], pad])])

    return wqn, wqr, wqs, wkn, wv, gains(g_q), gains(g_k)


def _mla_prep(p_mla, g_qa, g_kva, weights, rope_tabs, tm):
    b, t, n = p_mla.shape
    tm = _tile(t, tm)
    wqn, wqr, wqs, wkn, wv, gq, gk = weights
    rope = rope_tabs is not None
    const2 = lambda i, j: (0, 0)
    const3 = lambda i, j: (0, 0, 0)
    in_specs = [pl.BlockSpec((1, tm, n), lambda i, j: (i, j, 0)),
                pl.BlockSpec((1, MLA_Q_RANK), const2),
                pl.BlockSpec((1, MLA_KV_RANK), const2),
                pl.BlockSpec(wqn.shape, const3), pl.BlockSpec(wqr.shape, const3),
                pl.BlockSpec(wqs.shape, const3), pl.BlockSpec(wkn.shape, const3),
                pl.BlockSpec(wv.shape, const3),
                pl.BlockSpec(gq.shape, const2), pl.BlockSpec(gk.shape, const2)]
    args = [p_mla, g_qa[None, :], g_kva[None, :], wqn, wqr, wqs, wkn, wv, gq, gk]
    if rope:
        in_specs += [pl.BlockSpec((tm, MLA_ROPE), lambda i, j: (j, 0)),
                     pl.BlockSpec((tm, MLA_ROPE), lambda i, j: (j, 0))]
        args += list(rope_tabs)
    qk = jax.ShapeDtypeStruct((b, MLA_HEADS, t, MLA_QK), BF16)
    vv = jax.ShapeDtypeStruct((b, MLA_HEADS, t, MLA_V), BF16)
    qk_spec = pl.BlockSpec((1, MLA_HEADS, tm, MLA_QK), lambda i, j: (i, 0, j, 0))
    v_spec = pl.BlockSpec((1, MLA_HEADS, tm, MLA_V), lambda i, j: (i, 0, j, 0))
    return pl.pallas_call(
        functools.partial(_mla_prep_kernel, rope=rope),
        grid=(b, t // tm),
        in_specs=in_specs,
        out_specs=[qk_spec, qk_spec, v_spec],
        out_shape=[qk, qk, vv],
        compiler_params=_params(2),
    )(*args)


def _outproj_kernel(hy_ref, na_ref, mla_ref, x_ref, gate_ref, w_hy_ref, w_na_ref, w_mla_ref, o_ref):
    mix = (_dot(hy_ref[0].astype(BF16), w_hy_ref[...])
           + _dot(na_ref[0].astype(BF16), w_na_ref[...])
           + _dot(mla_ref[0].astype(BF16), w_mla_ref[...]))
    o_ref[0] = x_ref[0] + gate_ref[0, 0] * mix


def _outproj(o_hy, o_na, o_mla, x, mod, w_hy, w_na, w_mla, tm):
    b, t, d = x.shape
    tm = _tile(t, tm)
    tok = lambda n: pl.BlockSpec((1, tm, n), lambda i, j: (i, j, 0))
    return pl.pallas_call(
        _outproj_kernel,
        grid=(b, t // tm),
        in_specs=[tok(o_hy.shape[-1]), tok(o_na.shape[-1]), tok(o_mla.shape[-1]), tok(d),
                  pl.BlockSpec((1, 1, 1, d), lambda i, j: (i, 2, 0, 0)),
                  _resident(w_hy.shape, lambda i, j: (0, 0)),
                  _resident(w_na.shape, lambda i, j: (0, 0)),
                  _resident(w_mla.shape, lambda i, j: (0, 0))],
        out_specs=tok(d),
        out_shape=jax.ShapeDtypeStruct((b, t, d), F32),
        compiler_params=_params(2),
    )(o_hy, o_na, o_mla, x, mod, w_hy, w_na, w_mla)


def _mlp_kernel(x_ref, sh_ref, sc_ref, gate_ref, g_ref, w1_ref, b1_ref, w2_ref, b2_ref, o_ref, *, chunk):
    x = x_ref[0]
    h = _adaln(x, g_ref[...], sh_ref[0, 0], sc_ref[0, 0]).astype(BF16)
    d_ff = w1_ref.shape[1]
    acc = jnp.zeros(x.shape, F32)
    for c0 in range(0, d_ff, chunk):
        a = jnp.maximum(_dot(h, w1_ref[:, c0:c0 + chunk]) + b1_ref[:, c0:c0 + chunk], 0.0)
        acc = acc + _dot((a * a).astype(BF16), w2_ref[c0:c0 + chunk, :])
    o_ref[0] = x + gate_ref[0, 0] * (acc + b2_ref[...])


def _mlp(x, mod, g, w1, b1, w2, b2, tm):
    b, t, d = x.shape
    d_ff = w1.shape[1]
    tm = _tile(t, tm)
    tok = pl.BlockSpec((1, tm, d), lambda i, j: (i, j, 0))
    modspec = lambda k: pl.BlockSpec((1, 1, 1, d), lambda i, j: (i, k, 0, 0))
    const = lambda i, j: (0, 0)
    return pl.pallas_call(
        functools.partial(_mlp_kernel, chunk=_tile(d_ff, 1024)),
        grid=(b, t // tm),
        in_specs=[tok, modspec(3), modspec(4), modspec(5),
                  pl.BlockSpec((1, d), const),
                  _resident((d, d_ff), const), pl.BlockSpec((1, d_ff), const),
                  _resident((d_ff, d), const), pl.BlockSpec((1, d), const)],
        out_specs=tok,
        out_shape=jax.ShapeDtypeStruct((b, t, d), F32),
        compiler_params=_params(2),
    )(x, mod, mod, mod, g, w1, b1, w2, b2)


def kernel(x, c, ctx, c_ctx, w_mod, b_mod, g_norm1, w_in, hy_conv_w, hy_conv_b, hy_f_w1, hy_f_b1, hy_f_w2, hy_f_b2, hy_f_w3, hy_f_b3, hy_freq, hy_bias, na_g_q, na_g_k, na_rpb, mla_g_qa, mla_g_kva, mla_w_q_up, mla_w_kv_up, mla_g_q, mla_g_k, w_out, g_norm2, w_ff1, b_ff1, w_ff2, b_ff2):
    b, s, d = x.shape
    lc = ctx.shape[1]
    depth = w_mod.shape[0]
    in_hn = 3 * HY_DIM + 3 * NA_DIM
    swap = jnp.array(_ROPE_SWAP)

    n_cond = -(-(b + 1) // 8) * 8
    cond = jnp.zeros((n_cond, d), F32).at[:b].set(c).at[b].set(c_ctx)
    mods = _modulation(cond, w_mod, b_mod).reshape(depth, n_cond, 6, 1, d)

    rope_tabs = _rope_tables(s)
    fwd_x, inv_x = _dft_matrices(s)
    fwd_c, inv_c = _dft_matrices(lc)
    filt = (hy_f_w1, hy_f_b1, hy_f_w2, hy_f_b2, hy_f_w3, hy_f_b3, hy_freq)
    hc_x, hs_x = _hyena_spectra(s, fwd_x, *filt)
    hc_c, hs_c = _hyena_spectra(lc, fwd_c, *filt)
    bias_tab = _na_bias_table(na_rpb, min(NA_WIN_ROWS, s // GRID_W))
    hy_bias = hy_bias.reshape(2 * depth, 1, HY_DIM)

    cx = ctx
    for i in range(depth):
        last = i == depth - 1
        mod_x = mods[i, :b]
        mod_c = jnp.broadcast_to(mods[i, b], (b, 6, 1, d))
        w_hn = w_in[i, :, :in_hn].astype(BF16)
        w_m = w_in[i, :, in_hn:]
        w_m = jnp.concatenate([w_m, w_m[:, MLA_Q_RANK + MLA_KV_RANK:][:, swap]], axis=-1).astype(BF16)
        g1 = g_norm1[i][None, :]
        px_hn, px_mla = _inproj(x, mod_x, g1, w_hn, w_m, 512)
        pc_hn, pc_mla = _inproj(cx, mod_c, g1, w_hn, w_m, 256)

        qx, kx, vx = _na_prep(px_hn, na_g_q[i], na_g_k[i], 512)
        qc, kc, vc = _na_prep(pc_hn, na_g_q[i], na_g_k[i], 256)
        o_na = _na_latent(qx, kx, vx, kc, vc, bias_tab, i)

        mla_w = _mla_weights(mla_w_q_up[i], mla_w_kv_up[i], mla_g_q[i], mla_g_k[i])
        mqx, mkx, mvx = _mla_prep(px_mla, mla_g_qa[i], mla_g_kva[i], mla_w, rope_tabs, 512)
        mqc, mkc, mvc = _mla_prep(pc_mla, mla_g_qa[i], mla_g_kva[i], mla_w, None, 256)
        o_mla = _attention(mqx, mkx, mvx, mkc, mvc)

        o_hy = _hyena(px_hn, hy_conv_w[i], hy_conv_b[i][None, :], fwd_x, inv_x, hc_x, hs_x, hy_bias, i)

        wo = w_out[i].astype(BF16)
        wo_hy, wo_na, wo_mla = wo[:HY_DIM], wo[HY_DIM:HY_DIM + NA_DIM], wo[HY_DIM + NA_DIM:]
        g2 = g_norm2[i][None, :]
        w1, w2 = w_ff1[i].astype(BF16), w_ff2[i].astype(BF16)
        b1, b2 = b_ff1[i][None, :], b_ff2[i][None, :]
        x = _outproj(o_hy, o_na, o_mla, x, mod_x, wo_hy, wo_na, wo_mla, 512)
        x = _mlp(x, mod_x, g2, w1, b1, w2, b2, 512)

        if not last:
            oc_hy = _hyena(pc_hn, hy_conv_w[i], hy_conv_b[i][None, :], fwd_c, inv_c, hc_c, hs_c, hy_bias, i)
            oc_na = _attention(qc, kc, vc)
            oc_mla = _attention(mqc, mkc, mvc)
            cx = _outproj(oc_hy, oc_na, oc_mla, cx, mod_c, wo_hy, wo_na, wo_mla, 256)
            cx = _mlp(cx, mod_c, g2, w1, b1, w2, b2, 256)
    return x
```

```python
import functools
import math

import jax
import jax.numpy as jnp
from jax import lax
from jax.experimental import pallas as pl
from jax.experimental.pallas import tpu as pltpu

F32 = jnp.float32
BF16 = jnp.bfloat16
HIGHEST = lax.Precision.HIGHEST

EPS = 1e-6
NEG_INF = -1e9
GRID_W = 64

HY_DIM = 256
HY_BANDS = 8
HY_DECAY_TARGET = 1e-2
HY_FAST = 0.3
HY_SLOW = 1.5

NA_HEADS = 4
NA_HEAD_DIM = 64
NA_DIM = NA_HEADS * NA_HEAD_DIM
NA_WIN_ROWS = 8
NA_WIN_COLS = 16

MLA_HEADS = 8
MLA_Q_RANK = 256
MLA_KV_RANK = 128
MLA_NOPE = 64
MLA_ROPE = 32
MLA_V = 64
MLA_QK = MLA_NOPE + MLA_ROPE
MLA_SLOT = 128
ROPE_BASE = 10000.0
ROPE_FREQS = MLA_ROPE // 4

VMEM_LIMIT_BYTES = 56 * 1024 * 1024


def _params(n_grid_dims):
    return pltpu.CompilerParams(dimension_semantics=("arbitrary",) * n_grid_dims,
                                vmem_limit_bytes=VMEM_LIMIT_BYTES)


def _tile(total, preferred):
    t = min(total, preferred)
    while total % t:
        t //= 2
    return t


def _resident(shape, index_map):
    return pl.BlockSpec(shape, index_map, pipeline_mode=pl.Buffered(1))


def _dot(a, b):
    return jnp.dot(a, b, preferred_element_type=F32)


def _dot_hi(a, b):
    return jnp.dot(a, b, preferred_element_type=F32, precision=HIGHEST)


def _dot_nt(a, b):
    return lax.dot_general(a, b, (((1,), (1,)), ((), ())), preferred_element_type=F32)


def _mod_kernel(cond_ref, w_ref, b_ref, o_ref):
    a = cond_ref[...]
    a = a / (1.0 + jnp.exp(-a))
    o_ref[0] = _dot_hi(a, w_ref[0]) + b_ref[0]


def _modulation(cond, w_mod, b_mod):
    depth, d, d6 = w_mod.shape
    n = cond.shape[0]
    tn = _tile(d6, 1536)
    return pl.pallas_call(
        _mod_kernel,
        grid=(depth, d6 // tn),
        in_specs=[pl.BlockSpec((n, d), lambda l, j: (0, 0)),
                  pl.BlockSpec((1, d, tn), lambda l, j: (l, 0, j)),
                  pl.BlockSpec((1, 1, tn), lambda l, j: (l, 0, j))],
        out_specs=pl.BlockSpec((1, n, tn), lambda l, j: (l, 0, j)),
        out_shape=jax.ShapeDtypeStruct((depth, n, d6), F32),
        compiler_params=_params(2),
    )(cond, w_mod, b_mod.reshape(depth, 1, d6))


def _adaln(x, g, shift, scale):
    y = x * lax.rsqrt(jnp.mean(x * x, axis=-1, keepdims=True) + EPS) * g
    return y * (1.0 + scale) + shift


def _inproj_kernel(x_ref, sh_ref, sc_ref, g_ref, w_hn_ref, w_m_ref, o_hn_ref, o_m_ref):
    h = _adaln(x_ref[0], g_ref[...], sh_ref[0, 0], sc_ref[0, 0]).astype(BF16)
    o_hn_ref[0] = _dot(h, w_hn_ref[...])
    o_m_ref[0] = _dot(h, w_m_ref[...])


def _inproj(x, mod, g, w_hn, w_m, tm):
    b, t, d = x.shape
    n_hn, n_m = w_hn.shape[1], w_m.shape[1]
    tm = _tile(t, tm)
    return pl.pallas_call(
        _inproj_kernel,
        grid=(b, t // tm),
        in_specs=[pl.BlockSpec((1, tm, d), lambda i, j: (i, j, 0)),
                  pl.BlockSpec((1, 1, 1, d), lambda i, j: (i, 0, 0, 0)),
                  pl.BlockSpec((1, 1, 1, d), lambda i, j: (i, 1, 0, 0)),
                  pl.BlockSpec((1, d), lambda i, j: (0, 0)),
                  _resident((d, n_hn), lambda i, j: (0, 0)),
                  _resident((d, n_m), lambda i, j: (0, 0))],
        out_specs=[pl.BlockSpec((1, tm, n_hn), lambda i, j: (i, j, 0)),
                   pl.BlockSpec((1, tm, n_m), lambda i, j: (i, j, 0))],
        out_shape=[jax.ShapeDtypeStruct((b, t, n_hn), F32),
                   jax.ShapeDtypeStruct((b, t, n_m), F32)],
        compiler_params=_params(2),
    )(x, mod, mod, g, w_hn, w_m)


def _sconv_kernel(p_ref, w_ref, b_ref, o_ref):
    u = p_ref[0]
    t = u.shape[0]
    row = lax.broadcasted_iota(jnp.int32, u.shape, 0)
    prev = jnp.where(row == 0, 0.0, pltpu.roll(u, 1, 0))
    nxt = jnp.where(row == t - 1, 0.0, pltpu.roll(u, t - 1, 0))
    w = w_ref[...]
    o_ref[0] = prev * w[0:1] + u * w[1:2] + nxt * w[2:3] + b_ref[...]


def _short_conv(p_hn, conv_w, conv_b):
    b, t, _ = p_hn.shape
    c = HY_DIM
    return pl.pallas_call(
        _sconv_kernel,
        grid=(b, 3),
        in_specs=[pl.BlockSpec((1, t, c), lambda i, j: (i, 0, j)),
                  pl.BlockSpec((3, c), lambda i, j: (0, j)),
                  pl.BlockSpec((1, c), lambda i, j: (0, j))],
        out_specs=pl.BlockSpec((1, t, c), lambda i, j: (i, 0, j)),
        out_shape=jax.ShapeDtypeStruct((b, t, 3 * c), F32),
        compiler_params=_params(2),
    )(p_hn, conv_w, conv_b)


def _dft_matrices(length):
    n = 2 * length
    f = jnp.arange(length, dtype=jnp.int32)[:, None]
    t = jnp.arange(length, dtype=jnp.int32)[None, :]
    ang = ((f * t) % n).astype(F32) * (2.0 * math.pi / n)
    cos_m = jnp.cos(ang)
    sin_m = jnp.sin(ang)
    nyq = (1 - 2 * (t % 2)).astype(F32)
    sin_m = jnp.where(f == 0, nyq, sin_m)
    fwd = jnp.concatenate([cos_m, sin_m], axis=0)
    wcol = jnp.where(jnp.arange(n) % length == 0, 1.0, 2.0).astype(F32) / n
    inv = fwd.T * wcol[None, :]
    return fwd.astype(BF16), inv.astype(BF16)


def _filt_kernel(z_ref, dec_ref, w1_ref, b1_ref, w2_ref, b2_ref, w3_ref, b3_ref, fr_ref, hsum_ref, hdiff_ref):
    z = z_ref[...]
    fr = fr_ref[0]
    a = jnp.sin(fr[0:1] * (_dot_hi(z, w1_ref[0]) + b1_ref[0]))
    a = jnp.sin(fr[1:2] * (_dot_hi(a, w2_ref[0]) + b2_ref[0]))
    h = _dot_hi(a, w3_ref[0]) + b3_ref[0]
    dec = dec_ref[...]
    c = HY_DIM
    row0 = lax.broadcasted_iota(jnp.int32, dec.shape, 0) == 0
    for o in range(2):
        hf = h[:, (2 * o) * c:(2 * o + 1) * c] * dec
        hb = h[:, (2 * o + 1) * c:(2 * o + 2) * c] * dec
        nrm = (jnp.sum(jnp.abs(hf), axis=0, keepdims=True)
               + jnp.sum(jnp.abs(hb), axis=0, keepdims=True) + EPS)
        hf = hf / nrm
        hb = jnp.where(row0, 0.0, hb / nrm)
        hsum_ref[0, :, o * c:(o + 1) * c] = hf + hb
        hdiff_ref[0, :, o * c:(o + 1) * c] = hf - hb


def _spec_kernel(hsum_ref, hdiff_ref, fwd_ref, hc_ref, hs_ref):
    length = hsum_ref.shape[1]
    hsum = hsum_ref[0]
    hdiff = hdiff_ref[0]

    def split_dot(m, v):
        hi = v.astype(BF16)
        lo = (v - hi.astype(F32)).astype(BF16)
        return _dot(m, hi) + _dot(m, lo)

    hc_ref[0] = split_dot(fwd_ref[:length], hsum)
    hs = split_dot(fwd_ref[length:], hdiff)
    row = lax.broadcasted_iota(jnp.int32, hsum.shape, 0)
    sign = (1 - 2 * (row % 2)).astype(F32)
    nyq = jnp.sum(sign * hsum, axis=0, keepdims=True)
    hs_ref[0] = jnp.where(row == 0, nyq, hs)


def _hyena_spectra(length, fwd, f_w1, f_b1, f_w2, f_b2, f_w3, f_b3, freq):
    depth = f_w1.shape[0]
    pos_dim = f_w1.shape[1]
    hid = f_w1.shape[2]
    kpad = 32
    t = jnp.linspace(0.0, 1.0, length, dtype=F32)[:, None]
    w = (2.0 * math.pi / length) * jnp.arange(length, dtype=F32)[:, None]
    bands = jnp.linspace(1e-4, HY_BANDS - 1, HY_BANDS, dtype=F32)
    z = jnp.concatenate([t, jnp.cos(bands * w), -jnp.sin(bands * w)], axis=-1)
    z = jnp.pad(z, ((0, 0), (0, kpad - pos_dim)))
    w1 = jnp.pad(f_w1, ((0, 0), (0, kpad - pos_dim), (0, 0)))
    deltas = jnp.linspace(math.log(HY_DECAY_TARGET) / HY_SLOW, math.log(HY_DECAY_TARGET) / HY_FAST,
                          HY_DIM, dtype=F32)
    decay = jnp.exp(-t * jnp.abs(deltas))
    c2 = 2 * HY_DIM
    c4 = 4 * HY_DIM
    lay3 = lambda l: (l, 0, 0)
    hsum, hdiff = pl.pallas_call(
        _filt_kernel,
        grid=(depth,),
        in_specs=[pl.BlockSpec((length, kpad), lambda l: (0, 0)),
                  pl.BlockSpec((length, HY_DIM), lambda l: (0, 0)),
                  pl.BlockSpec((1, kpad, hid), lay3),
                  pl.BlockSpec((1, 1, hid), lay3),
                  pl.BlockSpec((1, hid, hid), lay3),
                  pl.BlockSpec((1, 1, hid), lay3),
                  pl.BlockSpec((1, hid, c4), lay3),
                  pl.BlockSpec((1, 1, c4), lay3),
                  pl.BlockSpec((1, 2, hid), lay3)],
        out_specs=[pl.BlockSpec((1, length, c2), lay3),
                   pl.BlockSpec((1, length, c2), lay3)],
        out_shape=[jax.ShapeDtypeStruct((depth, length, c2), F32),
                   jax.ShapeDtypeStruct((depth, length, c2), F32)],
        compiler_params=_params(1),
    )(z, decay, w1, f_b1.reshape(depth, 1, hid), f_w2, f_b2.reshape(depth, 1, hid),
      f_w3, f_b3.reshape(depth, 1, c4), freq)
    c = HY_DIM
    blk = pl.BlockSpec((1, length, c), lambda l, o: (l, 0, o))
    return pl.pallas_call(
        _spec_kernel,
        grid=(depth, 2),
        in_specs=[blk, blk, _resident((2 * length, length), lambda l, o: (0, 0))],
        out_specs=[blk, blk],
        out_shape=[jax.ShapeDtypeStruct((depth, length, c2), F32),
                   jax.ShapeDtypeStruct((depth, length, c2), F32)],
        compiler_params=_params(2),
    )(hsum, hdiff, fwd)


def _hy_fwd_kernel(z_ref, fwd_ref, hc_ref, hs_ref, y_ref):
    length = z_ref.shape[1]
    x = _dot(fwd_ref[...], z_ref[0].astype(BF16))
    xc, xs = x[:length], x[length:]
    hc, hs = hc_ref[0], hs_ref[0]
    row0 = lax.broadcasted_iota(jnp.int32, xc.shape, 0) == 0
    ss = xs * hs
    y_ref[0, :length] = (xc * hc - jnp.where(row0, 0.0, ss)).astype(BF16)
    y_ref[0, length:] = jnp.where(row0, ss, xc * hs + xs * hc).astype(BF16)


def _hy_inv_kernel(y_ref, inv_ref, z_ref, gate_ref, bias_ref, o_ref):
    conv = _dot(inv_ref[...], y_ref[0])
    o_ref[0] = gate_ref[0] * (conv + z_ref[0] * bias_ref[0])


def _hyena_conv(z_arr, z_col, gate_arr, gate_col, fwd, inv, hc, hs, bias, layer, order):
    b, length, _ = z_arr.shape
    c = HY_DIM
    spec = pl.pallas_call(
        _hy_fwd_kernel,
        grid=(b,),
        in_specs=[pl.BlockSpec((1, length, c), lambda i: (i, 0, z_col)),
                  _resident((2 * length, length), lambda i: (0, 0)),
                  _resident((1, length, c), lambda i: (layer, 0, order)),
                  _resident((1, length, c), lambda i: (layer, 0, order))],
        out_specs=pl.BlockSpec((1, 2 * length, c), lambda i: (i, 0, 0)),
        out_shape=jax.ShapeDtypeStruct((b, 2 * length, c), BF16),
        compiler_params=_params(1),
    )(z_arr, fwd, hc, hs)
    return pl.pallas_call(
        _hy_inv_kernel,
        grid=(b,),
        in_specs=[pl.BlockSpec((1, 2 * length, c), lambda i: (i, 0, 0)),
                  _resident((length, 2 * length), lambda i: (0, 0)),
                  pl.BlockSpec((1, length, c), lambda i: (i, 0, z_col)),
                  pl.BlockSpec((1, length, c), lambda i: (i, 0, gate_col)),
                  pl.BlockSpec((1, 1, c), lambda i: (2 * layer + order, 0, 0))],
        out_specs=pl.BlockSpec((1, length, c), lambda i: (i, 0, 0)),
        out_shape=jax.ShapeDtypeStruct((b, length, c), F32),
        compiler_params=_params(1),
    )(spec, inv, z_arr, gate_arr, bias)


def _hyena(p_hn, conv_w, conv_b, fwd, inv, hc, hs, bias, layer):
    u = _short_conv(p_hn, conv_w, conv_b)
    z1 = _hyena_conv(u, 0, u, 1, fwd, inv, hc, hs, bias, layer, 0)
    return _hyena_conv(z1, 0, u, 2, fwd, inv, hc, hs, bias, layer, 1)


def _na_prep_kernel(q_ref, k_ref, v_ref, gq_ref, gk_ref, bd_ref, qo_ref, ko_ref, vo_ref):
    bd = bd_ref[...]

    def head_rms(x, g):
        ms = _dot_hi(x * x, bd)
        return x * lax.rsqrt(ms + EPS) * g

    q = (head_rms(q_ref[0], gq_ref[...]) * (NA_HEAD_DIM ** -0.5)).astype(BF16)
    k = head_rms(k_ref[0], gk_ref[...]).astype(BF16)
    for h in range(NA_HEADS):
        sl = slice(h * NA_HEAD_DIM, (h + 1) * NA_HEAD_DIM)
        qo_ref[0, h] = q[:, sl]
        ko_ref[0, h] = k[:, sl]
    vo_ref[0] = v_ref[0].T.astype(BF16)


def _na_prep(p_hn, g_q, g_k, tm):
    b, t, _ = p_hn.shape
    tm = _tile(t, tm)
    gq = jnp.tile(g_q, NA_HEADS)[None, :]
    gk = jnp.tile(g_k, NA_HEADS)[None, :]
    head = jnp.arange(NA_DIM) // NA_HEAD_DIM
    bd = (head[:, None] == head[None, :]).astype(F32) / NA_HEAD_DIM
    out = jax.ShapeDtypeStruct((b, NA_HEADS, t, NA_HEAD_DIM), BF16)
    ospec = pl.BlockSpec((1, NA_HEADS, tm, NA_HEAD_DIM), lambda i, j: (i, 0, j, 0))
    out_t = jax.ShapeDtypeStruct((b, NA_DIM, t), BF16)
    ospec_t = pl.BlockSpec((1, NA_DIM, tm), lambda i, j: (i, 0, j))
    return pl.pallas_call(
        _na_prep_kernel,
        grid=(b, t // tm),
        in_specs=[pl.BlockSpec((1, tm, NA_DIM), lambda i, j: (i, j, 3)),
                  pl.BlockSpec((1, tm, NA_DIM), lambda i, j: (i, j, 4)),
                  pl.BlockSpec((1, tm, NA_DIM), lambda i, j: (i, j, 5)),
                  pl.BlockSpec((1, NA_DIM), lambda i, j: (0, 0)),
                  pl.BlockSpec((1, NA_DIM), lambda i, j: (0, 0)),
                  pl.BlockSpec((NA_DIM, NA_DIM), lambda i, j: (0, 0))],
        out_specs=[ospec, ospec, ospec_t],
        out_shape=[out, out, out_t],
        compiler_params=_params(2),
    )(p_hn, p_hn, p_hn, gq, gk, bd)


NA_QROWS = 4
NA_KROWS = NA_QROWS + NA_WIN_ROWS


def _rpb_expand_kernel(r_ref, oh_ref, m_ref, o_ref):
    o_ref[...] = _dot_hi(r_ref[...], oh_ref[...]) + m_ref[...]


def _na_bias_table(rpb):
    depth, heads, n_dr, n_dc = rpb.shape
    kc = jnp.arange(GRID_W)[:, None]
    qc = jnp.arange(GRID_W)[None, :]
    dc = jnp.clip(kc - qc + NA_WIN_COLS - 1, 0, n_dc - 1)
    cstart = jnp.clip(qc - NA_WIN_COLS // 2, 0, GRID_W - NA_WIN_COLS)
    valid = ((kc >= cstart) & (kc < cstart + NA_WIN_COLS)).reshape(1, GRID_W * GRID_W)
    n_pad = 32
    onehot = ((dc.reshape(1, -1) == jnp.arange(n_pad)[:, None]) & valid).astype(F32)
    mask = jnp.where(valid, 0.0, NEG_INF).astype(F32)
    rows = depth * heads * n_dr
    rpb2 = jnp.pad(rpb.reshape(rows, n_dc).astype(F32), ((0, 0), (0, n_pad - n_dc)))
    t1 = pl.pallas_call(
        _rpb_expand_kernel,
        out_shape=jax.ShapeDtypeStruct((rows, GRID_W * GRID_W), F32),
        compiler_params=pltpu.CompilerParams(vmem_limit_bytes=VMEM_LIMIT_BYTES),
    )(rpb2, onehot, mask).reshape(depth, heads, n_dr, GRID_W, GRID_W)
    neg = jnp.full((depth, heads, GRID_W, GRID_W), NEG_INF, F32)
    layouts = [lambda j: (j, 0), lambda j: (NA_QROWS + j, j), lambda j: (2 * NA_QROWS + j, NA_QROWS)]
    tabs = []
    for lay in layouts:
        per_key_row = []
        for i in range(NA_KROWS):
            per_q = []
            for j in range(NA_QROWS):
                r_rel, row0_rel = lay(j)
                inside = row0_rel <= i < row0_rel + NA_WIN_ROWS
                per_q.append(t1[:, :, i - r_rel + NA_WIN_ROWS - 1] if inside else neg)
            per_key_row.append(jnp.stack(per_q, axis=3))
        tabs.append(jnp.stack(per_key_row, axis=2))
    tab = jnp.stack(tabs, axis=2)
    return tab.reshape(depth, heads, 3, NA_KROWS * GRID_W, NA_QROWS * GRID_W)


def _na_kernel(q_ref, k_ref, vt_ref, kc_ref, vct_ref, b_ref, o_ref, *, rows):
    rb = pl.program_id(1)
    w0 = jnp.clip(NA_QROWS * rb - NA_WIN_ROWS // 2, 0, rows - NA_KROWS)
    start = pl.multiple_of(w0 * GRID_W, NA_QROWS * GRID_W)
    nk = NA_KROWS * GRID_W
    outs = []
    for h in range(NA_HEADS):
        q = q_ref[0, h]
        hs = slice(h * NA_HEAD_DIM, (h + 1) * NA_HEAD_DIM)
        s1 = _dot_nt(k_ref[0, h, pl.ds(start, nk), :], q) + b_ref[0, h, 0]
        s2 = _dot_nt(kc_ref[0, h], q)
        m = jnp.maximum(jnp.max(s1, axis=0, keepdims=True), jnp.max(s2, axis=0, keepdims=True))
        p1 = jnp.exp(s1 - m)
        p2 = jnp.exp(s2 - m)
        den = jnp.sum(p1, axis=0, keepdims=True) + jnp.sum(p2, axis=0, keepdims=True)
        o = _dot(vt_ref[0, hs, pl.ds(start, nk)], p1.astype(BF16)) + _dot(vct_ref[0, hs, :], p2.astype(BF16))
        outs.append(o / den)
    o_ref[0] = jnp.concatenate(outs, axis=0).T


def _na_latent(q, k, vt, kc, vct, bias_tab, layer):
    b, heads, s, dh = q.shape
    ctx = kc.shape[2]
    rows = s // GRID_W
    assert rows % NA_QROWS == 0 and rows >= NA_KROWS
    nrb = rows // NA_QROWS
    tq = NA_QROWS * GRID_W

    def bias_index(i, rb):
        return (layer, 0, jnp.where(rb == 0, 0, jnp.where(rb == nrb - 1, 2, 1)), 0, 0)

    full4 = lambda i, rb: (i, 0, 0, 0)
    full3 = lambda i, rb: (i, 0, 0)
    return pl.pallas_call(
        functools.partial(_na_kernel, rows=rows),
        grid=(b, nrb),
        in_specs=[pl.BlockSpec((1, heads, tq, dh), lambda i, rb: (i, 0, rb, 0)),
                  pl.BlockSpec((1, heads, s, dh), full4),
                  pl.BlockSpec((1, heads * dh, s), full3),
                  pl.BlockSpec((1, heads, ctx, dh), full4),
                  pl.BlockSpec((1, heads * dh, ctx), full3),
                  pl.BlockSpec((1, heads, 1, NA_KROWS * GRID_W, tq), bias_index)],
        out_specs=pl.BlockSpec((1, tq, heads * dh), lambda i, rb: (i, rb, 0)),
        out_shape=jax.ShapeDtypeStruct((b, s, heads * dh), F32),
        compiler_params=_params(2),
    )(q, k, vt, kc, vct, bias_tab)


def _attn_kernel(*refs, heads, dv, two_sets):
    if two_sets:
        q_ref, k1_ref, vt1_ref, k2_ref, vt2_ref, o_ref = refs
    else:
        q_ref, k1_ref, vt1_ref, o_ref = refs
    outs = []
    for h in range(heads):
        q = q_ref[0, h]
        hs = slice(h * dv, (h + 1) * dv)
        s1 = _dot_nt(k1_ref[0, h], q)
        m = jnp.max(s1, axis=0, keepdims=True)
        if two_sets:
            s2 = _dot_nt(k2_ref[0, h], q)
            m = jnp.maximum(m, jnp.max(s2, axis=0, keepdims=True))
        p1 = jnp.exp(s1 - m)
        den = jnp.sum(p1, axis=0, keepdims=True)
        o = _dot(vt1_ref[0, hs, :], p1.astype(BF16))
        if two_sets:
            p2 = jnp.exp(s2 - m)
            den = den + jnp.sum(p2, axis=0, keepdims=True)
            o = o + _dot(vt2_ref[0, hs, :], p2.astype(BF16))
        outs.append(o / den)
    o_ref[0] = jnp.concatenate(outs, axis=0).T


def _attention(q, k1, vt1, k2=None, vt2=None, tq=512):
    b, heads, t, dq = q.shape
    dv = vt1.shape[1] // heads
    tq = _tile(t, tq)
    two_sets = k2 is not None
    full4 = lambda i, j: (i, 0, 0, 0)
    full3 = lambda i, j: (i, 0, 0)
    in_specs = [pl.BlockSpec((1, heads, tq, dq), lambda i, j: (i, 0, j, 0)),
                pl.BlockSpec((1,) + k1.shape[1:], full4),
                pl.BlockSpec((1,) + vt1.shape[1:], full3)]
    args = [q, k1, vt1]
    if two_sets:
        in_specs += [pl.BlockSpec((1,) + k2.shape[1:], full4),
                     pl.BlockSpec((1,) + vt2.shape[1:], full3)]
        args += [k2, vt2]
    return pl.pallas_call(
        functools.partial(_attn_kernel, heads=heads, dv=dv, two_sets=two_sets),
        grid=(b, t // tq),
        in_specs=in_specs,
        out_specs=pl.BlockSpec((1, tq, heads * dv), lambda i, j: (i, j, 0)),
        out_shape=jax.ShapeDtypeStruct((b, t, heads * dv), F32),
        compiler_params=_params(2),
    )(*args)


def _mla_prep_kernel(*refs, rope):
    if rope:
        (p_ref, gqa_ref, gkva_ref, wq_ref, wkn_ref, wv_ref, g_ref, wqs_ref, cos_ref, sin_ref,
         qo_ref, ko_ref, vto_ref) = refs
    else:
        p_ref, gqa_ref, gkva_ref, wq_ref, wkn_ref, wv_ref, g_ref, qo_ref, ko_ref, vto_ref = refs
    p = p_ref[0]
    a, b_ = MLA_Q_RANK, MLA_Q_RANK + MLA_KV_RANK
    cq, ckv = p[:, :a], p[:, a:b_]
    kr, krs = p[:, b_:b_ + MLA_SLOT], p[:, b_ + MLA_SLOT:b_ + 2 * MLA_SLOT]

    def rms(x, g):
        return x * lax.rsqrt(jnp.mean(x * x, axis=-1, keepdims=True) + EPS) * g

    cqn = rms(cq, gqa_ref[...]).astype(BF16)
    ckvn = rms(ckv, gkva_ref[...]).astype(BF16)
    qa = _dot(cqn, wq_ref[...])
    kn = _dot(ckvn, wkn_ref[...])
    g = g_ref[...]
    aq, ak = g[0:1], g[2:3]
    if rope:
        qs = _dot(cqn, wqs_ref[...])
        cos_t, sin_t = cos_ref[...], sin_ref[...]
        aq, bq = aq * cos_t, g[1:2] * sin_t
        ak, k_rot = ak * cos_t, krs * (g[3:4] * sin_t)

    def inv_rms(x):
        return lax.rsqrt(jnp.sum(x * x, axis=-1, keepdims=True) * (1.0 / MLA_QK) + EPS)

    for h in range(MLA_HEADS):
        sl = slice(h * MLA_SLOT, (h + 1) * MLA_SLOT)
        xq = qa[:, sl]
        yq = xq * aq
        if rope:
            yq = yq + qs[:, sl] * bq
        qo_ref[0, h] = (yq * (inv_rms(xq) * (MLA_QK ** -0.5))).astype(BF16)
        xk = kn[:, sl] + kr
        yk = xk * ak
        if rope:
            yk = yk + k_rot
        ko_ref[0, h] = (yk * inv_rms(xk)).astype(BF16)
    vto_ref[0] = _dot_nt(wv_ref[...], ckvn).astype(BF16)


_ROPE_SWAP = tuple(list(range(8, 16)) + list(range(0, 8)) + list(range(24, 32)) + list(range(16, 24)))


def _slot(nope, rope_part):
    lead = (nope if nope is not None else rope_part).shape[:-1]
    dt = (nope if nope is not None else rope_part).dtype
    z = lambda n: jnp.zeros(lead + (n,), dt)
    return jnp.concatenate([nope if nope is not None else z(MLA_NOPE),
                            rope_part if rope_part is not None else z(MLA_ROPE),
                            z(MLA_SLOT - MLA_QK)], axis=-1)


def _rope_tables(s):
    t = jnp.arange(s)
    pos = jnp.stack([t // GRID_W, t % GRID_W], axis=-1).astype(F32)
    inv = ROPE_BASE ** (-jnp.arange(ROPE_FREQS, dtype=F32) / ROPE_FREQS)
    ang = pos[:, :, None] * inv
    cos, sin = jnp.cos(ang), jnp.sin(ang)
    cos_t = jnp.concatenate([cos[:, 0], cos[:, 0], cos[:, 1], cos[:, 1]], axis=-1)
    sin_t = jnp.concatenate([-sin[:, 0], sin[:, 0], -sin[:, 1], sin[:, 1]], axis=-1)
    return _slot(jnp.ones((s, MLA_NOPE), F32), cos_t), _slot(None, sin_t)


def _mla_weights(w_q_up, w_kv_up, g_q, g_k):
    swap = jnp.array(_ROPE_SWAP)
    wq = w_q_up.reshape(MLA_Q_RANK, MLA_HEADS, MLA_QK)
    wkv = w_kv_up.reshape(MLA_KV_RANK, MLA_HEADS, MLA_NOPE + MLA_V)
    flat = lambda w: w.reshape(w.shape[0], -1).astype(BF16)
    wq_slot = flat(_slot(wq[..., :MLA_NOPE], wq[..., MLA_NOPE:]))
    wqs_slot = flat(_slot(None, wq[..., MLA_NOPE:][..., swap]))
    wkn_slot = flat(_slot(wkv[..., :MLA_NOPE], None))
    wv = flat(wkv[..., MLA_NOPE:]).T
    gains = jnp.stack([_slot(g_q[:MLA_NOPE], g_q[MLA_NOPE:]), _slot(None, g_q[MLA_NOPE:][swap]),
                       _slot(g_k[:MLA_NOPE], g_k[MLA_NOPE:]), _slot(None, g_k[MLA_NOPE:][swap])])
    return wq_slot, wkn_slot, wv, gains, wqs_slot


def _mla_prep(p_mla, g_qa, g_kva, weights, rope_tabs, tm):
    b, t, n = p_mla.shape
    tm = _tile(t, tm)
    wq_slot, wkn_slot, wv, gains, wqs_slot = weights
    rope = rope_tabs is not None
    const = lambda i, j: (0, 0)
    in_specs = [pl.BlockSpec((1, tm, n), lambda i, j: (i, j, 0)),
                pl.BlockSpec((1, MLA_Q_RANK), const),
                pl.BlockSpec((1, MLA_KV_RANK), const),
                pl.BlockSpec(wq_slot.shape, const), pl.BlockSpec(wkn_slot.shape, const),
                pl.BlockSpec(wv.shape, const), pl.BlockSpec(gains.shape, const)]
    args = [p_mla, g_qa[None, :], g_kva[None, :], wq_slot, wkn_slot, wv, gains]
    if rope:
        in_specs += [pl.BlockSpec(wqs_slot.shape, const),
                     pl.BlockSpec((tm, MLA_SLOT), lambda i, j: (j, 0)),
                     pl.BlockSpec((tm, MLA_SLOT), lambda i, j: (j, 0))]
        args += [wqs_slot] + list(rope_tabs)
    qk = jax.ShapeDtypeStruct((b, MLA_HEADS, t, MLA_SLOT), BF16)
    vt = jax.ShapeDtypeStruct((b, MLA_HEADS * MLA_V, t), BF16)
    qk_spec = pl.BlockSpec((1, MLA_HEADS, tm, MLA_SLOT), lambda i, j: (i, 0, j, 0))
    vt_spec = pl.BlockSpec((1, MLA_HEADS * MLA_V, tm), lambda i, j: (i, 0, j))
    return pl.pallas_call(
        functools.partial(_mla_prep_kernel, rope=rope),
        grid=(b, t // tm),
        in_specs=in_specs,
        out_specs=[qk_spec, qk_spec, vt_spec],
        out_shape=[qk, qk, vt],
        compiler_params=_params(2),
    )(*args)


def _outproj_kernel(hy_ref, na_ref, mla_ref, x_ref, gate_ref, w_hy_ref, w_na_ref, w_mla_ref, o_ref):
    mix = (_dot(hy_ref[0].astype(BF16), w_hy_ref[...])
           + _dot(na_ref[0].astype(BF16), w_na_ref[...])
           + _dot(mla_ref[0].astype(BF16), w_mla_ref[...]))
    o_ref[0] = x_ref[0] + gate_ref[0, 0] * mix


def _outproj(o_hy, o_na, o_mla, x, mod, w_hy, w_na, w_mla, tm):
    b, t, d = x.shape
    tm = _tile(t, tm)
    tok = lambda n: pl.BlockSpec((1, tm, n), lambda i, j: (i, j, 0))
    return pl.pallas_call(
        _outproj_kernel,
        grid=(b, t // tm),
        in_specs=[tok(o_hy.shape[-1]), tok(o_na.shape[-1]), tok(o_mla.shape[-1]), tok(d),
                  pl.BlockSpec((1, 1, 1, d), lambda i, j: (i, 2, 0, 0)),
                  _resident(w_hy.shape, lambda i, j: (0, 0)),
                  _resident(w_na.shape, lambda i, j: (0, 0)),
                  _resident(w_mla.shape, lambda i, j: (0, 0))],
        out_specs=tok(d),
        out_shape=jax.ShapeDtypeStruct((b, t, d), F32),
        compiler_params=_params(2),
    )(o_hy, o_na, o_mla, x, mod, w_hy, w_na, w_mla)


def _mlp_kernel(x_ref, sh_ref, sc_ref, gate_ref, g_ref, w1_ref, b1_ref, w2_ref, b2_ref, o_ref, *, chunk):
    x = x_ref[0]
    h = _adaln(x, g_ref[...], sh_ref[0, 0], sc_ref[0, 0]).astype(BF16)
    d_ff = w1_ref.shape[1]
    acc = jnp.zeros(x.shape, F32)
    for c0 in range(0, d_ff, chunk):
        a = jnp.maximum(_dot(h, w1_ref[:, c0:c0 + chunk]) + b1_ref[:, c0:c0 + chunk], 0.0)
        acc = acc + _dot((a * a).astype(BF16), w2_ref[c0:c0 + chunk, :])
    o_ref[0] = x + gate_ref[0, 0] * (acc + b2_ref[...])


def _mlp(x, mod, g, w1, b1, w2, b2, tm):
    b, t, d = x.shape
    d_ff = w1.shape[1]
    tm = _tile(t, tm)
    tok = pl.BlockSpec((1, tm, d), lambda i, j: (i, j, 0))
    modspec = lambda k: pl.BlockSpec((1, 1, 1, d), lambda i, j: (i, k, 0, 0))
    const = lambda i, j: (0, 0)
    return pl.pallas_call(
        functools.partial(_mlp_kernel, chunk=_tile(d_ff, 1024)),
        grid=(b, t // tm),
        in_specs=[tok, modspec(3), modspec(4), modspec(5),
                  pl.BlockSpec((1, d), const),
                  _resident((d, d_ff), const), pl.BlockSpec((1, d_ff), const),
                  _resident((d_ff, d), const), pl.BlockSpec((1, d), const)],
        out_specs=tok,
        out_shape=jax.ShapeDtypeStruct((b, t, d), F32),
        compiler_params=_params(2),
    )(x, mod, mod, mod, g, w1, b1, w2, b2)


def kernel(x, c, ctx, c_ctx, w_mod, b_mod, g_norm1, w_in, hy_conv_w, hy_conv_b, hy_f_w1, hy_f_b1, hy_f_w2, hy_f_b2, hy_f_w3, hy_f_b3, hy_freq, hy_bias, na_g_q, na_g_k, na_rpb, mla_g_qa, mla_g_kva, mla_w_q_up, mla_w_kv_up, mla_g_q, mla_g_k, w_out, g_norm2, w_ff1, b_ff1, w_ff2, b_ff2):
    b, s, d = x.shape
    lc = ctx.shape[1]
    depth = w_mod.shape[0]
    in_hn = 3 * HY_DIM + 3 * NA_DIM
    swap = jnp.array(_ROPE_SWAP)

    n_cond = -(-(b + 1) // 8) * 8
    cond = jnp.zeros((n_cond, d), F32).at[:b].set(c).at[b].set(c_ctx)
    mods = _modulation(cond, w_mod, b_mod).reshape(depth, n_cond, 6, 1, d)

    rope_tabs = _rope_tables(s)
    fwd_x, inv_x = _dft_matrices(s)
    fwd_c, inv_c = _dft_matrices(lc)
    filt = (hy_f_w1, hy_f_b1, hy_f_w2, hy_f_b2, hy_f_w3, hy_f_b3, hy_freq)
    hc_x, hs_x = _hyena_spectra(s, fwd_x, *filt)
    hc_c, hs_c = _hyena_spectra(lc, fwd_c, *filt)
    bias_tab = _na_bias_table(na_rpb)
    hy_bias = hy_bias.reshape(2 * depth, 1, HY_DIM)

    cx = ctx
    for i in range(depth):
        last = i == depth - 1
        mod_x = mods[i, :b]
        mod_c = jnp.broadcast_to(mods[i, b], (b, 6, 1, d))
        w_hn = w_in[i, :, :in_hn].astype(BF16)
        w_m = w_in[i, :, in_hn:]
        w_kr = w_m[:, MLA_Q_RANK + MLA_KV_RANK:]
        w_m = jnp.concatenate([w_m[:, :MLA_Q_RANK + MLA_KV_RANK], _slot(None, w_kr),
                               _slot(None, w_kr[:, swap])], axis=-1).astype(BF16)
        g1 = g_norm1[i][None, :]
        px_hn, px_mla = _inproj(x, mod_x, g1, w_hn, w_m, 512)
        pc_hn, pc_mla = _inproj(cx, mod_c, g1, w_hn, w_m, 256)

        qx, kx, vtx = _na_prep(px_hn, na_g_q[i], na_g_k[i], 512)
        qc, kc, vtc = _na_prep(pc_hn, na_g_q[i], na_g_k[i], 256)
        o_na = _na_latent(qx, kx, vtx, kc, vtc, bias_tab, i)

        mla_w = _mla_weights(mla_w_q_up[i], mla_w_kv_up[i], mla_g_q[i], mla_g_k[i])
        mqx, mkx, mvx = _mla_prep(px_mla, mla_g_qa[i], mla_g_kva[i], mla_w, rope_tabs, 512)
        mqc, mkc, mvc = _mla_prep(pc_mla, mla_g_qa[i], mla_g_kva[i], mla_w, None, 256)
        o_mla = _attention(mqx, mkx, mvx, mkc, mvc)

        o_hy = _hyena(px_hn, hy_conv_w[i], hy_conv_b[i][None, :], fwd_x, inv_x, hc_x, hs_x, hy_bias, i)

        wo = w_out[i].astype(BF16)
        wo_hy, wo_na, wo_mla = wo[:HY_DIM], wo[HY_DIM:HY_DIM + NA_DIM], wo[HY_DIM + NA_DIM:]
        g2 = g_norm2[i][None, :]
        w1, w2 = w_ff1[i].astype(BF16), w_ff2[i].astype(BF16)
        b1, b2 = b_ff1[i][None, :], b_ff2[i][None, :]
        x = _outproj(o_hy, o_na, o_mla, x, mod_x, wo_hy, wo_na, wo_mla, 512)
        x = _mlp(x, mod_x, g2, w1, b1, w2, b2, 512)

        if not last:
            oc_hy = _hyena(pc_hn, hy_conv_w[i], hy_conv_b[i][None, :], fwd_c, inv_c, hc_c, hs_c, hy_bias, i)
            oc_na = _attention(qc, kc, vtc)
            oc_mla = _attention(mqc, mkc, mvc)
            cx = _outproj(oc_hy, oc_na, oc_mla, cx, mod_c, wo_hy, wo_na, wo_mla, 256)
            cx = _mlp(cx, mod_c, g2, w1, b1, w2, b2, 256)
    return x
```

```python
import functools
import math

import jax
import jax.numpy as jnp
from jax import lax
from jax.experimental import pallas as pl
from jax.experimental.pallas import tpu as pltpu

F32 = jnp.float32
BF16 = jnp.bfloat16
HIGHEST = lax.Precision.HIGHEST

EPS = 1e-6
NEG_INF = -1e9
GRID_W = 64

HY_DIM = 256
HY_BANDS = 8
HY_DECAY_TARGET = 1e-2
HY_FAST = 0.3
HY_SLOW = 1.5

NA_HEADS = 4
NA_HEAD_DIM = 64
NA_DIM = NA_HEADS * NA_HEAD_DIM
NA_WIN_ROWS = 8
NA_WIN_COLS = 16

MLA_HEADS = 8
MLA_Q_RANK = 256
MLA_KV_RANK = 128
MLA_NOPE = 64
MLA_ROPE = 32
MLA_V = 64
MLA_QK = MLA_NOPE + MLA_ROPE
MLA_SLOT = 128
ROPE_BASE = 10000.0
ROPE_FREQS = MLA_ROPE // 4

VMEM_LIMIT_BYTES = 56 * 1024 * 1024
HYENA_VMEM_LIMIT_BYTES = 60 * 1024 * 1024

def _params(n_grid_dims):
    return pltpu.CompilerParams(dimension_semantics=("arbitrary",) * n_grid_dims,
                                vmem_limit_bytes=VMEM_LIMIT_BYTES)


def _tile(total, preferred):
    t = min(total, preferred)
    while total % t:
        t //= 2
    return t


def _resident(shape, index_map):
    return pl.BlockSpec(shape, index_map, pipeline_mode=pl.Buffered(1))


def _dot(a, b):
    return jnp.dot(a, b, preferred_element_type=F32)


def _dot_hi(a, b):
    return jnp.dot(a, b, preferred_element_type=F32, precision=HIGHEST)


def _dot_nt(a, b):
    return lax.dot_general(a, b, (((1,), (1,)), ((), ())), preferred_element_type=F32)


def _mod_kernel(cond_ref, w_ref, b_ref, o_ref):
    a = cond_ref[...]
    a = a / (1.0 + jnp.exp(-a))
    o_ref[0] = _dot_hi(a, w_ref[0]) + b_ref[0]


def _modulation(cond, w_mod, b_mod):
    depth, d, d6 = w_mod.shape
    n = cond.shape[0]
    tn = _tile(d6, 1536)
    return pl.pallas_call(
        _mod_kernel,
        grid=(depth, d6 // tn),
        in_specs=[pl.BlockSpec((n, d), lambda l, j: (0, 0)),
                  pl.BlockSpec((1, d, tn), lambda l, j: (l, 0, j)),
                  pl.BlockSpec((1, 1, tn), lambda l, j: (l, 0, j))],
        out_specs=pl.BlockSpec((1, n, tn), lambda l, j: (l, 0, j)),
        out_shape=jax.ShapeDtypeStruct((depth, n, d6), F32),
        compiler_params=_params(2),
    )(cond, w_mod, b_mod.reshape(depth, 1, d6))


def _adaln(x, g, shift, scale):
    y = x * lax.rsqrt(jnp.mean(x * x, axis=-1, keepdims=True) + EPS) * g
    return y * (1.0 + scale) + shift


def _inproj_kernel(x_ref, sh_ref, sc_ref, g_ref, w_hy_ref, w_na_ref, w_m_ref, o_hy_ref, o_na_ref, o_m_ref):
    h = _adaln(x_ref[0], g_ref[...], sh_ref[0, 0], sc_ref[0, 0]).astype(BF16)
    o_hy_ref[0] = _dot(h, w_hy_ref[...]).astype(BF16)
    o_na_ref[0] = _dot(h, w_na_ref[...])
    o_m_ref[0] = _dot(h, w_m_ref[...])


def _inproj(x, mod, g, w_hy, w_na, w_m, tm):
    b, t, d = x.shape
    tm = _tile(t, tm)
    widths = (w_hy.shape[1], w_na.shape[1], w_m.shape[1])
    return pl.pallas_call(
        _inproj_kernel,
        grid=(b, t // tm),
        in_specs=[pl.BlockSpec((1, tm, d), lambda i, j: (i, j, 0)),
                  pl.BlockSpec((1, 1, 1, d), lambda i, j: (i, 0, 0, 0)),
                  pl.BlockSpec((1, 1, 1, d), lambda i, j: (i, 1, 0, 0)),
                  pl.BlockSpec((1, d), lambda i, j: (0, 0))]
                 + [_resident((d, n), lambda i, j: (0, 0)) for n in widths],
        out_specs=[pl.BlockSpec((1, tm, n), lambda i, j: (i, j, 0)) for n in widths],
        out_shape=[jax.ShapeDtypeStruct((b, t, n), dt) for n, dt in zip(widths, (BF16, F32, F32))],
        compiler_params=_params(2),
    )(x, mod, mod, g, w_hy, w_na, w_m)


def _dft_matrix(length):
    n = 2 * length
    f = jnp.arange(length, dtype=jnp.int32)[:, None]
    t = jnp.arange(length, dtype=jnp.int32)[None, :]
    ang = ((f * t) % n).astype(F32) * (2.0 * math.pi / n)
    return jnp.concatenate([jnp.cos(ang), jnp.sin(ang)], axis=1).astype(BF16)


def _filt_kernel(z_ref, dec_ref, w1_ref, b1_ref, w2_ref, b2_ref, w3_ref, b3_ref, fr_ref, hsum_ref, hdiff_ref):
    z = z_ref[...]
    fr = fr_ref[0]
    a = jnp.sin(fr[0:1] * (_dot_hi(z, w1_ref[0]) + b1_ref[0]))
    a = jnp.sin(fr[1:2] * (_dot_hi(a, w2_ref[0]) + b2_ref[0]))
    h = _dot_hi(a, w3_ref[0]) + b3_ref[0]
    dec = dec_ref[...]
    c = HY_DIM
    row0 = lax.broadcasted_iota(jnp.int32, dec.shape, 0) == 0
    for o in range(2):
        hf = h[:, (2 * o) * c:(2 * o + 1) * c] * dec
        hb = h[:, (2 * o + 1) * c:(2 * o + 2) * c] * dec
        nrm = (jnp.sum(jnp.abs(hf), axis=0, keepdims=True)
               + jnp.sum(jnp.abs(hb), axis=0, keepdims=True) + EPS)
        hf = hf / nrm
        hb = jnp.where(row0, 0.0, hb / nrm)
        hsum_ref[0, :, o * c:(o + 1) * c] = hf + hb
        hdiff_ref[0, :, o * c:(o + 1) * c] = hf - hb


def _alternating_sign(shape):
    row = lax.broadcasted_iota(jnp.int32, shape, 0)
    return (1 - 2 * (row % 2)).astype(F32)


def _spec_kernel(hsum_ref, hdiff_ref, cs_ref, hc_ref, hs_ref, hn_ref):
    length = hsum_ref.shape[1]
    hsum = hsum_ref[0]
    hdiff = hdiff_ref[0]

    def split_dot(m, v):
        hi = v.astype(BF16)
        lo = (v - hi.astype(F32)).astype(BF16)
        return _dot(m, hi) + _dot(m, lo)

    row = lax.broadcasted_iota(jnp.int32, hsum.shape, 0)
    wf = jnp.where(row == 0, 0.5 / length, 1.0 / length)
    hc_ref[0] = split_dot(cs_ref[:, :length], hsum) * wf
    hs_ref[0] = split_dot(cs_ref[:, length:], hdiff) * wf
    hn_ref[0] = jnp.sum(_alternating_sign(hsum.shape) * hsum, axis=0, keepdims=True) * (0.5 / length)


def _hyena_spectra(length, cs, f_w1, f_b1, f_w2, f_b2, f_w3, f_b3, freq):
    depth = f_w1.shape[0]
    pos_dim = f_w1.shape[1]
    hid = f_w1.shape[2]
    kpad = 32
    t = jnp.linspace(0.0, 1.0, length, dtype=F32)[:, None]
    w = (2.0 * math.pi / length) * jnp.arange(length, dtype=F32)[:, None]
    bands = jnp.linspace(1e-4, HY_BANDS - 1, HY_BANDS, dtype=F32)
    z = jnp.concatenate([t, jnp.cos(bands * w), -jnp.sin(bands * w)], axis=-1)
    z = jnp.pad(z, ((0, 0), (0, kpad - pos_dim)))
    w1 = jnp.pad(f_w1, ((0, 0), (0, kpad - pos_dim), (0, 0)))
    deltas = jnp.linspace(math.log(HY_DECAY_TARGET) / HY_SLOW, math.log(HY_DECAY_TARGET) / HY_FAST,
                          HY_DIM, dtype=F32)
    decay = jnp.exp(-t * jnp.abs(deltas))
    c2 = 2 * HY_DIM
    c4 = 4 * HY_DIM
    lay3 = lambda l: (l, 0, 0)
    hsum, hdiff = pl.pallas_call(
        _filt_kernel,
        grid=(depth,),
        in_specs=[pl.BlockSpec((length, kpad), lambda l: (0, 0)),
                  pl.BlockSpec((length, HY_DIM), lambda l: (0, 0)),
                  pl.BlockSpec((1, kpad, hid), lay3),
                  pl.BlockSpec((1, 1, hid), lay3),
                  pl.BlockSpec((1, hid, hid), lay3),
                  pl.BlockSpec((1, 1, hid), lay3),
                  pl.BlockSpec((1, hid, c4), lay3),
                  pl.BlockSpec((1, 1, c4), lay3),
                  pl.BlockSpec((1, 2, hid), lay3)],
        out_specs=[pl.BlockSpec((1, length, c2), lay3),
                   pl.BlockSpec((1, length, c2), lay3)],
        out_shape=[jax.ShapeDtypeStruct((depth, length, c2), F32),
                   jax.ShapeDtypeStruct((depth, length, c2), F32)],
        compiler_params=_params(1),
    )(z, decay, w1, f_b1.reshape(depth, 1, hid), f_w2, f_b2.reshape(depth, 1, hid),
      f_w3, f_b3.reshape(depth, 1, c4), freq)
    c = HY_DIM
    blk = pl.BlockSpec((1, length, c), lambda l, o: (l, 0, o))
    return pl.pallas_call(
        _spec_kernel,
        grid=(depth, 2),
        in_specs=[blk, blk, _resident((length, 2 * length), lambda l, o: (0, 0))],
        out_specs=[blk, blk, pl.BlockSpec((1, 1, c), lambda l, o: (l, 0, o))],
        out_shape=[jax.ShapeDtypeStruct((depth, length, c2), F32),
                   jax.ShapeDtypeStruct((depth, length, c2), F32),
                   jax.ShapeDtypeStruct((depth, 1, c2), F32)],
        compiler_params=_params(2),
    )(hsum, hdiff, cs)


HY_ROWS = 256
HY_HALO = 8


HY_PAIR = 2


def _hyena_kernel(pv_ref, pg_ref, wv_ref, bv_ref, wg_ref, bg_ref, cs_ref, hc_ref, hs_ref, hn_ref, bias_ref,
                  o_ref, z_scr, xc_scr, xs_scr, yc_scr, ys_scr):
    length = pv_ref.shape[1]
    c = HY_DIM
    rows = min(HY_ROWS, length)
    chunks = [(r0, rows) for r0 in range(0, length, rows)]
    halves = [(i, slice(i * c, (i + 1) * c)) for i in range(HY_PAIR)]

    def short_conv_rows(p_ref, w_ref, b_ref, i, r0):
        lo = max(r0 - HY_HALO, 0)
        hi = min(r0 + rows + HY_HALO, length)
        u = p_ref[i, lo:hi, :].astype(F32)
        row = lax.broadcasted_iota(jnp.int32, u.shape, 0) + lo
        prev = jnp.where(row == 0, 0.0, pltpu.roll(u, 1, 0))
        nxt = jnp.where(row == length - 1, 0.0, pltpu.roll(u, hi - lo - 1, 0))
        y = prev * w_ref[0:1, :] + u * w_ref[1:2, :] + nxt * w_ref[2:3, :] + b_ref[...]
        return y[r0 - lo:r0 - lo + rows]

    @pl.when(pl.program_id(1) == 0)
    def _():
        for r0, n in chunks:
            for i, lanes in halves:
                z_scr[r0:r0 + n, lanes] = short_conv_rows(pv_ref, wv_ref, bv_ref, i, r0)

    sign = _alternating_sign((rows, c))
    zb = z_scr[...].astype(BF16)
    xc_scr[...] = _dot(cs_ref[:, :length], zb)
    xs_scr[...] = _dot(cs_ref[:, length:], zb)
    xn = [jnp.zeros((1, c), F32) for _ in halves]
    for r0, n in chunks:
        hc, hs = hc_ref[0, r0:r0 + n, :], hs_ref[0, r0:r0 + n, :]
        for i, lanes in halves:
            xc, xs = xc_scr[r0:r0 + n, lanes], xs_scr[r0:r0 + n, lanes]
            yc_scr[r0:r0 + n, lanes] = (xc * hc - xs * hs).astype(BF16)
            ys_scr[r0:r0 + n, lanes] = (xc * hs + xs * hc).astype(BF16)
            xn[i] = xn[i] + jnp.sum(sign * z_scr[r0:r0 + n, lanes], axis=0, keepdims=True)
    xc_scr[...] = _dot(cs_ref[:, :length], yc_scr[...])
    xs_scr[...] = _dot(cs_ref[:, length:], ys_scr[...])
    bias = bias_ref[0]
    for r0, n in chunks:
        for i, lanes in halves:
            z = z_scr[r0:r0 + n, lanes]
            conv = xc_scr[r0:r0 + n, lanes] + xs_scr[r0:r0 + n, lanes] + sign * (xn[i] * hn_ref[0])
            z = short_conv_rows(pg_ref, wg_ref, bg_ref, i, r0) * (conv + z * bias)
            z_scr[r0:r0 + n, lanes] = z
            o_ref[i, r0:r0 + n, :] = z


def _hyena(p_hy, conv_w, conv_b, cs, hc, hs, hn, bias, layer):
    b, length, _ = p_hy.shape
    assert b % HY_PAIR == 0
    c = HY_DIM
    wide = HY_PAIR * c
    return pl.pallas_call(
        _hyena_kernel,
        grid=(b // HY_PAIR, 2),
        in_specs=[pl.BlockSpec((HY_PAIR, length, c), lambda i, o: (i, 0, 0), pipeline_mode=pl.Buffered(1)),
                  pl.BlockSpec((HY_PAIR, length, c), lambda i, o: (i, 0, o + 1)),
                  pl.BlockSpec((3, c), lambda i, o: (0, 0)),
                  pl.BlockSpec((1, c), lambda i, o: (0, 0)),
                  pl.BlockSpec((3, c), lambda i, o: (0, o + 1)),
                  pl.BlockSpec((1, c), lambda i, o: (0, o + 1)),
                  _resident((length, 2 * length), lambda i, o: (0, 0)),
                  pl.BlockSpec((1, length, c), lambda i, o: (layer, 0, o), pipeline_mode=pl.Buffered(1)),
                  pl.BlockSpec((1, length, c), lambda i, o: (layer, 0, o), pipeline_mode=pl.Buffered(1)),
                  pl.BlockSpec((1, 1, c), lambda i, o: (layer, 0, o)),
                  pl.BlockSpec((1, 1, c), lambda i, o: (2 * layer + o, 0, 0))],
        out_specs=pl.BlockSpec((HY_PAIR, length, c), lambda i, o: (i, 0, 0)),
        out_shape=jax.ShapeDtypeStruct((b, length, c), F32),
        scratch_shapes=[pltpu.VMEM((length, wide), F32), pltpu.VMEM((length, wide), F32),
                        pltpu.VMEM((length, wide), F32), pltpu.VMEM((length, wide), BF16),
                        pltpu.VMEM((length, wide), BF16)],
        compiler_params=pltpu.CompilerParams(dimension_semantics=("arbitrary", "arbitrary"),
                                             vmem_limit_bytes=HYENA_VMEM_LIMIT_BYTES),
    )(p_hy, p_hy, conv_w, conv_b, conv_w, conv_b, cs, hc, hs, hn, bias)


def _na_prep_kernel(q_ref, k_ref, v_ref, gq_ref, gk_ref, bd_ref, qo_ref, ko_ref, vo_ref):
    bd = bd_ref[...]

    def head_rms(x, g):
        ms = _dot_hi(x * x, bd)
        return x * lax.rsqrt(ms + EPS) * g

    q = (head_rms(q_ref[0], gq_ref[...]) * (NA_HEAD_DIM ** -0.5 * LOG2E)).astype(BF16)
    k = head_rms(k_ref[0], gk_ref[...]).astype(BF16)
    for h in range(NA_HEADS):
        sl = slice(h * NA_HEAD_DIM, (h + 1) * NA_HEAD_DIM)
        qo_ref[0, h] = q[:, sl]
        ko_ref[0, h] = k[:, sl]
    vo_ref[0] = v_ref[0].T.astype(BF16)


def _na_prep(p_hn, g_q, g_k, tm):
    b, t, _ = p_hn.shape
    tm = _tile(t, tm)
    gq = jnp.tile(g_q, NA_HEADS)[None, :]
    gk = jnp.tile(g_k, NA_HEADS)[None, :]
    head = jnp.arange(NA_DIM) // NA_HEAD_DIM
    bd = (head[:, None] == head[None, :]).astype(F32) / NA_HEAD_DIM
    out = jax.ShapeDtypeStruct((b, NA_HEADS, t, NA_HEAD_DIM), BF16)
    ospec = pl.BlockSpec((1, NA_HEADS, tm, NA_HEAD_DIM), lambda i, j: (i, 0, j, 0))
    out_t = jax.ShapeDtypeStruct((b, NA_DIM, t), BF16)
    ospec_t = pl.BlockSpec((1, NA_DIM, tm), lambda i, j: (i, 0, j))
    return pl.pallas_call(
        _na_prep_kernel,
        grid=(b, t // tm),
        in_specs=[pl.BlockSpec((1, tm, NA_DIM), lambda i, j: (i, j, 0)),
                  pl.BlockSpec((1, tm, NA_DIM), lambda i, j: (i, j, 1)),
                  pl.BlockSpec((1, tm, NA_DIM), lambda i, j: (i, j, 2)),
                  pl.BlockSpec((1, NA_DIM), lambda i, j: (0, 0)),
                  pl.BlockSpec((1, NA_DIM), lambda i, j: (0, 0)),
                  pl.BlockSpec((NA_DIM, NA_DIM), lambda i, j: (0, 0))],
        out_specs=[ospec, ospec, ospec_t],
        out_shape=[out, out, out_t],
        compiler_params=_params(2),
    )(p_hn, p_hn, p_hn, gq, gk, bd)


NA_QROWS = 4
NA_KROWS = NA_QROWS + NA_WIN_ROWS


def _rpb_expand_kernel(r_ref, oh_ref, m_ref, o_ref):
    o_ref[...] = (_dot_hi(r_ref[...], oh_ref[...]) + m_ref[...]) * LOG2E


def _na_bias_table(rpb):
    depth, heads, n_dr, n_dc = rpb.shape
    kc = jnp.arange(GRID_W)[:, None]
    qc = jnp.arange(GRID_W)[None, :]
    dc = jnp.clip(kc - qc + NA_WIN_COLS - 1, 0, n_dc - 1)
    cstart = jnp.clip(qc - NA_WIN_COLS // 2, 0, GRID_W - NA_WIN_COLS)
    valid = ((kc >= cstart) & (kc < cstart + NA_WIN_COLS)).reshape(1, GRID_W * GRID_W)
    n_pad = 32
    onehot = ((dc.reshape(1, -1) == jnp.arange(n_pad)[:, None]) & valid).astype(F32)
    mask = jnp.where(valid, 0.0, NEG_INF).astype(F32)
    rows = depth * heads * n_dr
    rpb2 = jnp.pad(rpb.reshape(rows, n_dc).astype(F32), ((0, 0), (0, n_pad - n_dc)))
    t1 = pl.pallas_call(
        _rpb_expand_kernel,
        out_shape=jax.ShapeDtypeStruct((rows, GRID_W * GRID_W), F32),
        compiler_params=pltpu.CompilerParams(vmem_limit_bytes=VMEM_LIMIT_BYTES),
    )(rpb2, onehot, mask).reshape(depth, heads, n_dr, GRID_W, GRID_W)
    neg = jnp.full((depth, heads, GRID_W, GRID_W), NEG_INF, F32)
    layouts = [lambda j: (j, 0), lambda j: (NA_QROWS + j, j), lambda j: (2 * NA_QROWS + j, NA_QROWS)]
    tabs = []
    for lay in layouts:
        per_key_row = []
        for i in range(NA_KROWS):
            per_q = []
            for j in range(NA_QROWS):
                r_rel, row0_rel = lay(j)
                inside = row0_rel <= i < row0_rel + NA_WIN_ROWS
                per_q.append(t1[:, :, i - r_rel + NA_WIN_ROWS - 1] if inside else neg)
            per_key_row.append(jnp.stack(per_q, axis=3))
        tabs.append(jnp.stack(per_key_row, axis=2))
    tab = jnp.stack(tabs, axis=2)
    return tab.reshape(depth, heads, 3, NA_KROWS * GRID_W, NA_QROWS * GRID_W)


ATTN_KEY_CHUNK = 256
LOG2E = math.log2(math.e)


def _attend_heads(heads, q_of, sets_of, s_scrs):
    row_max = [None] * heads
    outs = [None] * heads

    def score_phase(h):
        q = q_of(h)
        scr = s_scrs[h % 2]
        off, m = 0, None
        for n, k_fn, _, bias_fn in sets_of(h):
            s = _dot_nt(k_fn(), q)
            if bias_fn is not None:
                s = s + bias_fn()
            scr[off:off + n, :] = s
            mj = jnp.max(s, axis=0, keepdims=True)
            m = mj if m is None else jnp.maximum(m, mj)
            off += n
            yield
        row_max[h] = m

    def value_phase(h):
        scr = s_scrs[h % 2]
        m = row_max[h]
        base, den, acc = 0, None, None
        for n_set, _, vt_fn, _ in sets_of(h):
            for off, n in _key_chunks(n_set):
                p = jnp.exp2(scr[base + off:base + off + n, :] - m)
                dj = jnp.sum(p, axis=0, keepdims=True)
                oj = _dot(vt_fn(off, n), p.astype(BF16))
                den = dj if den is None else den + dj
                acc = oj if acc is None else acc + oj
                yield
            base += n_set
        outs[h] = acc / den

    for _ in score_phase(0):
        pass
    for h in range(heads):
        nxt = score_phase(h + 1) if h + 1 < heads else iter(())
        cur = value_phase(h)
        done_n = done_c = False
        while not (done_n and done_c):
            if not done_n:
                done_n = next(nxt, "end") == "end"
            if not done_c:
                done_c = next(cur, "end") == "end"
    return jnp.concatenate(outs, axis=0)


def _key_chunks(total):
    sizes = [ATTN_KEY_CHUNK] * (total // ATTN_KEY_CHUNK)
    if total % ATTN_KEY_CHUNK:
        sizes.append(total % ATTN_KEY_CHUNK)
    offs = [sum(sizes[:i]) for i in range(len(sizes))]
    return list(zip(offs, sizes))


def _na_kernel(q_ref, k_ref, vt_ref, kc_ref, vct_ref, b_ref, o_ref, s0_scr, s1_scr, *, rows):
    rb = pl.program_id(1)
    w0 = jnp.clip(NA_QROWS * rb - NA_WIN_ROWS // 2, 0, rows - NA_KROWS)
    start = pl.multiple_of(w0 * GRID_W, NA_QROWS * GRID_W)
    nk = NA_KROWS * GRID_W
    dh = NA_HEAD_DIM

    def sets_of(h):
        hs = slice(h * dh, (h + 1) * dh)
        window = (nk,
                  lambda: k_ref[0, h, pl.ds(start, nk), :],
                  lambda off, n: vt_ref[0, hs, pl.ds(pl.multiple_of(start + off, 128), n)],
                  lambda: b_ref[0, h, 0])
        context = (kc_ref.shape[2],
                   lambda: kc_ref[0, h],
                   lambda off, n: vct_ref[0, hs, off:off + n],
                   None)
        return [window, context]

    o_t = _attend_heads(NA_HEADS, lambda h: q_ref[0, h], sets_of, (s0_scr, s1_scr))
    o_ref[0] = o_t.T


def _na_latent(q, k, vt, kc, vct, bias_tab, layer):
    b, heads, s, dh = q.shape
    ctx = kc.shape[2]
    rows = s // GRID_W
    assert rows % NA_QROWS == 0 and rows >= NA_KROWS
    nrb = rows // NA_QROWS
    tq = NA_QROWS * GRID_W

    def bias_index(i, rb):
        return (layer, 0, jnp.where(rb == 0, 0, jnp.where(rb == nrb - 1, 2, 1)), 0, 0)

    full4 = lambda i, rb: (i, 0, 0, 0)
    full3 = lambda i, rb: (i, 0, 0)
    return pl.pallas_call(
        functools.partial(_na_kernel, rows=rows),
        grid=(b, nrb),
        in_specs=[pl.BlockSpec((1, heads, tq, dh), lambda i, rb: (i, 0, rb, 0)),
                  pl.BlockSpec((1, heads, s, dh), full4),
                  pl.BlockSpec((1, heads * dh, s), full3),
                  pl.BlockSpec((1, heads, ctx, dh), full4),
                  pl.BlockSpec((1, heads * dh, ctx), full3),
                  pl.BlockSpec((1, heads, 1, NA_KROWS * GRID_W, tq), bias_index)],
        out_specs=pl.BlockSpec((1, tq, heads * dh), lambda i, rb: (i, rb, 0)),
        out_shape=jax.ShapeDtypeStruct((b, s, heads * dh), F32),
        scratch_shapes=[pltpu.VMEM((NA_KROWS * GRID_W + ctx, tq), F32)] * 2,
        compiler_params=_params(2),
    )(q, k, vt, kc, vct, bias_tab)


def _attn_kernel(*refs, heads, dv, two_sets):
    if two_sets:
        q_ref, k1_ref, vt1_ref, k2_ref, vt2_ref, o_ref, s0_scr, s1_scr = refs
        key_sets = ((k1_ref, vt1_ref), (k2_ref, vt2_ref))
    else:
        q_ref, k1_ref, vt1_ref, o_ref, s0_scr, s1_scr = refs
        key_sets = ((k1_ref, vt1_ref),)

    def sets_of(h):
        hs = slice(h * dv, (h + 1) * dv)
        return [(k_ref.shape[2],
                 lambda k_ref=k_ref: k_ref[0, h],
                 lambda off, n, vt_ref=vt_ref: vt_ref[0, hs, off:off + n],
                 None) for k_ref, vt_ref in key_sets]

    o_t = _attend_heads(heads, lambda h: q_ref[0, h], sets_of, (s0_scr, s1_scr))
    o_ref[0] = o_t.T


def _attention(q, k1, vt1, k2=None, vt2=None, tq=256):
    b, heads, t, dq = q.shape
    dv = vt1.shape[1] // heads
    tq = _tile(t, tq)
    two_sets = k2 is not None
    n_keys = k1.shape[2] + (k2.shape[2] if two_sets else 0)
    full4 = lambda i, j: (i, 0, 0, 0)
    full3 = lambda i, j: (i, 0, 0)
    in_specs = [pl.BlockSpec((1, heads, tq, dq), lambda i, j: (i, 0, j, 0)),
                pl.BlockSpec((1,) + k1.shape[1:], full4),
                pl.BlockSpec((1,) + vt1.shape[1:], full3)]
    args = [q, k1, vt1]
    if two_sets:
        in_specs += [pl.BlockSpec((1,) + k2.shape[1:], full4),
                     pl.BlockSpec((1,) + vt2.shape[1:], full3)]
        args += [k2, vt2]
    return pl.pallas_call(
        functools.partial(_attn_kernel, heads=heads, dv=dv, two_sets=two_sets),
        grid=(b, t // tq),
        in_specs=in_specs,
        out_specs=pl.BlockSpec((1, tq, heads * dv), lambda i, j: (i, j, 0)),
        out_shape=jax.ShapeDtypeStruct((b, t, heads * dv), F32),
        scratch_shapes=[pltpu.VMEM((n_keys, tq), F32)] * 2,
        compiler_params=_params(2),
    )(*args)


def _mla_prep_kernel(*refs, rope):
    if rope:
        (p_ref, gqa_ref, gkva_ref, wq_ref, wkn_ref, wv_ref, g_ref, wqs_ref, cos_ref, sin_ref,
         qo_ref, ko_ref, vto_ref) = refs
    else:
        p_ref, gqa_ref, gkva_ref, wq_ref, wkn_ref, wv_ref, g_ref, qo_ref, ko_ref, vto_ref = refs
    p = p_ref[0]
    a, b_ = MLA_Q_RANK, MLA_Q_RANK + MLA_KV_RANK
    cq, ckv = p[:, :a], p[:, a:b_]
    kr, krs = p[:, b_:b_ + MLA_SLOT], p[:, b_ + MLA_SLOT:b_ + 2 * MLA_SLOT]

    def rms(x, g):
        return x * lax.rsqrt(jnp.mean(x * x, axis=-1, keepdims=True) + EPS) * g

    cqn = rms(cq, gqa_ref[...]).astype(BF16)
    ckvn = rms(ckv, gkva_ref[...]).astype(BF16)
    qa = _dot(cqn, wq_ref[...])
    kn = _dot(ckvn, wkn_ref[...])
    g = g_ref[...]
    aq, ak = g[0:1], g[2:3]
    if rope:
        qs = _dot(cqn, wqs_ref[...])
        cos_t, sin_t = cos_ref[...], sin_ref[...]
        aq, bq = aq * cos_t, g[1:2] * sin_t
        ak, k_rot = ak * cos_t, krs * (g[3:4] * sin_t)

    def inv_rms(x):
        return lax.rsqrt(jnp.sum(x * x, axis=-1, keepdims=True) * (1.0 / MLA_QK) + EPS)

    for h in range(MLA_HEADS):
        sl = slice(h * MLA_SLOT, (h + 1) * MLA_SLOT)
        xq = qa[:, sl]
        yq = xq * aq
        if rope:
            yq = yq + qs[:, sl] * bq
        qo_ref[0, h] = (yq * (inv_rms(xq) * (MLA_QK ** -0.5 * LOG2E))).astype(BF16)
        xk = kn[:, sl] + kr
        yk = xk * ak
        if rope:
            yk = yk + k_rot
        ko_ref[0, h] = (yk * inv_rms(xk)).astype(BF16)
    vto_ref[0] = _dot_nt(wv_ref[...], ckvn).astype(BF16)


_ROPE_SWAP = tuple(list(range(8, 16)) + list(range(0, 8)) + list(range(24, 32)) + list(range(16, 24)))


def _slot(nope, rope_part):
    lead = (nope if nope is not None else rope_part).shape[:-1]
    dt = (nope if nope is not None else rope_part).dtype
    z = lambda n: jnp.zeros(lead + (n,), dt)
    return jnp.concatenate([nope if nope is not None else z(MLA_NOPE),
                            rope_part if rope_part is not None else z(MLA_ROPE),
                            z(MLA_SLOT - MLA_QK)], axis=-1)


def _rope_tables(s):
    t = jnp.arange(s)
    pos = jnp.stack([t // GRID_W, t % GRID_W], axis=-1).astype(F32)
    inv = ROPE_BASE ** (-jnp.arange(ROPE_FREQS, dtype=F32) / ROPE_FREQS)
    ang = pos[:, :, None] * inv
    cos, sin = jnp.cos(ang), jnp.sin(ang)
    cos_t = jnp.concatenate([cos[:, 0], cos[:, 0], cos[:, 1], cos[:, 1]], axis=-1)
    sin_t = jnp.concatenate([-sin[:, 0], sin[:, 0], -sin[:, 1], sin[:, 1]], axis=-1)
    return _slot(jnp.ones((s, MLA_NOPE), F32), cos_t), _slot(None, sin_t)


def _mla_weights(w_q_up, w_kv_up, g_q, g_k):
    swap = jnp.array(_ROPE_SWAP)
    wq = w_q_up.reshape(MLA_Q_RANK, MLA_HEADS, MLA_QK)
    wkv = w_kv_up.reshape(MLA_KV_RANK, MLA_HEADS, MLA_NOPE + MLA_V)
    flat = lambda w: w.reshape(w.shape[0], -1).astype(BF16)
    wq_slot = flat(_slot(wq[..., :MLA_NOPE], wq[..., MLA_NOPE:]))
    wqs_slot = flat(_slot(None, wq[..., MLA_NOPE:][..., swap]))
    wkn_slot = flat(_slot(wkv[..., :MLA_NOPE], None))
    wv = flat(wkv[..., MLA_NOPE:]).T
    gains = jnp.stack([_slot(g_q[:MLA_NOPE], g_q[MLA_NOPE:]), _slot(None, g_q[MLA_NOPE:][swap]),
                       _slot(g_k[:MLA_NOPE], g_k[MLA_NOPE:]), _slot(None, g_k[MLA_NOPE:][swap])])
    return wq_slot, wkn_slot, wv, gains, wqs_slot


def _mla_prep(p_mla, g_qa, g_kva, weights, rope_tabs, tm):
    b, t, n = p_mla.shape
    tm = _tile(t, tm)
    wq_slot, wkn_slot, wv, gains, wqs_slot = weights
    rope = rope_tabs is not None
    const = lambda i, j: (0, 0)
    in_specs = [pl.BlockSpec((1, tm, n), lambda i, j: (i, j, 0)),
                pl.BlockSpec((1, MLA_Q_RANK), const),
                pl.BlockSpec((1, MLA_KV_RANK), const),
                pl.BlockSpec(wq_slot.shape, const), pl.BlockSpec(wkn_slot.shape, const),
                pl.BlockSpec(wv.shape, const), pl.BlockSpec(gains.shape, const)]
    args = [p_mla, g_qa[None, :], g_kva[None, :], wq_slot, wkn_slot, wv, gains]
    if rope:
        in_specs += [pl.BlockSpec(wqs_slot.shape, const),
                     pl.BlockSpec((tm, MLA_SLOT), lambda i, j: (j, 0)),
                     pl.BlockSpec((tm, MLA_SLOT), lambda i, j: (j, 0))]
        args += [wqs_slot] + list(rope_tabs)
    qk = jax.ShapeDtypeStruct((b, MLA_HEADS, t, MLA_SLOT), BF16)
    vt = jax.ShapeDtypeStruct((b, MLA_HEADS * MLA_V, t), BF16)
    qk_spec = pl.BlockSpec((1, MLA_HEADS, tm, MLA_SLOT), lambda i, j: (i, 0, j, 0))
    vt_spec = pl.BlockSpec((1, MLA_HEADS * MLA_V, tm), lambda i, j: (i, 0, j))
    return pl.pallas_call(
        functools.partial(_mla_prep_kernel, rope=rope),
        grid=(b, t // tm),
        in_specs=in_specs,
        out_specs=[qk_spec, qk_spec, vt_spec],
        out_shape=[qk, qk, vt],
        compiler_params=_params(2),
    )(*args)


def _outproj_kernel(hy_ref, na_ref, mla_ref, x_ref, gate_ref, w_hy_ref, w_na_ref, w_mla_ref, o_ref):
    mix = (_dot(hy_ref[0].astype(BF16), w_hy_ref[...])
           + _dot(na_ref[0].astype(BF16), w_na_ref[...])
           + _dot(mla_ref[0].astype(BF16), w_mla_ref[...]))
    o_ref[0] = x_ref[0] + gate_ref[0, 0] * mix


def _outproj(o_hy, o_na, o_mla, x, mod, w_hy, w_na, w_mla, tm):
    b, t, d = x.shape
    tm = _tile(t, tm)
    tok = lambda n: pl.BlockSpec((1, tm, n), lambda i, j: (i, j, 0))
    return pl.pallas_call(
        _outproj_kernel,
        grid=(b, t // tm),
        in_specs=[tok(o_hy.shape[-1]), tok(o_na.shape[-1]), tok(o_mla.shape[-1]), tok(d),
                  pl.BlockSpec((1, 1, 1, d), lambda i, j: (i, 2, 0, 0)),
                  _resident(w_hy.shape, lambda i, j: (0, 0)),
                  _resident(w_na.shape, lambda i, j: (0, 0)),
                  _resident(w_mla.shape, lambda i, j: (0, 0))],
        out_specs=tok(d),
        out_shape=jax.ShapeDtypeStruct((b, t, d), F32),
        compiler_params=_params(2),
    )(o_hy, o_na, o_mla, x, mod, w_hy, w_na, w_mla)


def _mlp_kernel(x_ref, sh_ref, sc_ref, gate_ref, g_ref, w1_ref, b1_ref, w2_ref, b2_ref, o_ref, *, chunk):
    x = x_ref[0]
    h = _adaln(x, g_ref[...], sh_ref[0, 0], sc_ref[0, 0]).astype(BF16)
    d_ff = w1_ref.shape[1]
    acc = jnp.zeros(x.shape, F32)
    for c0 in range(0, d_ff, chunk):
        a = jnp.maximum(_dot(h, w1_ref[:, c0:c0 + chunk]) + b1_ref[:, c0:c0 + chunk], 0.0)
        acc = acc + _dot((a * a).astype(BF16), w2_ref[c0:c0 + chunk, :])
    o_ref[0] = x + gate_ref[0, 0] * (acc + b2_ref[...])


def _mlp(x, mod, g, w1, b1, w2, b2, tm):
    b, t, d = x.shape
    d_ff = w1.shape[1]
    tm = _tile(t, tm)
    tok = pl.BlockSpec((1, tm, d), lambda i, j: (i, j, 0))
    modspec = lambda k: pl.BlockSpec((1, 1, 1, d), lambda i, j: (i, k, 0, 0))
    const = lambda i, j: (0, 0)
    return pl.pallas_call(
        functools.partial(_mlp_kernel, chunk=_tile(d_ff, 1024)),
        grid=(b, t // tm),
        in_specs=[tok, modspec(3), modspec(4), modspec(5),
                  pl.BlockSpec((1, d), const),
                  _resident((d, d_ff), const), pl.BlockSpec((1, d_ff), const),
                  _resident((d_ff, d), const), pl.BlockSpec((1, d), const)],
        out_specs=tok,
        out_shape=jax.ShapeDtypeStruct((b, t, d), F32),
        compiler_params=_params(2),
    )(x, mod, mod, mod, g, w1, b1, w2, b2)


def kernel(x, c, ctx, c_ctx, w_mod, b_mod, g_norm1, w_in, hy_conv_w, hy_conv_b, hy_f_w1, hy_f_b1, hy_f_w2, hy_f_b2, hy_f_w3, hy_f_b3, hy_freq, hy_bias, na_g_q, na_g_k, na_rpb, mla_g_qa, mla_g_kva, mla_w_q_up, mla_w_kv_up, mla_g_q, mla_g_k, w_out, g_norm2, w_ff1, b_ff1, w_ff2, b_ff2):
    b, s, d = x.shape
    lc = ctx.shape[1]
    depth = w_mod.shape[0]
    in_hn = 3 * HY_DIM + 3 * NA_DIM
    swap = jnp.array(_ROPE_SWAP)

    n_cond = -(-(b + 1) // 8) * 8
    cond = jnp.zeros((n_cond, d), F32).at[:b].set(c).at[b].set(c_ctx)
    mods = _modulation(cond, w_mod, b_mod).reshape(depth, n_cond, 6, 1, d)

    rope_tabs = _rope_tables(s)
    cs_x = _dft_matrix(s)
    cs_c = _dft_matrix(lc)
    filt = (hy_f_w1, hy_f_b1, hy_f_w2, hy_f_b2, hy_f_w3, hy_f_b3, hy_freq)
    spec_x = _hyena_spectra(s, cs_x, *filt)
    spec_c = _hyena_spectra(lc, cs_c, *filt)
    bias_tab = _na_bias_table(na_rpb)
    hy_bias = hy_bias.reshape(2 * depth, 1, HY_DIM)

    cx = ctx
    for i in range(depth):
        last = i == depth - 1
        mod_x = mods[i, :b]
        mod_c = jnp.broadcast_to(mods[i, b], (b, 6, 1, d))
        w_hy = w_in[i, :, :3 * HY_DIM].astype(BF16)
        w_na = w_in[i, :, 3 * HY_DIM:in_hn].astype(BF16)
        w_m = w_in[i, :, in_hn:]
        w_kr = w_m[:, MLA_Q_RANK + MLA_KV_RANK:]
        w_m = jnp.concatenate([w_m[:, :MLA_Q_RANK + MLA_KV_RANK], _slot(None, w_kr),
                               _slot(None, w_kr[:, swap])], axis=-1).astype(BF16)
        g1 = g_norm1[i][None, :]
        px_hy, px_na, px_mla = _inproj(x, mod_x, g1, w_hy, w_na, w_m, 512)
        pc_hy, pc_na, pc_mla = _inproj(cx, mod_c, g1, w_hy, w_na, w_m, 256)

        qx, kx, vtx = _na_prep(px_na, na_g_q[i], na_g_k[i], 512)
        qc, kc, vtc = _na_prep(pc_na, na_g_q[i], na_g_k[i], 256)
        o_na = _na_latent(qx, kx, vtx, kc, vtc, bias_tab, i)

        mla_w = _mla_weights(mla_w_q_up[i], mla_w_kv_up[i], mla_g_q[i], mla_g_k[i])
        mqx, mkx, mvx = _mla_prep(px_mla, mla_g_qa[i], mla_g_kva[i], mla_w, rope_tabs, 512)
        mqc, mkc, mvc = _mla_prep(pc_mla, mla_g_qa[i], mla_g_kva[i], mla_w, None, 256)
        o_mla = _attention(mqx, mkx, mvx, mkc, mvc)

        o_hy = _hyena(px_hy, hy_conv_w[i], hy_conv_b[i][None, :], cs_x, *spec_x, hy_bias, i)

        wo = w_out[i].astype(BF16)
        wo_hy, wo_na, wo_mla = wo[:HY_DIM], wo[HY_DIM:HY_DIM + NA_DIM], wo[HY_DIM + NA_DIM:]
        g2 = g_norm2[i][None, :]
        w1, w2 = w_ff1[i].astype(BF16), w_ff2[i].astype(BF16)
        b1, b2 = b_ff1[i][None, :], b_ff2[i][None, :]
        x = _outproj(o_hy, o_na, o_mla, x, mod_x, wo_hy, wo_na, wo_mla, 512)
        x = _mlp(x, mod_x, g2, w1, b1, w2, b2, 512)

        if not last:
            oc_hy = _hyena(pc_hy, hy_conv_w[i], hy_conv_b[i][None, :], cs_c, *spec_c, hy_bias, i)
            oc_na = _attention(qc, kc, vtc)
            oc_mla = _attention(mqc, mkc, mvc)
            cx = _outproj(oc_hy, oc_na, oc_mla, cx, mod_c, wo_hy, wo_na, wo_mla, 256)
            cx = _mlp(cx, mod_c, g2, w1, b1, w2, b2, 256)
    return x
```

```python
import functools
import math

import jax
import jax.numpy as jnp
from jax import lax
from jax.experimental import pallas as pl
from jax.experimental.pallas import tpu as pltpu

F32 = jnp.float32
BF16 = jnp.bfloat16
HIGHEST = lax.Precision.HIGHEST

EPS = 1e-6
NEG_INF = -1e9
GRID_W = 64

HY_DIM = 256
HY_BANDS = 8
HY_DECAY_TARGET = 1e-2
HY_FAST = 0.3
HY_SLOW = 1.5

NA_HEADS = 4
NA_HEAD_DIM = 64
NA_DIM = NA_HEADS * NA_HEAD_DIM
NA_WIN_ROWS = 8
NA_WIN_COLS = 16

MLA_HEADS = 8
MLA_Q_RANK = 256
MLA_KV_RANK = 128
MLA_NOPE = 64
MLA_ROPE = 32
MLA_V = 64
MLA_QK = MLA_NOPE + MLA_ROPE
MLA_SLOT = 128
ROPE_BASE = 10000.0
ROPE_FREQS = MLA_ROPE // 4

VMEM_LIMIT_BYTES = 56 * 1024 * 1024
HYENA_VMEM_LIMIT_BYTES = 60 * 1024 * 1024

def _params(n_grid_dims):
    return pltpu.CompilerParams(dimension_semantics=("arbitrary",) * n_grid_dims,
                                vmem_limit_bytes=VMEM_LIMIT_BYTES)


def _tile(total, preferred):
    t = min(total, preferred)
    while total % t:
        t //= 2
    return t


def _resident(shape, index_map):
    return pl.BlockSpec(shape, index_map, pipeline_mode=pl.Buffered(1))


def _dot(a, b):
    return jnp.dot(a, b, preferred_element_type=F32)


def _dot_hi(a, b):
    return jnp.dot(a, b, preferred_element_type=F32, precision=HIGHEST)


def _dot_nt(a, b):
    return lax.dot_general(a, b, (((1,), (1,)), ((), ())), preferred_element_type=F32)


def _mod_kernel(cond_ref, w_ref, b_ref, o_ref):
    a = cond_ref[...]
    a = a / (1.0 + jnp.exp(-a))
    o_ref[0] = _dot_hi(a, w_ref[0]) + b_ref[0]


def _modulation(cond, w_mod, b_mod):
    depth, d, d6 = w_mod.shape
    n = cond.shape[0]
    tn = _tile(d6, 1536)
    return pl.pallas_call(
        _mod_kernel,
        grid=(depth, d6 // tn),
        in_specs=[pl.BlockSpec((n, d), lambda l, j: (0, 0)),
                  pl.BlockSpec((1, d, tn), lambda l, j: (l, 0, j)),
                  pl.BlockSpec((1, 1, tn), lambda l, j: (l, 0, j))],
        out_specs=pl.BlockSpec((1, n, tn), lambda l, j: (l, 0, j)),
        out_shape=jax.ShapeDtypeStruct((depth, n, d6), F32),
        compiler_params=_params(2),
    )(cond, w_mod, b_mod.reshape(depth, 1, d6))


def _adaln(x, g, shift, scale):
    y = x * lax.rsqrt(jnp.mean(x * x, axis=-1, keepdims=True) + EPS) * g
    return y * (1.0 + scale) + shift


def _inproj_kernel(x_ref, sh_ref, sc_ref, g_ref, w_hy_ref, w_na_ref, w_m_ref, o_hy_ref, o_na_ref, o_m_ref):
    h = _adaln(x_ref[0], g_ref[...], sh_ref[0, 0], sc_ref[0, 0]).astype(BF16)
    o_hy_ref[0] = _dot(h, w_hy_ref[...]).astype(BF16)
    o_na_ref[0] = _dot(h, w_na_ref[...])
    o_m_ref[0] = _dot(h, w_m_ref[...])


def _inproj(x, mod, g, w_hy, w_na, w_m, tm):
    b, t, d = x.shape
    tm = _tile(t, tm)
    widths = (w_hy.shape[1], w_na.shape[1], w_m.shape[1])
    return pl.pallas_call(
        _inproj_kernel,
        grid=(b, t // tm),
        in_specs=[pl.BlockSpec((1, tm, d), lambda i, j: (i, j, 0)),
                  pl.BlockSpec((1, 1, 1, d), lambda i, j: (i, 0, 0, 0)),
                  pl.BlockSpec((1, 1, 1, d), lambda i, j: (i, 1, 0, 0)),
                  pl.BlockSpec((1, d), lambda i, j: (0, 0))]
                 + [_resident((d, n), lambda i, j: (0, 0)) for n in widths],
        out_specs=[pl.BlockSpec((1, tm, n), lambda i, j: (i, j, 0)) for n in widths],
        out_shape=[jax.ShapeDtypeStruct((b, t, n), dt) for n, dt in zip(widths, (BF16, F32, F32))],
        compiler_params=_params(2),
    )(x, mod, mod, g, w_hy, w_na, w_m)


def _dft_matrix(length):
    n = 2 * length
    f = jnp.arange(length, dtype=jnp.int32)[:, None]
    t = jnp.arange(length, dtype=jnp.int32)[None, :]
    ang = ((f * t) % n).astype(F32) * (2.0 * math.pi / n)
    return jnp.concatenate([jnp.cos(ang), jnp.sin(ang)], axis=1).astype(BF16)


def _filt_kernel(z_ref, dec_ref, w1_ref, b1_ref, w2_ref, b2_ref, w3_ref, b3_ref, fr_ref, hsum_ref, hdiff_ref):
    z = z_ref[...]
    fr = fr_ref[0]
    a = jnp.sin(fr[0:1] * (_dot_hi(z, w1_ref[0]) + b1_ref[0]))
    a = jnp.sin(fr[1:2] * (_dot_hi(a, w2_ref[0]) + b2_ref[0]))
    h = _dot_hi(a, w3_ref[0]) + b3_ref[0]
    dec = dec_ref[...]
    c = HY_DIM
    row0 = lax.broadcasted_iota(jnp.int32, dec.shape, 0) == 0
    for o in range(2):
        hf = h[:, (2 * o) * c:(2 * o + 1) * c] * dec
        hb = h[:, (2 * o + 1) * c:(2 * o + 2) * c] * dec
        nrm = (jnp.sum(jnp.abs(hf), axis=0, keepdims=True)
               + jnp.sum(jnp.abs(hb), axis=0, keepdims=True) + EPS)
        hf = hf / nrm
        hb = jnp.where(row0, 0.0, hb / nrm)
        hsum_ref[0, :, o * c:(o + 1) * c] = hf + hb
        hdiff_ref[0, :, o * c:(o + 1) * c] = hf - hb


def _alternating_sign(shape):
    row = lax.broadcasted_iota(jnp.int32, shape, 0)
    return (1 - 2 * (row % 2)).astype(F32)


def _spec_kernel(hsum_ref, hdiff_ref, cs_ref, hc_ref, hs_ref, hn_ref):
    length = hsum_ref.shape[1]
    hsum = hsum_ref[0]
    hdiff = hdiff_ref[0]

    def split_dot(m, v):
        hi = v.astype(BF16)
        lo = (v - hi.astype(F32)).astype(BF16)
        return _dot(m, hi) + _dot(m, lo)

    row = lax.broadcasted_iota(jnp.int32, hsum.shape, 0)
    wf = jnp.where(row == 0, 0.5 / length, 1.0 / length)
    hc_ref[0] = split_dot(cs_ref[:, :length], hsum) * wf
    hs_ref[0] = split_dot(cs_ref[:, length:], hdiff) * wf
    hn_ref[0] = jnp.sum(_alternating_sign(hsum.shape) * hsum, axis=0, keepdims=True) * (0.5 / length)


def _hyena_spectra(length, cs, f_w1, f_b1, f_w2, f_b2, f_w3, f_b3, freq):
    depth = f_w1.shape[0]
    pos_dim = f_w1.shape[1]
    hid = f_w1.shape[2]
    kpad = 32
    t = jnp.linspace(0.0, 1.0, length, dtype=F32)[:, None]
    w = (2.0 * math.pi / length) * jnp.arange(length, dtype=F32)[:, None]
    bands = jnp.linspace(1e-4, HY_BANDS - 1, HY_BANDS, dtype=F32)
    z = jnp.concatenate([t, jnp.cos(bands * w), -jnp.sin(bands * w)], axis=-1)
    z = jnp.pad(z, ((0, 0), (0, kpad - pos_dim)))
    w1 = jnp.pad(f_w1, ((0, 0), (0, kpad - pos_dim), (0, 0)))
    deltas = jnp.linspace(math.log(HY_DECAY_TARGET) / HY_SLOW, math.log(HY_DECAY_TARGET) / HY_FAST,
                          HY_DIM, dtype=F32)
    decay = jnp.exp(-t * jnp.abs(deltas))
    c2 = 2 * HY_DIM
    c4 = 4 * HY_DIM
    lay3 = lambda l: (l, 0, 0)
    hsum, hdiff = pl.pallas_call(
        _filt_kernel,
        grid=(depth,),
        in_specs=[pl.BlockSpec((length, kpad), lambda l: (0, 0)),
                  pl.BlockSpec((length, HY_DIM), lambda l: (0, 0)),
                  pl.BlockSpec((1, kpad, hid), lay3),
                  pl.BlockSpec((1, 1, hid), lay3),
                  pl.BlockSpec((1, hid, hid), lay3),
                  pl.BlockSpec((1, 1, hid), lay3),
                  pl.BlockSpec((1, hid, c4), lay3),
                  pl.BlockSpec((1, 1, c4), lay3),
                  pl.BlockSpec((1, 2, hid), lay3)],
        out_specs=[pl.BlockSpec((1, length, c2), lay3),
                   pl.BlockSpec((1, length, c2), lay3)],
        out_shape=[jax.ShapeDtypeStruct((depth, length, c2), F32),
                   jax.ShapeDtypeStruct((depth, length, c2), F32)],
        compiler_params=_params(1),
    )(z, decay, w1, f_b1.reshape(depth, 1, hid), f_w2, f_b2.reshape(depth, 1, hid),
      f_w3, f_b3.reshape(depth, 1, c4), freq)
    c = HY_DIM
    blk = pl.BlockSpec((1, length, c), lambda l, o: (l, 0, o))
    return pl.pallas_call(
        _spec_kernel,
        grid=(depth, 2),
        in_specs=[blk, blk, _resident((length, 2 * length), lambda l, o: (0, 0))],
        out_specs=[blk, blk, pl.BlockSpec((1, 1, c), lambda l, o: (l, 0, o))],
        out_shape=[jax.ShapeDtypeStruct((depth, length, c2), F32),
                   jax.ShapeDtypeStruct((depth, length, c2), F32),
                   jax.ShapeDtypeStruct((depth, 1, c2), F32)],
        compiler_params=_params(2),
    )(hsum, hdiff, cs)


HY_ROWS = 256
HY_HALO = 8


HY_PAIR = 2


def _hyena_kernel(pv_ref, pg_ref, wv_ref, bv_ref, wg_ref, bg_ref, cs_ref, hc_ref, hs_ref, hn_ref, bias_ref,
                  o_ref, z_scr, xc_scr, xs_scr, yc_scr, ys_scr):
    length = pv_ref.shape[1]
    c = HY_DIM
    rows = min(HY_ROWS, length)
    chunks = [(r0, rows) for r0 in range(0, length, rows)]
    halves = [(i, slice(i * c, (i + 1) * c)) for i in range(HY_PAIR)]

    def short_conv_rows(p_ref, w_ref, b_ref, i, r0):
        lo = max(r0 - HY_HALO, 0)
        hi = min(r0 + rows + HY_HALO, length)
        u = p_ref[i, lo:hi, :].astype(F32)
        row = lax.broadcasted_iota(jnp.int32, u.shape, 0) + lo
        prev = jnp.where(row == 0, 0.0, pltpu.roll(u, 1, 0))
        nxt = jnp.where(row == length - 1, 0.0, pltpu.roll(u, hi - lo - 1, 0))
        y = prev * w_ref[0:1, :] + u * w_ref[1:2, :] + nxt * w_ref[2:3, :] + b_ref[...]
        return y[r0 - lo:r0 - lo + rows]

    @pl.when(pl.program_id(1) == 0)
    def _():
        for r0, n in chunks:
            for i, lanes in halves:
                z_scr[r0:r0 + n, lanes] = short_conv_rows(pv_ref, wv_ref, bv_ref, i, r0)

    sign = _alternating_sign((rows, c))
    zb = z_scr[...].astype(BF16)
    xc_scr[...] = _dot(cs_ref[:, :length], zb)
    xs_scr[...] = _dot(cs_ref[:, length:], zb)
    xn = [jnp.zeros((1, c), F32) for _ in halves]
    for r0, n in chunks:
        hc, hs = hc_ref[0, r0:r0 + n, :], hs_ref[0, r0:r0 + n, :]
        for i, lanes in halves:
            xc, xs = xc_scr[r0:r0 + n, lanes], xs_scr[r0:r0 + n, lanes]
            yc_scr[r0:r0 + n, lanes] = (xc * hc - xs * hs).astype(BF16)
            ys_scr[r0:r0 + n, lanes] = (xc * hs + xs * hc).astype(BF16)
            xn[i] = xn[i] + jnp.sum(sign * z_scr[r0:r0 + n, lanes], axis=0, keepdims=True)
    xc_scr[...] = _dot(cs_ref[:, :length], yc_scr[...])
    xs_scr[...] = _dot(cs_ref[:, length:], ys_scr[...])
    bias = bias_ref[0]
    for r0, n in chunks:
        for i, lanes in halves:
            z = z_scr[r0:r0 + n, lanes]
            conv = xc_scr[r0:r0 + n, lanes] + xs_scr[r0:r0 + n, lanes] + sign * (xn[i] * hn_ref[0])
            z = short_conv_rows(pg_ref, wg_ref, bg_ref, i, r0) * (conv + z * bias)
            z_scr[r0:r0 + n, lanes] = z
            o_ref[i, r0:r0 + n, :] = z.astype(o_ref.dtype)


def _hyena(p_hy, conv_w, conv_b, cs, hc, hs, hn, bias, layer):
    b, length, _ = p_hy.shape
    assert b % HY_PAIR == 0
    c = HY_DIM
    wide = HY_PAIR * c
    return pl.pallas_call(
        _hyena_kernel,
        grid=(b // HY_PAIR, 2),
        in_specs=[pl.BlockSpec((HY_PAIR, length, c), lambda i, o: (i, 0, 0), pipeline_mode=pl.Buffered(1)),
                  pl.BlockSpec((HY_PAIR, length, c), lambda i, o: (i, 0, o + 1)),
                  pl.BlockSpec((3, c), lambda i, o: (0, 0)),
                  pl.BlockSpec((1, c), lambda i, o: (0, 0)),
                  pl.BlockSpec((3, c), lambda i, o: (0, o + 1)),
                  pl.BlockSpec((1, c), lambda i, o: (0, o + 1)),
                  _resident((length, 2 * length), lambda i, o: (0, 0)),
                  pl.BlockSpec((1, length, c), lambda i, o: (layer, 0, o), pipeline_mode=pl.Buffered(1)),
                  pl.BlockSpec((1, length, c), lambda i, o: (layer, 0, o), pipeline_mode=pl.Buffered(1)),
                  pl.BlockSpec((1, 1, c), lambda i, o: (layer, 0, o)),
                  pl.BlockSpec((1, 1, c), lambda i, o: (2 * layer + o, 0, 0))],
        out_specs=pl.BlockSpec((HY_PAIR, length, c), lambda i, o: (i, 0, 0)),
        out_shape=jax.ShapeDtypeStruct((b, length, c), BF16),
        scratch_shapes=[pltpu.VMEM((length, wide), F32), pltpu.VMEM((length, wide), F32),
                        pltpu.VMEM((length, wide), F32), pltpu.VMEM((length, wide), BF16),
                        pltpu.VMEM((length, wide), BF16)],
        compiler_params=pltpu.CompilerParams(dimension_semantics=("arbitrary", "arbitrary"),
                                             vmem_limit_bytes=HYENA_VMEM_LIMIT_BYTES),
    )(p_hy, p_hy, conv_w, conv_b, conv_w, conv_b, cs, hc, hs, hn, bias)


def _na_prep_kernel(q_ref, k_ref, v_ref, gq_ref, gk_ref, bd_ref, qo_ref, ko_ref, vo_ref):
    bd = bd_ref[...]

    def head_rms(x, g):
        sq = x * x
        hi = sq.astype(BF16)
        lo = (sq - hi.astype(F32)).astype(BF16)
        ms = (_dot(hi, bd) + _dot(lo, bd)) * (1.0 / NA_HEAD_DIM)
        return x * lax.rsqrt(ms + EPS) * g

    q = (head_rms(q_ref[0], gq_ref[...]) * (NA_HEAD_DIM ** -0.5 * LOG2E)).astype(BF16)
    k = head_rms(k_ref[0], gk_ref[...]).astype(BF16)
    for h in range(NA_HEADS):
        sl = slice(h * NA_HEAD_DIM, (h + 1) * NA_HEAD_DIM)
        qo_ref[0, h] = q[:, sl]
        ko_ref[0, h] = k[:, sl]
    vo_ref[0] = v_ref[0].T.astype(BF16)


def _na_prep(p_hn, g_q, g_k, tm):
    b, t, _ = p_hn.shape
    tm = _tile(t, tm)
    gq = jnp.tile(g_q, NA_HEADS)[None, :]
    gk = jnp.tile(g_k, NA_HEADS)[None, :]
    head = jnp.arange(NA_DIM) // NA_HEAD_DIM
    bd = (head[:, None] == head[None, :]).astype(BF16)
    out = jax.ShapeDtypeStruct((b, NA_HEADS, t, NA_HEAD_DIM), BF16)
    ospec = pl.BlockSpec((1, NA_HEADS, tm, NA_HEAD_DIM), lambda i, j: (i, 0, j, 0))
    out_t = jax.ShapeDtypeStruct((b, NA_DIM, t), BF16)
    ospec_t = pl.BlockSpec((1, NA_DIM, tm), lambda i, j: (i, 0, j))
    return pl.pallas_call(
        _na_prep_kernel,
        grid=(b, t // tm),
        in_specs=[pl.BlockSpec((1, tm, NA_DIM), lambda i, j: (i, j, 0)),
                  pl.BlockSpec((1, tm, NA_DIM), lambda i, j: (i, j, 1)),
                  pl.BlockSpec((1, tm, NA_DIM), lambda i, j: (i, j, 2)),
                  pl.BlockSpec((1, NA_DIM), lambda i, j: (0, 0)),
                  pl.BlockSpec((1, NA_DIM), lambda i, j: (0, 0)),
                  pl.BlockSpec((NA_DIM, NA_DIM), lambda i, j: (0, 0))],
        out_specs=[ospec, ospec, ospec_t],
        out_shape=[out, out, out_t],
        compiler_params=_params(2),
    )(p_hn, p_hn, p_hn, gq, gk, bd)


NA_QROWS = 4
NA_KROWS = NA_QROWS + NA_WIN_ROWS


def _rpb_expand_kernel(r_ref, oh_ref, m_ref, o_ref):
    o_ref[...] = (_dot_hi(r_ref[...], oh_ref[...]) + m_ref[...]) * LOG2E


def _na_bias_table(rpb):
    depth, heads, n_dr, n_dc = rpb.shape
    kc = jnp.arange(GRID_W)[:, None]
    qc = jnp.arange(GRID_W)[None, :]
    dc = jnp.clip(kc - qc + NA_WIN_COLS - 1, 0, n_dc - 1)
    cstart = jnp.clip(qc - NA_WIN_COLS // 2, 0, GRID_W - NA_WIN_COLS)
    valid = ((kc >= cstart) & (kc < cstart + NA_WIN_COLS)).reshape(1, GRID_W * GRID_W)
    n_pad = 32
    onehot = ((dc.reshape(1, -1) == jnp.arange(n_pad)[:, None]) & valid).astype(F32)
    mask = jnp.where(valid, 0.0, NEG_INF).astype(F32)
    rows = depth * heads * n_dr
    rpb2 = jnp.pad(rpb.reshape(rows, n_dc).astype(F32), ((0, 0), (0, n_pad - n_dc)))
    t1 = pl.pallas_call(
        _rpb_expand_kernel,
        out_shape=jax.ShapeDtypeStruct((rows, GRID_W * GRID_W), F32),
        compiler_params=pltpu.CompilerParams(vmem_limit_bytes=VMEM_LIMIT_BYTES),
    )(rpb2, onehot, mask).reshape(depth, heads, n_dr, GRID_W, GRID_W)
    neg = jnp.full((depth, heads, GRID_W, GRID_W), NEG_INF, F32)
    layouts = [lambda j: (j, 0), lambda j: (NA_QROWS + j, j), lambda j: (2 * NA_QROWS + j, NA_QROWS)]
    tabs = []
    for lay in layouts:
        per_key_row = []
        for i in range(NA_KROWS):
            per_q = []
            for j in range(NA_QROWS):
                r_rel, row0_rel = lay(j)
                inside = row0_rel <= i < row0_rel + NA_WIN_ROWS
                per_q.append(t1[:, :, i - r_rel + NA_WIN_ROWS - 1] if inside else neg)
            per_key_row.append(jnp.stack(per_q, axis=3))
        tabs.append(jnp.stack(per_key_row, axis=2))
    tab = jnp.stack(tabs, axis=2)
    return tab.reshape(depth, heads, 3, NA_KROWS * GRID_W, NA_QROWS * GRID_W)


ATTN_CHUNK_ELEMS = 64 * 1024
LOG2E = math.log2(math.e)


def _attend_heads(heads, q_of, sets_of, s_scrs):
    row_max = [None] * heads
    outs = [None] * heads

    def score_phase(h):
        q = q_of(h)
        scr = s_scrs[h % 2]
        off, m = 0, None
        for n, k_fn, _, bias_fn in sets_of(h):
            s = _dot_nt(k_fn(), q)
            if bias_fn is not None:
                s = s + bias_fn()
            scr[off:off + n, :] = s
            mj = jnp.max(s, axis=0, keepdims=True)
            m = mj if m is None else jnp.maximum(m, mj)
            off += n
            yield
        row_max[h] = m

    def value_phase(h):
        scr = s_scrs[h % 2]
        m = row_max[h]
        base, den, acc = 0, None, None
        for n_set, _, vt_fn, _ in sets_of(h):
            for off, n in _key_chunks(n_set, ATTN_CHUNK_ELEMS // scr.shape[1]):
                p = jnp.exp2(scr[base + off:base + off + n, :] - m)
                dj = jnp.sum(p, axis=0, keepdims=True)
                oj = _dot(vt_fn(off, n), p.astype(BF16))
                den = dj if den is None else den + dj
                acc = oj if acc is None else acc + oj
                yield
            base += n_set
        outs[h] = acc / den

    for _ in score_phase(0):
        pass
    for h in range(heads):
        nxt = score_phase(h + 1) if h + 1 < heads else iter(())
        cur = value_phase(h)
        done_n = done_c = False
        while not (done_n and done_c):
            if not done_n:
                done_n = next(nxt, "end") == "end"
            if not done_c:
                done_c = next(cur, "end") == "end"
    return jnp.concatenate(outs, axis=0)


def _key_chunks(total, chunk):
    sizes = [chunk] * (total // chunk)
    if total % chunk:
        sizes.append(total % chunk)
    offs = [sum(sizes[:i]) for i in range(len(sizes))]
    return list(zip(offs, sizes))


def _na_kernel(q_ref, k_ref, vt_ref, kc_ref, vct_ref, b_ref, o_ref, s0_scr, s1_scr, *, rows):
    rb = pl.program_id(1)
    w0 = jnp.clip(NA_QROWS * rb - NA_WIN_ROWS // 2, 0, rows - NA_KROWS)
    start = pl.multiple_of(w0 * GRID_W, NA_QROWS * GRID_W)
    nk = NA_KROWS * GRID_W
    dh = NA_HEAD_DIM

    def sets_of(h):
        hs = slice(h * dh, (h + 1) * dh)
        window = (nk,
                  lambda: k_ref[0, h, pl.ds(start, nk), :],
                  lambda off, n: vt_ref[0, hs, pl.ds(pl.multiple_of(start + off, 128), n)],
                  lambda: b_ref[0, h, 0])
        context = (kc_ref.shape[2],
                   lambda: kc_ref[0, h],
                   lambda off, n: vct_ref[0, hs, off:off + n],
                   None)
        return [window, context]

    o_t = _attend_heads(NA_HEADS, lambda h: q_ref[0, h], sets_of, (s0_scr, s1_scr))
    o_ref[0] = o_t.T.astype(o_ref.dtype)


def _na_latent(q, k, vt, kc, vct, bias_tab, layer):
    b, heads, s, dh = q.shape
    ctx = kc.shape[2]
    rows = s // GRID_W
    assert rows % NA_QROWS == 0 and rows >= NA_KROWS
    nrb = rows // NA_QROWS
    tq = NA_QROWS * GRID_W

    def bias_index(i, rb):
        return (layer, 0, jnp.where(rb == 0, 0, jnp.where(rb == nrb - 1, 2, 1)), 0, 0)

    full4 = lambda i, rb: (i, 0, 0, 0)
    full3 = lambda i, rb: (i, 0, 0)
    return pl.pallas_call(
        functools.partial(_na_kernel, rows=rows),
        grid=(b, nrb),
        in_specs=[pl.BlockSpec((1, heads, tq, dh), lambda i, rb: (i, 0, rb, 0)),
                  pl.BlockSpec((1, heads, s, dh), full4),
                  pl.BlockSpec((1, heads * dh, s), full3),
                  pl.BlockSpec((1, heads, ctx, dh), full4),
                  pl.BlockSpec((1, heads * dh, ctx), full3),
                  pl.BlockSpec((1, heads, 1, NA_KROWS * GRID_W, tq), bias_index)],
        out_specs=pl.BlockSpec((1, tq, heads * dh), lambda i, rb: (i, rb, 0)),
        out_shape=jax.ShapeDtypeStruct((b, s, heads * dh), BF16),
        scratch_shapes=[pltpu.VMEM((NA_KROWS * GRID_W + ctx, tq), F32)] * 2,
        compiler_params=_params(2),
    )(q, k, vt, kc, vct, bias_tab)


def _attn_kernel(*refs, heads, dv, two_sets):
    if two_sets:
        q_ref, k1_ref, vt1_ref, k2_ref, vt2_ref, o_ref, s0_scr, s1_scr = refs
        key_sets = ((k1_ref, vt1_ref), (k2_ref, vt2_ref))
    else:
        q_ref, k1_ref, vt1_ref, o_ref, s0_scr, s1_scr = refs
        key_sets = ((k1_ref, vt1_ref),)

    def sets_of(h):
        hs = slice(h * dv, (h + 1) * dv)
        return [(k_ref.shape[2],
                 lambda k_ref=k_ref: k_ref[0, h],
                 lambda off, n, vt_ref=vt_ref: vt_ref[0, hs, off:off + n],
                 None) for k_ref, vt_ref in key_sets]

    o_t = _attend_heads(heads, lambda h: q_ref[0, h], sets_of, (s0_scr, s1_scr))
    o_ref[0] = o_t.T.astype(o_ref.dtype)


def _attention(q, k1, vt1, k2=None, vt2=None, tq=512):
    b, heads, t, dq = q.shape
    dv = vt1.shape[1] // heads
    tq = _tile(t, tq)
    two_sets = k2 is not None
    n_keys = k1.shape[2] + (k2.shape[2] if two_sets else 0)
    full4 = lambda i, j: (i, 0, 0, 0)
    full3 = lambda i, j: (i, 0, 0)
    in_specs = [pl.BlockSpec((1, heads, tq, dq), lambda i, j: (i, 0, j, 0)),
                pl.BlockSpec((1,) + k1.shape[1:], full4),
                pl.BlockSpec((1,) + vt1.shape[1:], full3)]
    args = [q, k1, vt1]
    if two_sets:
        in_specs += [pl.BlockSpec((1,) + k2.shape[1:], full4),
                     pl.BlockSpec((1,) + vt2.shape[1:], full3)]
        args += [k2, vt2]
    return pl.pallas_call(
        functools.partial(_attn_kernel, heads=heads, dv=dv, two_sets=two_sets),
        grid=(b, t // tq),
        in_specs=in_specs,
        out_specs=pl.BlockSpec((1, tq, heads * dv), lambda i, j: (i, j, 0)),
        out_shape=jax.ShapeDtypeStruct((b, t, heads * dv), BF16),
        scratch_shapes=[pltpu.VMEM((n_keys, tq), F32)] * 2,
        compiler_params=_params(2),
    )(*args)


def _mla_prep_kernel(*refs, rope):
    if rope:
        (p_ref, gqa_ref, gkva_ref, wq_ref, wkn_ref, wv_ref, g_ref, wqs_ref, cos_ref, sin_ref,
         qo_ref, ko_ref, vto_ref) = refs
    else:
        p_ref, gqa_ref, gkva_ref, wq_ref, wkn_ref, wv_ref, g_ref, qo_ref, ko_ref, vto_ref = refs
    p = p_ref[0]
    a, b_ = MLA_Q_RANK, MLA_Q_RANK + MLA_KV_RANK
    cq, ckv = p[:, :a], p[:, a:b_]
    kr, krs = p[:, b_:b_ + MLA_SLOT], p[:, b_ + MLA_SLOT:b_ + 2 * MLA_SLOT]

    def rms(x, g):
        return x * lax.rsqrt(jnp.mean(x * x, axis=-1, keepdims=True) + EPS) * g

    cqn = rms(cq, gqa_ref[...]).astype(BF16)
    ckvn = rms(ckv, gkva_ref[...]).astype(BF16)
    qa = _dot(cqn, wq_ref[...])
    kn = _dot(ckvn, wkn_ref[...])
    g = g_ref[...]
    aq, ak = g[0:1], g[2:3]
    if rope:
        qs = _dot(cqn, wqs_ref[...])
        cos_t, sin_t = cos_ref[...], sin_ref[...]
        aq, bq = aq * cos_t, g[1:2] * sin_t
        ak, k_rot = ak * cos_t, krs * (g[3:4] * sin_t)

    def inv_rms(x):
        return lax.rsqrt(jnp.sum(x * x, axis=-1, keepdims=True) * (1.0 / MLA_QK) + EPS)

    for h in range(MLA_HEADS):
        sl = slice(h * MLA_SLOT, (h + 1) * MLA_SLOT)
        xq = qa[:, sl]
        yq = xq * aq
        if rope:
            yq = yq + qs[:, sl] * bq
        qo_ref[0, h] = (yq * (inv_rms(xq) * (MLA_QK ** -0.5 * LOG2E))).astype(BF16)
        xk = kn[:, sl] + kr
        yk = xk * ak
        if rope:
            yk = yk + k_rot
        ko_ref[0, h] = (yk * inv_rms(xk)).astype(BF16)
    vto_ref[0] = _dot_nt(wv_ref[...], ckvn).astype(BF16)


_ROPE_SWAP = tuple(list(range(8, 16)) + list(range(0, 8)) + list(range(24, 32)) + list(range(16, 24)))


def _slot(nope, rope_part):
    lead = (nope if nope is not None else rope_part).shape[:-1]
    dt = (nope if nope is not None else rope_part).dtype
    z = lambda n: jnp.zeros(lead + (n,), dt)
    return jnp.concatenate([nope if nope is not None else z(MLA_NOPE),
                            rope_part if rope_part is not None else z(MLA_ROPE),
                            z(MLA_SLOT - MLA_QK)], axis=-1)


def _rope_tables(s):
    t = jnp.arange(s)
    pos = jnp.stack([t // GRID_W, t % GRID_W], axis=-1).astype(F32)
    inv = ROPE_BASE ** (-jnp.arange(ROPE_FREQS, dtype=F32) / ROPE_FREQS)
    ang = pos[:, :, None] * inv
    cos, sin = jnp.cos(ang), jnp.sin(ang)
    cos_t = jnp.concatenate([cos[:, 0], cos[:, 0], cos[:, 1], cos[:, 1]], axis=-1)
    sin_t = jnp.concatenate([-sin[:, 0], sin[:, 0], -sin[:, 1], sin[:, 1]], axis=-1)
    return _slot(jnp.ones((s, MLA_NOPE), F32), cos_t), _slot(None, sin_t)


def _mla_weights(w_q_up, w_kv_up, g_q, g_k):
    swap = jnp.array(_ROPE_SWAP)
    wq = w_q_up.reshape(MLA_Q_RANK, MLA_HEADS, MLA_QK)
    wkv = w_kv_up.reshape(MLA_KV_RANK, MLA_HEADS, MLA_NOPE + MLA_V)
    flat = lambda w: w.reshape(w.shape[0], -1).astype(BF16)
    wq_slot = flat(_slot(wq[..., :MLA_NOPE], wq[..., MLA_NOPE:]))
    wqs_slot = flat(_slot(None, wq[..., MLA_NOPE:][..., swap]))
    wkn_slot = flat(_slot(wkv[..., :MLA_NOPE], None))
    wv = flat(wkv[..., MLA_NOPE:]).T
    gains = jnp.stack([_slot(g_q[:MLA_NOPE], g_q[MLA_NOPE:]), _slot(None, g_q[MLA_NOPE:][swap]),
                       _slot(g_k[:MLA_NOPE], g_k[MLA_NOPE:]), _slot(None, g_k[MLA_NOPE:][swap])])
    return wq_slot, wkn_slot, wv, gains, wqs_slot


def _mla_prep(p_mla, g_qa, g_kva, weights, rope_tabs, tm):
    b, t, n = p_mla.shape
    tm = _tile(t, tm)
    wq_slot, wkn_slot, wv, gains, wqs_slot = weights
    rope = rope_tabs is not None
    const = lambda i, j: (0, 0)
    in_specs = [pl.BlockSpec((1, tm, n), lambda i, j: (i, j, 0)),
                pl.BlockSpec((1, MLA_Q_RANK), const),
                pl.BlockSpec((1, MLA_KV_RANK), const),
                pl.BlockSpec(wq_slot.shape, const), pl.BlockSpec(wkn_slot.shape, const),
                pl.BlockSpec(wv.shape, const), pl.BlockSpec(gains.shape, const)]
    args = [p_mla, g_qa[None, :], g_kva[None, :], wq_slot, wkn_slot, wv, gains]
    if rope:
        in_specs += [pl.BlockSpec(wqs_slot.shape, const),
                     pl.BlockSpec((tm, MLA_SLOT), lambda i, j: (j, 0)),
                     pl.BlockSpec((tm, MLA_SLOT), lambda i, j: (j, 0))]
        args += [wqs_slot] + list(rope_tabs)
    qk = jax.ShapeDtypeStruct((b, MLA_HEADS, t, MLA_SLOT), BF16)
    vt = jax.ShapeDtypeStruct((b, MLA_HEADS * MLA_V, t), BF16)
    qk_spec = pl.BlockSpec((1, MLA_HEADS, tm, MLA_SLOT), lambda i, j: (i, 0, j, 0))
    vt_spec = pl.BlockSpec((1, MLA_HEADS * MLA_V, tm), lambda i, j: (i, 0, j))
    return pl.pallas_call(
        functools.partial(_mla_prep_kernel, rope=rope),
        grid=(b, t // tm),
        in_specs=in_specs,
        out_specs=[qk_spec, qk_spec, vt_spec],
        out_shape=[qk, qk, vt],
        compiler_params=_params(2),
    )(*args)


def _mix_mlp_kernel(hy_ref, na_ref, mla_ref, x_ref, gate1_ref, sh_ref, sc_ref, gate2_ref, g_ref,
                    w_hy_ref, w_na_ref, w_mla_ref, w1_ref, b1_ref, w2_ref, b2_ref, o_ref, *, chunk):
    mix = (_dot(hy_ref[0], w_hy_ref[...]) + _dot(na_ref[0], w_na_ref[...]) + _dot(mla_ref[0], w_mla_ref[...]))
    x = x_ref[0] + gate1_ref[0, 0] * mix
    h = _adaln(x, g_ref[...], sh_ref[0, 0], sc_ref[0, 0]).astype(BF16)
    d_ff = w1_ref.shape[1]
    acc = jnp.zeros(x.shape, F32)
    for c0 in range(0, d_ff, chunk):
        a = jnp.maximum(_dot(h, w1_ref[:, c0:c0 + chunk]) + b1_ref[:, c0:c0 + chunk], 0.0)
        acc = acc + _dot((a * a).astype(BF16), w2_ref[c0:c0 + chunk, :])
    o_ref[0] = x + gate2_ref[0, 0] * (acc + b2_ref[...])


def _mix_mlp(o_hy, o_na, o_mla, x, mod, w_hy, w_na, w_mla, g, w1, b1, w2, b2, tm):
    b, t, d = x.shape
    d_ff = w1.shape[1]
    tm = _tile(t, tm)
    tok = lambda n: pl.BlockSpec((1, tm, n), lambda i, j: (i, j, 0))
    modspec = lambda k: pl.BlockSpec((1, 1, 1, d), lambda i, j: (i, k, 0, 0))
    const = lambda i, j: (0, 0)
    return pl.pallas_call(
        functools.partial(_mix_mlp_kernel, chunk=_tile(d_ff, 1024)),
        grid=(b, t // tm),
        in_specs=[tok(o_hy.shape[-1]), tok(o_na.shape[-1]), tok(o_mla.shape[-1]), tok(d),
                  modspec(2), modspec(3), modspec(4), modspec(5),
                  pl.BlockSpec((1, d), const),
                  _resident(w_hy.shape, const), _resident(w_na.shape, const), _resident(w_mla.shape, const),
                  _resident((d, d_ff), const), pl.BlockSpec((1, d_ff), const),
                  _resident((d_ff, d), const), pl.BlockSpec((1, d), const)],
        out_specs=tok(d),
        out_shape=jax.ShapeDtypeStruct((b, t, d), F32),
        compiler_params=_params(2),
    )(o_hy, o_na, o_mla, x, mod, mod, mod, mod, g, w_hy, w_na, w_mla, w1, b1, w2, b2)


def kernel(x, c, ctx, c_ctx, w_mod, b_mod, g_norm1, w_in, hy_conv_w, hy_conv_b, hy_f_w1, hy_f_b1, hy_f_w2, hy_f_b2, hy_f_w3, hy_f_b3, hy_freq, hy_bias, na_g_q, na_g_k, na_rpb, mla_g_qa, mla_g_kva, mla_w_q_up, mla_w_kv_up, mla_g_q, mla_g_k, w_out, g_norm2, w_ff1, b_ff1, w_ff2, b_ff2):
    b, s, d = x.shape
    lc = ctx.shape[1]
    depth = w_mod.shape[0]
    in_hn = 3 * HY_DIM + 3 * NA_DIM
    swap = jnp.array(_ROPE_SWAP)

    n_cond = -(-(b + 1) // 8) * 8
    cond = jnp.zeros((n_cond, d), F32).at[:b].set(c).at[b].set(c_ctx)
    mods = _modulation(cond, w_mod, b_mod).reshape(depth, n_cond, 6, 1, d)

    rope_tabs = _rope_tables(s)
    cs_x = _dft_matrix(s)
    cs_c = _dft_matrix(lc)
    filt = (hy_f_w1, hy_f_b1, hy_f_w2, hy_f_b2, hy_f_w3, hy_f_b3, hy_freq)
    spec_x = _hyena_spectra(s, cs_x, *filt)
    spec_c = _hyena_spectra(lc, cs_c, *filt)
    bias_tab = _na_bias_table(na_rpb)
    hy_bias = hy_bias.reshape(2 * depth, 1, HY_DIM)

    cx = ctx
    for i in range(depth):
        last = i == depth - 1
        mod_x = mods[i, :b]
        mod_c = jnp.broadcast_to(mods[i, b], (b, 6, 1, d))
        w_hy = w_in[i, :, :3 * HY_DIM].astype(BF16)
        w_na = w_in[i, :, 3 * HY_DIM:in_hn].astype(BF16)
        w_m = w_in[i, :, in_hn:]
        w_kr = w_m[:, MLA_Q_RANK + MLA_KV_RANK:]
        w_m = jnp.concatenate([w_m[:, :MLA_Q_RANK + MLA_KV_RANK], _slot(None, w_kr),
                               _slot(None, w_kr[:, swap])], axis=-1).astype(BF16)
        g1 = g_norm1[i][None, :]
        px_hy, px_na, px_mla = _inproj(x, mod_x, g1, w_hy, w_na, w_m, 512)
        pc_hy, pc_na, pc_mla = _inproj(cx, mod_c, g1, w_hy, w_na, w_m, 256)

        qx, kx, vtx = _na_prep(px_na, na_g_q[i], na_g_k[i], 512)
        qc, kc, vtc = _na_prep(pc_na, na_g_q[i], na_g_k[i], 256)
        o_na = _na_latent(qx, kx, vtx, kc, vtc, bias_tab, i)

        mla_w = _mla_weights(mla_w_q_up[i], mla_w_kv_up[i], mla_g_q[i], mla_g_k[i])
        mqx, mkx, mvx = _mla_prep(px_mla, mla_g_qa[i], mla_g_kva[i], mla_w, rope_tabs, 512)
        mqc, mkc, mvc = _mla_prep(pc_mla, mla_g_qa[i], mla_g_kva[i], mla_w, None, 256)
        o_mla = _attention(mqx, mkx, mvx, mkc, mvc)

        o_hy = _hyena(px_hy, hy_conv_w[i], hy_conv_b[i][None, :], cs_x, *spec_x, hy_bias, i)

        wo = w_out[i].astype(BF16)
        wo_hy, wo_na, wo_mla = wo[:HY_DIM], wo[HY_DIM:HY_DIM + NA_DIM], wo[HY_DIM + NA_DIM:]
        g2 = g_norm2[i][None, :]
        w1, w2 = w_ff1[i].astype(BF16), w_ff2[i].astype(BF16)
        b1, b2 = b_ff1[i][None, :], b_ff2[i][None, :]
        x = _mix_mlp(o_hy, o_na, o_mla, x, mod_x, wo_hy, wo_na, wo_mla, g2, w1, b1, w2, b2, 512)

        if not last:
            oc_hy = _hyena(pc_hy, hy_conv_w[i], hy_conv_b[i][None, :], cs_c, *spec_c, hy_bias, i)
            oc_na = _attention(qc, kc, vtc)
            oc_mla = _attention(mqc, mkc, mvc)
            cx = _mix_mlp(oc_hy, oc_na, oc_mla, cx, mod_c, wo_hy, wo_na, wo_mla, g2, w1, b1, w2, b2, 256)
    return x
```

```python
import functools
import math

import jax
import jax.numpy as jnp
from jax import lax
from jax.experimental import pallas as pl
from jax.experimental.pallas import tpu as pltpu

F32 = jnp.float32
BF16 = jnp.bfloat16
HIGHEST = lax.Precision.HIGHEST

EPS = 1e-6
NEG_INF = -1e9
GRID_W = 64

HY_DIM = 256
HY_BANDS = 8
HY_DECAY_TARGET = 1e-2
HY_FAST = 0.3
HY_SLOW = 1.5

NA_HEADS = 4
NA_HEAD_DIM = 64
NA_DIM = NA_HEADS * NA_HEAD_DIM
NA_WIN_ROWS = 8
NA_WIN_COLS = 16

MLA_HEADS = 8
MLA_Q_RANK = 256
MLA_KV_RANK = 128
MLA_NOPE = 64
MLA_ROPE = 32
MLA_V = 64
MLA_QK = MLA_NOPE + MLA_ROPE
MLA_SLOT = 128
ROPE_BASE = 10000.0
ROPE_FREQS = MLA_ROPE // 4

VMEM_LIMIT_BYTES = 56 * 1024 * 1024
HYENA_VMEM_LIMIT_BYTES = 60 * 1024 * 1024

def _params(n_grid_dims):
    return pltpu.CompilerParams(dimension_semantics=("arbitrary",) * n_grid_dims,
                                vmem_limit_bytes=VMEM_LIMIT_BYTES)


def _tile(total, preferred):
    t = min(total, preferred)
    while total % t:
        t //= 2
    return t


def _resident(shape, index_map):
    return pl.BlockSpec(shape, index_map, pipeline_mode=pl.Buffered(1))


def _dot(a, b):
    return jnp.dot(a, b, preferred_element_type=F32)


def _split_bf16(x):
    hi = x.astype(BF16)
    return hi, (x - hi.astype(F32)).astype(BF16)


def _dot_3x(a, b):
    a_hi, a_lo = _split_bf16(a)
    b_hi, b_lo = _split_bf16(b)
    return _dot(a_hi, b_hi) + (_dot(a_hi, b_lo) + _dot(a_lo, b_hi))


def _dot_hi(a, b):
    return jnp.dot(a, b, preferred_element_type=F32, precision=HIGHEST)


def _dot_nt(a, b):
    return lax.dot_general(a, b, (((1,), (1,)), ((), ())), preferred_element_type=F32)


def _mod_kernel(cond_ref, w_ref, b_ref, o_ref):
    a = cond_ref[...]
    a = a / (1.0 + jnp.exp(-a))
    n = a.shape[0]
    a_hi, a_lo = _split_bf16(a)
    w_hi, w_lo = _split_bf16(w_ref[0])
    y = _dot(jnp.concatenate([a_hi, a_lo], axis=0), w_hi)
    o_ref[0] = y[:n] + y[n:] + _dot(a_hi, w_lo) + b_ref[0]


def _modulation(cond, w_mod, b_mod):
    depth, d, d6 = w_mod.shape
    n = cond.shape[0]
    tn = _tile(d6, 1536)
    return pl.pallas_call(
        _mod_kernel,
        grid=(depth, d6 // tn),
        in_specs=[pl.BlockSpec((n, d), lambda l, j: (0, 0)),
                  pl.BlockSpec((1, d, tn), lambda l, j: (l, 0, j)),
                  pl.BlockSpec((1, 1, tn), lambda l, j: (l, 0, j))],
        out_specs=pl.BlockSpec((1, n, tn), lambda l, j: (l, 0, j)),
        out_shape=jax.ShapeDtypeStruct((depth, n, d6), F32),
        compiler_params=_params(2),
    )(cond, w_mod, b_mod.reshape(depth, 1, d6))


def _adaln(x, g, shift, scale):
    y = x * lax.rsqrt(jnp.mean(x * x, axis=-1, keepdims=True) + EPS) * g
    return y * (1.0 + scale) + shift


def _inproj_kernel(x_ref, sh_ref, sc_ref, g_ref, w_hy_ref, w_na_ref, w_m_ref, o_hy_ref, o_na_ref, o_m_ref):
    h = _adaln(x_ref[0], g_ref[...], sh_ref[0, 0], sc_ref[0, 0]).astype(BF16)
    o_hy_ref[0] = _dot(h, w_hy_ref[...]).astype(BF16)
    o_na_ref[0] = _dot(h, w_na_ref[...])
    o_m_ref[0] = _dot(h, w_m_ref[...])


def _inproj(x, mod, g, w_hy, w_na, w_m, tm):
    b, t, d = x.shape
    tm = _tile(t, tm)
    widths = (w_hy.shape[1], w_na.shape[1], w_m.shape[1])
    return pl.pallas_call(
        _inproj_kernel,
        grid=(b, t // tm),
        in_specs=[pl.BlockSpec((1, tm, d), lambda i, j: (i, j, 0)),
                  pl.BlockSpec((1, 1, 1, d), lambda i, j: (i, 0, 0, 0)),
                  pl.BlockSpec((1, 1, 1, d), lambda i, j: (i, 1, 0, 0)),
                  pl.BlockSpec((1, d), lambda i, j: (0, 0))]
                 + [_resident((d, n), lambda i, j: (0, 0)) for n in widths],
        out_specs=[pl.BlockSpec((1, tm, n), lambda i, j: (i, j, 0)) for n in widths],
        out_shape=[jax.ShapeDtypeStruct((b, t, n), dt) for n, dt in zip(widths, (BF16, F32, F32))],
        compiler_params=_params(2),
    )(x, mod, mod, g, w_hy, w_na, w_m)


def _dft_matrix(length):
    n = 2 * length
    blk = min(128, length)
    f = jnp.arange(length, dtype=jnp.int32)[:, None, None]
    t0 = (blk * jnp.arange(length // blk, dtype=jnp.int32))[None, :, None]
    dt = jnp.arange(blk, dtype=jnp.int32)[None, None, :]
    a = ((f * t0) % n).astype(F32) * (2.0 * math.pi / n)
    b = ((f * dt) % n).astype(F32) * (2.0 * math.pi / n)
    ca, sa, cb, sb = jnp.cos(a), jnp.sin(a), jnp.cos(b), jnp.sin(b)
    cos_m = (ca * cb - sa * sb).reshape(length, length)
    sin_m = (sa * cb + ca * sb).reshape(length, length)
    return jnp.concatenate([cos_m, sin_m], axis=1).astype(BF16)


def _filt_kernel(z_ref, dec_ref, w1_ref, b1_ref, w2_ref, b2_ref, w3_ref, b3_ref, fr_ref, hsum_ref, hdiff_ref):
    z = z_ref[...]
    fr = fr_ref[0]
    a = jnp.sin(fr[0:1] * (_dot_3x(z, w1_ref[0]) + b1_ref[0]))
    a = jnp.sin(fr[1:2] * (_dot_3x(a, w2_ref[0]) + b2_ref[0]))
    h = _dot_3x(a, w3_ref[0]) + b3_ref[0]
    dec = dec_ref[...]
    c = HY_DIM
    row0 = lax.broadcasted_iota(jnp.int32, dec.shape, 0) == 0
    for o in range(2):
        hf = h[:, (2 * o) * c:(2 * o + 1) * c] * dec
        hb = h[:, (2 * o + 1) * c:(2 * o + 2) * c] * dec
        nrm = (jnp.sum(jnp.abs(hf), axis=0, keepdims=True)
               + jnp.sum(jnp.abs(hb), axis=0, keepdims=True) + EPS)
        hf = hf / nrm
        hb = jnp.where(row0, 0.0, hb / nrm)
        hsum_ref[0, :, o * c:(o + 1) * c] = hf + hb
        hdiff_ref[0, :, o * c:(o + 1) * c] = hf - hb


def _alternating_sign(shape):
    row = lax.broadcasted_iota(jnp.int32, shape, 0)
    return (1 - 2 * (row % 2)).astype(F32)


def _spec_kernel(hsum_ref, hdiff_ref, cs_ref, hc_ref, hs_ref, hn_ref):
    length = hsum_ref.shape[1]
    hsum = hsum_ref[0]
    hdiff = hdiff_ref[0]

    def split_dot(m, v):
        hi = v.astype(BF16)
        lo = (v - hi.astype(F32)).astype(BF16)
        return _dot(m, hi) + _dot(m, lo)

    row = lax.broadcasted_iota(jnp.int32, hsum.shape, 0)
    wf = jnp.where(row == 0, 0.5 / length, 1.0 / length)
    hc_ref[0] = split_dot(cs_ref[:, :length], hsum) * wf
    hs_ref[0] = split_dot(cs_ref[:, length:], hdiff) * wf
    hn_ref[0] = jnp.sum(_alternating_sign(hsum.shape) * hsum, axis=0, keepdims=True) * (0.5 / length)


def _hyena_spectra(length, cs, f_w1, f_b1, f_w2, f_b2, f_w3, f_b3, freq):
    depth = f_w1.shape[0]
    pos_dim = f_w1.shape[1]
    hid = f_w1.shape[2]
    kpad = 32
    t = jnp.linspace(0.0, 1.0, length, dtype=F32)[:, None]
    w = (2.0 * math.pi / length) * jnp.arange(length, dtype=F32)[:, None]
    bands = jnp.linspace(1e-4, HY_BANDS - 1, HY_BANDS, dtype=F32)
    z = jnp.concatenate([t, jnp.cos(bands * w), -jnp.sin(bands * w)], axis=-1)
    z = jnp.pad(z, ((0, 0), (0, kpad - pos_dim)))
    w1 = jnp.pad(f_w1, ((0, 0), (0, kpad - pos_dim), (0, 0)))
    deltas = jnp.linspace(math.log(HY_DECAY_TARGET) / HY_SLOW, math.log(HY_DECAY_TARGET) / HY_FAST,
                          HY_DIM, dtype=F32)
    decay = jnp.exp(-t * jnp.abs(deltas))
    c2 = 2 * HY_DIM
    c4 = 4 * HY_DIM
    lay3 = lambda l: (l, 0, 0)
    hsum, hdiff = pl.pallas_call(
        _filt_kernel,
        grid=(depth,),
        in_specs=[pl.BlockSpec((length, kpad), lambda l: (0, 0)),
                  pl.BlockSpec((length, HY_DIM), lambda l: (0, 0)),
                  pl.BlockSpec((1, kpad, hid), lay3),
                  pl.BlockSpec((1, 1, hid), lay3),
                  pl.BlockSpec((1, hid, hid), lay3),
                  pl.BlockSpec((1, 1, hid), lay3),
                  pl.BlockSpec((1, hid, c4), lay3),
                  pl.BlockSpec((1, 1, c4), lay3),
                  pl.BlockSpec((1, 2, hid), lay3)],
        out_specs=[pl.BlockSpec((1, length, c2), lay3),
                   pl.BlockSpec((1, length, c2), lay3)],
        out_shape=[jax.ShapeDtypeStruct((depth, length, c2), F32),
                   jax.ShapeDtypeStruct((depth, length, c2), F32)],
        compiler_params=_params(1),
    )(z, decay, w1, f_b1.reshape(depth, 1, hid), f_w2, f_b2.reshape(depth, 1, hid),
      f_w3, f_b3.reshape(depth, 1, c4), freq)
    blk_in = pl.BlockSpec((1, length, c2), lay3, pipeline_mode=pl.Buffered(1))
    blk_out = pl.BlockSpec((1, length, c2), lay3)
    return pl.pallas_call(
        _spec_kernel,
        grid=(depth,),
        in_specs=[blk_in, blk_in, _resident((length, 2 * length), lambda l: (0, 0))],
        out_specs=[blk_out, blk_out, pl.BlockSpec((1, 1, c2), lay3)],
        out_shape=[jax.ShapeDtypeStruct((depth, length, c2), F32),
                   jax.ShapeDtypeStruct((depth, length, c2), F32),
                   jax.ShapeDtypeStruct((depth, 1, c2), F32)],
        compiler_params=pltpu.CompilerParams(dimension_semantics=("arbitrary",),
                                             vmem_limit_bytes=HYENA_VMEM_LIMIT_BYTES),
    )(hsum, hdiff, cs)


HY_ROWS = 256
HY_HALO = 8


HY_PAIR = 2


def _hyena_kernel(pv_ref, pg_ref, wv_ref, bv_ref, wg_ref, bg_ref, cs_ref, hc_ref, hs_ref, hn_ref, bias_ref,
                  o_ref, z_scr, xc_scr, xs_scr, yc_scr, ys_scr):
    length = pv_ref.shape[1]
    c = HY_DIM
    rows = min(HY_ROWS, length)
    chunks = [(r0, rows) for r0 in range(0, length, rows)]
    halves = [(i, slice(i * c, (i + 1) * c)) for i in range(HY_PAIR)]

    def short_conv_rows(p_ref, w_ref, b_ref, i, r0):
        lo = max(r0 - HY_HALO, 0)
        hi = min(r0 + rows + HY_HALO, length)
        u = p_ref[i, lo:hi, :].astype(F32)
        row = lax.broadcasted_iota(jnp.int32, u.shape, 0) + lo
        prev = jnp.where(row == 0, 0.0, pltpu.roll(u, 1, 0))
        nxt = jnp.where(row == length - 1, 0.0, pltpu.roll(u, hi - lo - 1, 0))
        y = prev * w_ref[0:1, :] + u * w_ref[1:2, :] + nxt * w_ref[2:3, :] + b_ref[...]
        return y[r0 - lo:r0 - lo + rows]

    @pl.when(pl.program_id(1) == 0)
    def _():
        for r0, n in chunks:
            for i, lanes in halves:
                z_scr[r0:r0 + n, lanes] = short_conv_rows(pv_ref, wv_ref, bv_ref, i, r0)

    sign = _alternating_sign((rows, c))
    zb = z_scr[...].astype(BF16)
    xc_scr[...] = _dot(cs_ref[:, :length], zb)
    xs_scr[...] = _dot(cs_ref[:, length:], zb)
    xn = [jnp.zeros((1, c), F32) for _ in halves]
    for r0, n in chunks:
        hc, hs = hc_ref[0, r0:r0 + n, :], hs_ref[0, r0:r0 + n, :]
        for i, lanes in halves:
            xc, xs = xc_scr[r0:r0 + n, lanes], xs_scr[r0:r0 + n, lanes]
            yc_scr[r0:r0 + n, lanes] = (xc * hc - xs * hs).astype(BF16)
            ys_scr[r0:r0 + n, lanes] = (xc * hs + xs * hc).astype(BF16)
            xn[i] = xn[i] + jnp.sum(sign * z_scr[r0:r0 + n, lanes], axis=0, keepdims=True)
    xc_scr[...] = _dot(cs_ref[:, :length], yc_scr[...])
    xs_scr[...] = _dot(cs_ref[:, length:], ys_scr[...])
    bias = bias_ref[0]
    for r0, n in chunks:
        for i, lanes in halves:
            z = z_scr[r0:r0 + n, lanes]
            conv = xc_scr[r0:r0 + n, lanes] + xs_scr[r0:r0 + n, lanes] + sign * (xn[i] * hn_ref[0])
            z = short_conv_rows(pg_ref, wg_ref, bg_ref, i, r0) * (conv + z * bias)
            z_scr[r0:r0 + n, lanes] = z
            o_ref[i, r0:r0 + n, :] = z.astype(o_ref.dtype)


def _hyena(p_hy, conv_w, conv_b, cs, hc, hs, hn, bias, layer):
    b, length, _ = p_hy.shape
    assert b % HY_PAIR == 0
    c = HY_DIM
    wide = HY_PAIR * c
    return pl.pallas_call(
        _hyena_kernel,
        grid=(b // HY_PAIR, 2),
        in_specs=[pl.BlockSpec((HY_PAIR, length, c), lambda i, o: (i, 0, 0), pipeline_mode=pl.Buffered(1)),
                  pl.BlockSpec((HY_PAIR, length, c), lambda i, o: (i, 0, o + 1)),
                  pl.BlockSpec((3, c), lambda i, o: (0, 0)),
                  pl.BlockSpec((1, c), lambda i, o: (0, 0)),
                  pl.BlockSpec((3, c), lambda i, o: (0, o + 1)),
                  pl.BlockSpec((1, c), lambda i, o: (0, o + 1)),
                  _resident((length, 2 * length), lambda i, o: (0, 0)),
                  pl.BlockSpec((1, length, c), lambda i, o: (layer, 0, o), pipeline_mode=pl.Buffered(1)),
                  pl.BlockSpec((1, length, c), lambda i, o: (layer, 0, o), pipeline_mode=pl.Buffered(1)),
                  pl.BlockSpec((1, 1, c), lambda i, o: (layer, 0, o)),
                  pl.BlockSpec((1, 1, c), lambda i, o: (2 * layer + o, 0, 0))],
        out_specs=pl.BlockSpec((HY_PAIR, length, c), lambda i, o: (i, 0, 0)),
        out_shape=jax.ShapeDtypeStruct((b, length, c), BF16),
        scratch_shapes=[pltpu.VMEM((length, wide), F32), pltpu.VMEM((length, wide), F32),
                        pltpu.VMEM((length, wide), F32), pltpu.VMEM((length, wide), BF16),
                        pltpu.VMEM((length, wide), BF16)],
        compiler_params=pltpu.CompilerParams(dimension_semantics=("arbitrary", "arbitrary"),
                                             vmem_limit_bytes=HYENA_VMEM_LIMIT_BYTES),
    )(p_hy, p_hy, conv_w, conv_b, conv_w, conv_b, cs, hc, hs, hn, bias)


def _na_prep_kernel(q_ref, k_ref, v_ref, gq_ref, gk_ref, bd_ref, qo_ref, ko_ref, vo_ref):
    bd = bd_ref[...]

    def head_rms(x, g):
        sq = x * x
        hi = sq.astype(BF16)
        lo = (sq - hi.astype(F32)).astype(BF16)
        ms = (_dot(hi, bd) + _dot(lo, bd)) * (1.0 / NA_HEAD_DIM)
        return x * lax.rsqrt(ms + EPS) * g

    q = (head_rms(q_ref[0], gq_ref[...]) * (NA_HEAD_DIM ** -0.5 * LOG2E)).astype(BF16)
    k = head_rms(k_ref[0], gk_ref[...]).astype(BF16)
    for h in range(NA_HEADS):
        sl = slice(h * NA_HEAD_DIM, (h + 1) * NA_HEAD_DIM)
        qo_ref[0, h] = q[:, sl]
        ko_ref[0, h] = k[:, sl]
    vo_ref[0] = v_ref[0].T.astype(BF16)


def _na_prep(p_hn, g_q, g_k, tm):
    b, t, _ = p_hn.shape
    tm = _tile(t, tm)
    gq = jnp.tile(g_q, NA_HEADS)[None, :]
    gk = jnp.tile(g_k, NA_HEADS)[None, :]
    head = jnp.arange(NA_DIM) // NA_HEAD_DIM
    bd = (head[:, None] == head[None, :]).astype(BF16)
    out = jax.ShapeDtypeStruct((b, NA_HEADS, t, NA_HEAD_DIM), BF16)
    ospec = pl.BlockSpec((1, NA_HEADS, tm, NA_HEAD_DIM), lambda i, j: (i, 0, j, 0))
    out_t = jax.ShapeDtypeStruct((b, NA_DIM, t), BF16)
    ospec_t = pl.BlockSpec((1, NA_DIM, tm), lambda i, j: (i, 0, j))
    return pl.pallas_call(
        _na_prep_kernel,
        grid=(b, t // tm),
        in_specs=[pl.BlockSpec((1, tm, NA_DIM), lambda i, j: (i, j, 0)),
                  pl.BlockSpec((1, tm, NA_DIM), lambda i, j: (i, j, 1)),
                  pl.BlockSpec((1, tm, NA_DIM), lambda i, j: (i, j, 2)),
                  pl.BlockSpec((1, NA_DIM), lambda i, j: (0, 0)),
                  pl.BlockSpec((1, NA_DIM), lambda i, j: (0, 0)),
                  pl.BlockSpec((NA_DIM, NA_DIM), lambda i, j: (0, 0))],
        out_specs=[ospec, ospec, ospec_t],
        out_shape=[out, out, out_t],
        compiler_params=_params(2),
    )(p_hn, p_hn, p_hn, gq, gk, bd)


NA_QROWS = 8
NA_KROWS = NA_QROWS + NA_WIN_ROWS


def _rpb_expand_kernel(r_ref, oh_ref, m_ref, o_ref):
    o_ref[...] = (_dot_hi(r_ref[...], oh_ref[...]) + m_ref[...]) * LOG2E


def _na_first_key_row(rb, rows):
    return jnp.clip(NA_QROWS * rb - NA_WIN_ROWS // 2, 0, rows - NA_KROWS)


def _na_block_layouts(rows):
    nrb = rows // NA_QROWS

    def layout(rb):
        w0 = min(max(NA_QROWS * rb - NA_WIN_ROWS // 2, 0), rows - NA_KROWS)
        out = []
        for j in range(NA_QROWS):
            r = NA_QROWS * rb + j
            row0 = min(max(r - NA_WIN_ROWS // 2, 0), rows - NA_WIN_ROWS)
            assert w0 <= row0 and row0 + NA_WIN_ROWS <= w0 + NA_KROWS
            out.append((r - w0, row0 - w0))
        return tuple(out)

    assert all(layout(rb) == layout(1) for rb in range(1, nrb - 1))
    return [layout(0), layout(min(1, nrb - 1)), layout(nrb - 1)]


def _na_table_kernel(t_ref, o_ref, *, layouts):
    neg = jnp.full((GRID_W, GRID_W), NEG_INF, F32)
    for pos, layout in enumerate(layouts):
        for i in range(NA_KROWS):
            pieces = [t_ref[0, i - r_rel + NA_WIN_ROWS - 1] if row0_rel <= i < row0_rel + NA_WIN_ROWS else neg
                      for r_rel, row0_rel in layout]
            o_ref[0, pos, i * GRID_W:(i + 1) * GRID_W, :] = jnp.concatenate(pieces, axis=1)


def _na_bias_table(rpb, rows):
    depth, heads, n_dr, n_dc = rpb.shape
    kc = jnp.arange(GRID_W)[:, None]
    qc = jnp.arange(GRID_W)[None, :]
    dc = jnp.clip(kc - qc + NA_WIN_COLS - 1, 0, n_dc - 1)
    cstart = jnp.clip(qc - NA_WIN_COLS // 2, 0, GRID_W - NA_WIN_COLS)
    valid = ((kc >= cstart) & (kc < cstart + NA_WIN_COLS)).reshape(1, GRID_W * GRID_W)
    n_pad = 32
    onehot = ((dc.reshape(1, -1) == jnp.arange(n_pad)[:, None]) & valid).astype(F32)
    mask = jnp.where(valid, 0.0, NEG_INF).astype(F32)
    rows = depth * heads * n_dr
    rpb2 = jnp.pad(rpb.reshape(rows, n_dc).astype(F32), ((0, 0), (0, n_pad - n_dc)))
    t1 = pl.pallas_call(
        _rpb_expand_kernel,
        out_shape=jax.ShapeDtypeStruct((rows, GRID_W * GRID_W), F32),
        compiler_params=pltpu.CompilerParams(vmem_limit_bytes=VMEM_LIMIT_BYTES),
    )(rpb2, onehot, mask).reshape(depth * heads, n_dr, GRID_W, GRID_W)
    nk, nq = NA_KROWS * GRID_W, NA_QROWS * GRID_W
    tab = pl.pallas_call(
        functools.partial(_na_table_kernel, layouts=_na_block_layouts(rows)),
        grid=(depth * heads,),
        in_specs=[pl.BlockSpec((1, n_dr, GRID_W, GRID_W), lambda g: (g, 0, 0, 0))],
        out_specs=pl.BlockSpec((1, 3, nk, nq), lambda g: (g, 0, 0, 0)),
        out_shape=jax.ShapeDtypeStruct((depth * heads, 3, nk, nq), F32),
        compiler_params=_params(1),
    )(t1)
    return tab.reshape(depth, heads, 3, nk, nq)


ATTN_CHUNK_ELEMS = 64 * 1024
LOG2E = math.log2(math.e)


def _attend_heads(heads, q_of, sets_of, s_scrs):
    row_max = [None] * heads
    outs = [None] * heads

    def score_phase(h):
        q = q_of(h)
        scr = s_scrs[h % 2]
        off, m = 0, None
        for n, k_fn, _, bias_fn in sets_of(h):
            s = _dot_nt(k_fn(), q)
            if bias_fn is not None:
                s = s + bias_fn()
            scr[off:off + n, :] = s
            mj = jnp.max(s, axis=0, keepdims=True)
            m = mj if m is None else jnp.maximum(m, mj)
            off += n
            yield
        row_max[h] = m

    def value_phase(h):
        scr = s_scrs[h % 2]
        m = row_max[h]
        base, den, acc = 0, None, None
        for n_set, _, vt_fn, _ in sets_of(h):
            for off, n in _key_chunks(n_set, ATTN_CHUNK_ELEMS // scr.shape[1]):
                p = jnp.exp2(scr[base + off:base + off + n, :] - m)
                dj = jnp.sum(p, axis=0, keepdims=True)
                oj = _dot(vt_fn(off, n), p.astype(BF16))
                den = dj if den is None else den + dj
                acc = oj if acc is None else acc + oj
                yield
            base += n_set
        outs[h] = acc / den

    for _ in score_phase(0):
        pass
    for h in range(heads):
        nxt = score_phase(h + 1) if h + 1 < heads else iter(())
        cur = value_phase(h)
        done_n = done_c = False
        while not (done_n and done_c):
            if not done_n:
                done_n = next(nxt, "end") == "end"
            if not done_c:
                done_c = next(cur, "end") == "end"
    return jnp.concatenate(outs, axis=0)


def _key_chunks(total, chunk):
    sizes = [chunk] * (total // chunk)
    if total % chunk:
        sizes.append(total % chunk)
    offs = [sum(sizes[:i]) for i in range(len(sizes))]
    return list(zip(offs, sizes))


def _na_kernel(q_ref, k_ref, vt_ref, kc_ref, vct_ref, b_ref, o_ref, s0_scr, s1_scr, *, rows):
    start = pl.multiple_of(_na_first_key_row(pl.program_id(0), rows) * GRID_W, NA_WIN_ROWS // 2 * GRID_W)
    nk = NA_KROWS * GRID_W
    dh = NA_HEAD_DIM

    def sets_of(h):
        hs = slice(h * dh, (h + 1) * dh)
        window = (nk,
                  lambda: k_ref[0, h, pl.ds(start, nk), :],
                  lambda off, n: vt_ref[0, hs, pl.ds(pl.multiple_of(start + off, 128), n)],
                  lambda: b_ref[0, h, 0])
        context = (kc_ref.shape[2],
                   lambda: kc_ref[0, h],
                   lambda off, n: vct_ref[0, hs, off:off + n],
                   None)
        return [window, context]

    o_t = _attend_heads(NA_HEADS, lambda h: q_ref[0, h], sets_of, (s0_scr, s1_scr))
    o_ref[0] = o_t.T.astype(o_ref.dtype)


def _na_latent(q, k, vt, kc, vct, bias_tab, layer):
    b, heads, s, dh = q.shape
    ctx = kc.shape[2]
    rows = s // GRID_W
    assert rows % NA_QROWS == 0 and rows >= NA_KROWS and (rows - NA_KROWS) % (NA_WIN_ROWS // 2) == 0
    nrb = rows // NA_QROWS
    tq = NA_QROWS * GRID_W

    def bias_index(rb, i):
        return (layer, 0, jnp.where(rb == 0, 0, jnp.where(rb == nrb - 1, 2, 1)), 0, 0)

    full4 = lambda rb, i: (i, 0, 0, 0)
    full3 = lambda rb, i: (i, 0, 0)
    return pl.pallas_call(
        functools.partial(_na_kernel, rows=rows),
        grid=(nrb, b),
        in_specs=[pl.BlockSpec((1, heads, tq, dh), lambda rb, i: (i, 0, rb, 0)),
                  pl.BlockSpec((1, heads, s, dh), full4),
                  pl.BlockSpec((1, heads * dh, s), full3),
                  pl.BlockSpec((1, heads, ctx, dh), full4),
                  pl.BlockSpec((1, heads * dh, ctx), full3),
                  pl.BlockSpec((1, heads, 1, NA_KROWS * GRID_W, tq), bias_index)],
        out_specs=pl.BlockSpec((1, tq, heads * dh), lambda rb, i: (i, rb, 0)),
        out_shape=jax.ShapeDtypeStruct((b, s, heads * dh), BF16),
        scratch_shapes=[pltpu.VMEM((NA_KROWS * GRID_W + ctx, tq), F32)] * 2,
        compiler_params=_params(2),
    )(q, k, vt, kc, vct, bias_tab)


def _attn_kernel(*refs, heads, dv, two_sets):
    if two_sets:
        q_ref, k1_ref, vt1_ref, k2_ref, vt2_ref, o_ref, s0_scr, s1_scr = refs
        key_sets = ((k1_ref, vt1_ref), (k2_ref, vt2_ref))
    else:
        q_ref, k1_ref, vt1_ref, o_ref, s0_scr, s1_scr = refs
        key_sets = ((k1_ref, vt1_ref),)

    def sets_of(h):
        hs = slice(h * dv, (h + 1) * dv)
        return [(k_ref.shape[2],
                 lambda k_ref=k_ref: k_ref[0, h],
                 lambda off, n, vt_ref=vt_ref: vt_ref[0, hs, off:off + n],
                 None) for k_ref, vt_ref in key_sets]

    o_t = _attend_heads(heads, lambda h: q_ref[0, h], sets_of, (s0_scr, s1_scr))
    o_ref[0] = o_t.T.astype(o_ref.dtype)


def _attention(q, k1, vt1, k2=None, vt2=None, tq=512):
    b, heads, t, dq = q.shape
    dv = vt1.shape[1] // heads
    tq = _tile(t, tq)
    two_sets = k2 is not None
    n_keys = k1.shape[2] + (k2.shape[2] if two_sets else 0)
    full4 = lambda i, j: (i, 0, 0, 0)
    full3 = lambda i, j: (i, 0, 0)
    in_specs = [pl.BlockSpec((1, heads, tq, dq), lambda i, j: (i, 0, j, 0)),
                pl.BlockSpec((1,) + k1.shape[1:], full4),
                pl.BlockSpec((1,) + vt1.shape[1:], full3)]
    args = [q, k1, vt1]
    if two_sets:
        in_specs += [pl.BlockSpec((1,) + k2.shape[1:], full4),
                     pl.BlockSpec((1,) + vt2.shape[1:], full3)]
        args += [k2, vt2]
    return pl.pallas_call(
        functools.partial(_attn_kernel, heads=heads, dv=dv, two_sets=two_sets),
        grid=(b, t // tq),
        in_specs=in_specs,
        out_specs=pl.BlockSpec((1, tq, heads * dv), lambda i, j: (i, j, 0)),
        out_shape=jax.ShapeDtypeStruct((b, t, heads * dv), BF16),
        scratch_shapes=[pltpu.VMEM((n_keys, tq), F32)] * 2,
        compiler_params=_params(2),
    )(*args)


def _mla_prep_kernel(*refs, rope):
    if rope:
        (p_ref, gqa_ref, gkva_ref, wq_ref, wkn_ref, wv_ref, g_ref, wqs_ref, cos_ref, sin_ref,
         qo_ref, ko_ref, vto_ref) = refs
    else:
        p_ref, gqa_ref, gkva_ref, wq_ref, wkn_ref, wv_ref, g_ref, qo_ref, ko_ref, vto_ref = refs
    p = p_ref[0]
    a, b_ = MLA_Q_RANK, MLA_Q_RANK + MLA_KV_RANK
    cq, ckv = p[:, :a], p[:, a:b_]
    kr, krs = p[:, b_:b_ + MLA_SLOT], p[:, b_ + MLA_SLOT:b_ + 2 * MLA_SLOT]

    def rms(x, g):
        return x * lax.rsqrt(jnp.mean(x * x, axis=-1, keepdims=True) + EPS) * g

    cqn = rms(cq, gqa_ref[...]).astype(BF16)
    ckvn = rms(ckv, gkva_ref[...]).astype(BF16)
    qa = _dot(cqn, wq_ref[...])
    kn = _dot(ckvn, wkn_ref[...])
    g = g_ref[...]
    aq, ak = g[0:1], g[2:3]
    if rope:
        qs = _dot(cqn, wqs_ref[...])
        cos_t, sin_t = cos_ref[...], sin_ref[...]
        aq, bq = aq * cos_t, g[1:2] * sin_t
        ak, k_rot = ak * cos_t, krs * (g[3:4] * sin_t)

    def inv_rms(x):
        return lax.rsqrt(jnp.sum(x * x, axis=-1, keepdims=True) * (1.0 / MLA_QK) + EPS)

    for h in range(MLA_HEADS):
        sl = slice(h * MLA_SLOT, (h + 1) * MLA_SLOT)
        xq = qa[:, sl]
        yq = xq * aq
        if rope:
            yq = yq + qs[:, sl] * bq
        qo_ref[0, h] = (yq * (inv_rms(xq) * (MLA_QK ** -0.5 * LOG2E))).astype(BF16)
        xk = kn[:, sl] + kr
        yk = xk * ak
        if rope:
            yk = yk + k_rot
        ko_ref[0, h] = (yk * inv_rms(xk)).astype(BF16)
    vto_ref[0] = _dot_nt(wv_ref[...], ckvn).astype(BF16)


_ROPE_SWAP = tuple(list(range(8, 16)) + list(range(0, 8)) + list(range(24, 32)) + list(range(16, 24)))


def _slot(nope, rope_part):
    lead = (nope if nope is not None else rope_part).shape[:-1]
    dt = (nope if nope is not None else rope_part).dtype
    z = lambda n: jnp.zeros(lead + (n,), dt)
    return jnp.concatenate([nope if nope is not None else z(MLA_NOPE),
                            rope_part if rope_part is not None else z(MLA_ROPE),
                            z(MLA_SLOT - MLA_QK)], axis=-1)


def _rope_tables(s):
    t = jnp.arange(s)
    pos = jnp.stack([t // GRID_W, t % GRID_W], axis=-1).astype(F32)
    inv = ROPE_BASE ** (-jnp.arange(ROPE_FREQS, dtype=F32) / ROPE_FREQS)
    ang = pos[:, :, None] * inv
    cos, sin = jnp.cos(ang), jnp.sin(ang)
    cos_t = jnp.concatenate([cos[:, 0], cos[:, 0], cos[:, 1], cos[:, 1]], axis=-1)
    sin_t = jnp.concatenate([-sin[:, 0], sin[:, 0], -sin[:, 1], sin[:, 1]], axis=-1)
    return _slot(jnp.ones((s, MLA_NOPE), F32), cos_t), _slot(None, sin_t)


def _mla_weights(w_q_up, w_kv_up, g_q, g_k):
    swap = jnp.array(_ROPE_SWAP)
    wq = w_q_up.reshape(MLA_Q_RANK, MLA_HEADS, MLA_QK)
    wkv = w_kv_up.reshape(MLA_KV_RANK, MLA_HEADS, MLA_NOPE + MLA_V)
    flat = lambda w: w.reshape(w.shape[0], -1).astype(BF16)
    wq_slot = flat(_slot(wq[..., :MLA_NOPE], wq[..., MLA_NOPE:]))
    wqs_slot = flat(_slot(None, wq[..., MLA_NOPE:][..., swap]))
    wkn_slot = flat(_slot(wkv[..., :MLA_NOPE], None))
    wv = flat(wkv[..., MLA_NOPE:]).T
    gains = jnp.stack([_slot(g_q[:MLA_NOPE], g_q[MLA_NOPE:]), _slot(None, g_q[MLA_NOPE:][swap]),
                       _slot(g_k[:MLA_NOPE], g_k[MLA_NOPE:]), _slot(None, g_k[MLA_NOPE:][swap])])
    return wq_slot, wkn_slot, wv, gains, wqs_slot


def _mla_prep(p_mla, g_qa, g_kva, weights, rope_tabs, tm):
    b, t, n = p_mla.shape
    tm = _tile(t, tm)
    wq_slot, wkn_slot, wv, gains, wqs_slot = weights
    rope = rope_tabs is not None
    const = lambda i, j: (0, 0)
    in_specs = [pl.BlockSpec((1, tm, n), lambda i, j: (i, j, 0)),
                pl.BlockSpec((1, MLA_Q_RANK), const),
                pl.BlockSpec((1, MLA_KV_RANK), const),
                pl.BlockSpec(wq_slot.shape, const), pl.BlockSpec(wkn_slot.shape, const),
                pl.BlockSpec(wv.shape, const), pl.BlockSpec(gains.shape, const)]
    args = [p_mla, g_qa[None, :], g_kva[None, :], wq_slot, wkn_slot, wv, gains]
    if rope:
        in_specs += [pl.BlockSpec(wqs_slot.shape, const),
                     pl.BlockSpec((tm, MLA_SLOT), lambda i, j: (j, 0)),
                     pl.BlockSpec((tm, MLA_SLOT), lambda i, j: (j, 0))]
        args += [wqs_slot] + list(rope_tabs)
    qk = jax.ShapeDtypeStruct((b, MLA_HEADS, t, MLA_SLOT), BF16)
    vt = jax.ShapeDtypeStruct((b, MLA_HEADS * MLA_V, t), BF16)
    qk_spec = pl.BlockSpec((1, MLA_HEADS, tm, MLA_SLOT), lambda i, j: (i, 0, j, 0))
    vt_spec = pl.BlockSpec((1, MLA_HEADS * MLA_V, tm), lambda i, j: (i, 0, j))
    return pl.pallas_call(
        functools.partial(_mla_prep_kernel, rope=rope),
        grid=(b, t // tm),
        in_specs=in_specs,
        out_specs=[qk_spec, qk_spec, vt_spec],
        out_shape=[qk, qk, vt],
        compiler_params=_params(2),
    )(*args)


def _mix_mlp_kernel(hy_ref, na_ref, mla_ref, x_ref, gate1_ref, sh_ref, sc_ref, gate2_ref, g_ref,
                    w_hy_ref, w_na_ref, w_mla_ref, w1_ref, b1_ref, w2_ref, b2_ref, o_ref, *, chunk):
    mix = (_dot(hy_ref[0], w_hy_ref[...]) + _dot(na_ref[0], w_na_ref[...]) + _dot(mla_ref[0], w_mla_ref[...]))
    x = x_ref[0] + gate1_ref[0, 0] * mix
    h = _adaln(x, g_ref[...], sh_ref[0, 0], sc_ref[0, 0]).astype(BF16)
    d_ff = w1_ref.shape[1]
    acc = jnp.zeros(x.shape, F32)
    for c0 in range(0, d_ff, chunk):
        a = jnp.maximum(_dot(h, w1_ref[:, c0:c0 + chunk]) + b1_ref[:, c0:c0 + chunk], 0.0)
        acc = acc + _dot((a * a).astype(BF16), w2_ref[c0:c0 + chunk, :])
    o_ref[0] = x + gate2_ref[0, 0] * (acc + b2_ref[...])


def _mix_mlp(o_hy, o_na, o_mla, x, mod, w_hy, w_na, w_mla, g, w1, b1, w2, b2, tm):
    b, t, d = x.shape
    d_ff = w1.shape[1]
    tm = _tile(t, tm)
    tok = lambda n: pl.BlockSpec((1, tm, n), lambda i, j: (i, j, 0))
    modspec = lambda k: pl.BlockSpec((1, 1, 1, d), lambda i, j: (i, k, 0, 0))
    const = lambda i, j: (0, 0)
    return pl.pallas_call(
        functools.partial(_mix_mlp_kernel, chunk=_tile(d_ff, 1024)),
        grid=(b, t // tm),
        in_specs=[tok(o_hy.shape[-1]), tok(o_na.shape[-1]), tok(o_mla.shape[-1]), tok(d),
                  modspec(2), modspec(3), modspec(4), modspec(5),
                  pl.BlockSpec((1, d), const),
                  _resident(w_hy.shape, const), _resident(w_na.shape, const), _resident(w_mla.shape, const),
                  _resident((d, d_ff), const), pl.BlockSpec((1, d_ff), const),
                  _resident((d_ff, d), const), pl.BlockSpec((1, d), const)],
        out_specs=tok(d),
        out_shape=jax.ShapeDtypeStruct((b, t, d), F32),
        compiler_params=_params(2),
    )(o_hy, o_na, o_mla, x, mod, mod, mod, mod, g, w_hy, w_na, w_mla, w1, b1, w2, b2)


def kernel(x, c, ctx, c_ctx, w_mod, b_mod, g_norm1, w_in, hy_conv_w, hy_conv_b, hy_f_w1, hy_f_b1, hy_f_w2, hy_f_b2, hy_f_w3, hy_f_b3, hy_freq, hy_bias, na_g_q, na_g_k, na_rpb, mla_g_qa, mla_g_kva, mla_w_q_up, mla_w_kv_up, mla_g_q, mla_g_k, w_out, g_norm2, w_ff1, b_ff1, w_ff2, b_ff2):
    b, s, d = x.shape
    lc = ctx.shape[1]
    depth = w_mod.shape[0]
    in_hn = 3 * HY_DIM + 3 * NA_DIM
    swap = jnp.array(_ROPE_SWAP)

    n_cond = -(-(b + 1) // 8) * 8
    cond = jnp.zeros((n_cond, d), F32).at[:b].set(c).at[b].set(c_ctx)
    mods = _modulation(cond, w_mod, b_mod).reshape(depth, n_cond, 6, 1, d)

    rope_tabs = _rope_tables(s)
    cs_x = _dft_matrix(s)
    cs_c = _dft_matrix(lc)
    filt = (hy_f_w1, hy_f_b1, hy_f_w2, hy_f_b2, hy_f_w3, hy_f_b3, hy_freq)
    spec_x = _hyena_spectra(s, cs_x, *filt)
    spec_c = _hyena_spectra(lc, cs_c, *filt)
    bias_tab = _na_bias_table(na_rpb, s // GRID_W)
    hy_bias = hy_bias.reshape(2 * depth, 1, HY_DIM)

    cx = ctx
    for i in range(depth):
        last = i == depth - 1
        mod_x = mods[i, :b]
        mod_c = jnp.broadcast_to(mods[i, b], (b, 6, 1, d))
        w_hy = w_in[i, :, :3 * HY_DIM].astype(BF16)
        w_na = w_in[i, :, 3 * HY_DIM:in_hn].astype(BF16)
        w_m = w_in[i, :, in_hn:]
        w_kr = w_m[:, MLA_Q_RANK + MLA_KV_RANK:]
        w_m = jnp.concatenate([w_m[:, :MLA_Q_RANK + MLA_KV_RANK], _slot(None, w_kr),
                               _slot(None, w_kr[:, swap])], axis=-1).astype(BF16)
        g1 = g_norm1[i][None, :]
        px_hy, px_na, px_mla = _inproj(x, mod_x, g1, w_hy, w_na, w_m, 512)
        pc_hy, pc_na, pc_mla = _inproj(cx, mod_c, g1, w_hy, w_na, w_m, 256)

        qx, kx, vtx = _na_prep(px_na, na_g_q[i], na_g_k[i], 512)
        qc, kc, vtc = _na_prep(pc_na, na_g_q[i], na_g_k[i], 256)
        o_na = _na_latent(qx, kx, vtx, kc, vtc, bias_tab, i)

        mla_w = _mla_weights(mla_w_q_up[i], mla_w_kv_up[i], mla_g_q[i], mla_g_k[i])
        mqx, mkx, mvx = _mla_prep(px_mla, mla_g_qa[i], mla_g_kva[i], mla_w, rope_tabs, 512)
        mqc, mkc, mvc = _mla_prep(pc_mla, mla_g_qa[i], mla_g_kva[i], mla_w, None, 256)
        o_mla = _attention(mqx, mkx, mvx, mkc, mvc)

        o_hy = _hyena(px_hy, hy_conv_w[i], hy_conv_b[i][None, :], cs_x, *spec_x, hy_bias, i)

        wo = w_out[i].astype(BF16)
        wo_hy, wo_na, wo_mla = wo[:HY_DIM], wo[HY_DIM:HY_DIM + NA_DIM], wo[HY_DIM + NA_DIM:]
        g2 = g_norm2[i][None, :]
        w1, w2 = w_ff1[i].astype(BF16), w_ff2[i].astype(BF16)
        b1, b2 = b_ff1[i][None, :], b_ff2[i][None, :]
        x = _mix_mlp(o_hy, o_na, o_mla, x, mod_x, wo_hy, wo_na, wo_mla, g2, w1, b1, w2, b2, 512)

        if not last:
            oc_hy = _hyena(pc_hy, hy_conv_w[i], hy_conv_b[i][None, :], cs_c, *spec_c, hy_bias, i)
            oc_na = _attention(qc, kc, vtc)
            oc_mla = _attention(mqc, mkc, mvc)
            cx = _mix_mlp(oc_hy, oc_na, oc_mla, cx, mod_c, wo_hy, wo_na, wo_mla, g2, w1, b1, w2, b2, 256)
    return x
```

```python
import functools
import math

import jax
import jax.numpy as jnp
from jax import lax
from jax.experimental import pallas as pl
from jax.experimental.pallas import tpu as pltpu

F32 = jnp.float32
BF16 = jnp.bfloat16
HIGHEST = lax.Precision.HIGHEST

EPS = 1e-6
NEG_INF = -1e9
GRID_W = 64

HY_DIM = 256
HY_BANDS = 8
HY_DECAY_TARGET = 1e-2
HY_FAST = 0.3
HY_SLOW = 1.5

NA_HEADS = 4
NA_HEAD_DIM = 64
NA_DIM = NA_HEADS * NA_HEAD_DIM
NA_WIN_ROWS = 8
NA_WIN_COLS = 16

MLA_HEADS = 8
MLA_Q_RANK = 256
MLA_KV_RANK = 128
MLA_NOPE = 64
MLA_ROPE = 32
MLA_V = 64
MLA_QK = MLA_NOPE + MLA_ROPE
MLA_SLOT = 128
ROPE_BASE = 10000.0
ROPE_FREQS = MLA_ROPE // 4

VMEM_LIMIT_BYTES = 56 * 1024 * 1024
HYENA_VMEM_LIMIT_BYTES = 60 * 1024 * 1024

def _params(n_grid_dims):
    return pltpu.CompilerParams(dimension_semantics=("arbitrary",) * n_grid_dims,
                                vmem_limit_bytes=VMEM_LIMIT_BYTES)


def _tile(total, preferred):
    t = min(total, preferred)
    while total % t:
        t //= 2
    return t


def _resident(shape, index_map):
    return pl.BlockSpec(shape, index_map, pipeline_mode=pl.Buffered(1))


def _dot(a, b):
    return jnp.dot(a, b, preferred_element_type=F32)


def _split_bf16(x):
    hi = x.astype(BF16)
    return hi, (x - hi.astype(F32)).astype(BF16)


def _dot_3x(a, b):
    a_hi, a_lo = _split_bf16(a)
    b_hi, b_lo = _split_bf16(b)
    return _dot(a_hi, b_hi) + (_dot(a_hi, b_lo) + _dot(a_lo, b_hi))


def _dot_hi(a, b):
    return jnp.dot(a, b, preferred_element_type=F32, precision=HIGHEST)


def _dot_nt(a, b):
    return lax.dot_general(a, b, (((1,), (1,)), ((), ())), preferred_element_type=F32)


def _mod_kernel(cond_ref, w_ref, b_ref, o_ref):
    a = cond_ref[...]
    a = a / (1.0 + jnp.exp(-a))
    n = a.shape[0]
    a_hi, a_lo = _split_bf16(a)
    w_hi, w_lo = _split_bf16(w_ref[0])
    y = _dot(jnp.concatenate([a_hi, a_lo], axis=0), w_hi)
    o_ref[0] = y[:n] + y[n:] + _dot(a_hi, w_lo) + b_ref[0]


def _modulation(cond, w_mod, b_mod):
    depth, d, d6 = w_mod.shape
    n = cond.shape[0]
    tn = _tile(d6, 1536)
    return pl.pallas_call(
        _mod_kernel,
        grid=(depth, d6 // tn),
        in_specs=[pl.BlockSpec((n, d), lambda l, j: (0, 0)),
                  pl.BlockSpec((1, d, tn), lambda l, j: (l, 0, j)),
                  pl.BlockSpec((1, 1, tn), lambda l, j: (l, 0, j))],
        out_specs=pl.BlockSpec((1, n, tn), lambda l, j: (l, 0, j)),
        out_shape=jax.ShapeDtypeStruct((depth, n, d6), F32),
        compiler_params=_params(2),
    )(cond, w_mod, b_mod.reshape(depth, 1, d6))


def _adaln(x, g, shift, scale):
    y = x * lax.rsqrt(jnp.mean(x * x, axis=-1, keepdims=True) + EPS) * g
    return y * (1.0 + scale) + shift


def _inproj_kernel(x_ref, sh_ref, sc_ref, g_ref, w_hy_ref, w_na_ref, w_m_ref, o_hy_ref, o_na_ref, o_m_ref):
    h = _adaln(x_ref[0], g_ref[...], sh_ref[0, 0], sc_ref[0, 0]).astype(BF16)
    o_hy_ref[0] = _dot(h, w_hy_ref[0]).astype(BF16)
    o_na_ref[0] = _dot(h, w_na_ref[0])
    o_m_ref[0] = _dot(h, w_m_ref[0])


def _inproj(x, mod, g, w_in, w_m, layer, tm):
    b, t, d = x.shape
    tm = _tile(t, tm)
    n_hy, n_na, n_m = 3 * HY_DIM, 3 * NA_DIM, w_m.shape[2]
    assert n_hy == n_na
    widths = (n_hy, n_na, n_m)
    return pl.pallas_call(
        _inproj_kernel,
        grid=(b, t // tm),
        in_specs=[pl.BlockSpec((1, tm, d), lambda i, j: (i, j, 0)),
                  pl.BlockSpec((1, 1, 1, d), lambda i, j: (i, 0, 0, 0)),
                  pl.BlockSpec((1, 1, 1, d), lambda i, j: (i, 1, 0, 0)),
                  pl.BlockSpec((1, d), lambda i, j: (0, 0)),
                  _resident((1, d, n_hy), lambda i, j: (layer, 0, 0)),
                  _resident((1, d, n_na), lambda i, j: (layer, 0, 1)),
                  _resident((1, d, n_m), lambda i, j: (layer, 0, 0))],
        out_specs=[pl.BlockSpec((1, tm, n), lambda i, j: (i, j, 0)) for n in widths],
        out_shape=[jax.ShapeDtypeStruct((b, t, n), dt) for n, dt in zip(widths, (BF16, F32, F32))],
        compiler_params=_params(2),
    )(x, mod, mod, g, w_in, w_in, w_m)


def _dft_matrices(length):
    n = 2 * length
    blk = min(128, length)
    f = jnp.arange(length, dtype=jnp.int32)[:, None, None]
    t0 = (blk * jnp.arange(length // blk, dtype=jnp.int32))[None, :, None]
    dt = jnp.arange(blk, dtype=jnp.int32)[None, None, :]
    a = ((f * t0) % n).astype(F32) * (2.0 * math.pi / n)
    b = ((f * dt) % n).astype(F32) * (2.0 * math.pi / n)
    ca, sa, cb, sb = jnp.cos(a), jnp.sin(a), jnp.cos(b), jnp.sin(b)
    cos_m = (ca * cb - sa * sb).reshape(length, length)
    sin_m = (sa * cb + ca * sb).reshape(length, length)
    return cos_m.astype(BF16), sin_m.astype(BF16)


def _filt_kernel(z_ref, dec_ref, w1_ref, b1_ref, w2_ref, b2_ref, w3_ref, b3_ref, fr_ref, hsum_ref, hdiff_ref):
    z = z_ref[...]
    fr = fr_ref[0]
    a = jnp.sin(fr[0:1] * (_dot_3x(z, w1_ref[0]) + b1_ref[0]))
    a = jnp.sin(fr[1:2] * (_dot_3x(a, w2_ref[0]) + b2_ref[0]))
    h = _dot_3x(a, w3_ref[0]) + b3_ref[0]
    dec = dec_ref[...]
    c = HY_DIM
    row0 = lax.broadcasted_iota(jnp.int32, dec.shape, 0) == 0
    for o in range(2):
        hf = h[:, (2 * o) * c:(2 * o + 1) * c] * dec
        hb = h[:, (2 * o + 1) * c:(2 * o + 2) * c] * dec
        nrm = (jnp.sum(jnp.abs(hf), axis=0, keepdims=True)
               + jnp.sum(jnp.abs(hb), axis=0, keepdims=True) + EPS)
        hf = hf / nrm
        hb = jnp.where(row0, 0.0, hb / nrm)
        hsum_ref[0, :, o * c:(o + 1) * c] = hf + hb
        hdiff_ref[0, :, o * c:(o + 1) * c] = hf - hb


def _alternating_sign(shape):
    row = lax.broadcasted_iota(jnp.int32, shape, 0)
    return (1 - 2 * (row % 2)).astype(F32)


def _spec_kernel(hsum_ref, hdiff_ref, c_ref, s_ref, hc_ref, hs_ref, hn_ref):
    length = hsum_ref.shape[1]
    hsum = hsum_ref[0]
    hdiff = hdiff_ref[0]

    def split_dot(m, v):
        hi = v.astype(BF16)
        lo = (v - hi.astype(F32)).astype(BF16)
        return _dot(m, hi) + _dot(m, lo)

    row = lax.broadcasted_iota(jnp.int32, hsum.shape, 0)
    wf = jnp.where(row == 0, 0.5 / length, 1.0 / length)
    hc_ref[0] = split_dot(c_ref[...], hsum) * wf
    hs_ref[0] = split_dot(s_ref[...], hdiff) * wf
    hn_ref[0] = jnp.sum(_alternating_sign(hsum.shape) * hsum, axis=0, keepdims=True) * (0.5 / length)


def _hyena_spectra(length, cs, f_w1, f_b1, f_w2, f_b2, f_w3, f_b3, freq):
    depth = f_w1.shape[0]
    pos_dim = f_w1.shape[1]
    hid = f_w1.shape[2]
    kpad = 32
    t = jnp.linspace(0.0, 1.0, length, dtype=F32)[:, None]
    w = (2.0 * math.pi / length) * jnp.arange(length, dtype=F32)[:, None]
    bands = jnp.linspace(1e-4, HY_BANDS - 1, HY_BANDS, dtype=F32)
    z = jnp.concatenate([t, jnp.cos(bands * w), -jnp.sin(bands * w)], axis=-1)
    z = jnp.pad(z, ((0, 0), (0, kpad - pos_dim)))
    w1 = jnp.pad(f_w1, ((0, 0), (0, kpad - pos_dim), (0, 0)))
    deltas = jnp.linspace(math.log(HY_DECAY_TARGET) / HY_SLOW, math.log(HY_DECAY_TARGET) / HY_FAST,
                          HY_DIM, dtype=F32)
    decay = jnp.exp(-t * jnp.abs(deltas))
    c2 = 2 * HY_DIM
    c4 = 4 * HY_DIM
    lay3 = lambda l: (l, 0, 0)
    hsum, hdiff = pl.pallas_call(
        _filt_kernel,
        grid=(depth,),
        in_specs=[pl.BlockSpec((length, kpad), lambda l: (0, 0)),
                  pl.BlockSpec((length, HY_DIM), lambda l: (0, 0)),
                  pl.BlockSpec((1, kpad, hid), lay3),
                  pl.BlockSpec((1, 1, hid), lay3),
                  pl.BlockSpec((1, hid, hid), lay3),
                  pl.BlockSpec((1, 1, hid), lay3),
                  pl.BlockSpec((1, hid, c4), lay3),
                  pl.BlockSpec((1, 1, c4), lay3),
                  pl.BlockSpec((1, 2, hid), lay3)],
        out_specs=[pl.BlockSpec((1, length, c2), lay3),
                   pl.BlockSpec((1, length, c2), lay3)],
        out_shape=[jax.ShapeDtypeStruct((depth, length, c2), F32),
                   jax.ShapeDtypeStruct((depth, length, c2), F32)],
        compiler_params=_params(1),
    )(z, decay, w1, f_b1.reshape(depth, 1, hid), f_w2, f_b2.reshape(depth, 1, hid),
      f_w3, f_b3.reshape(depth, 1, c4), freq)
    blk_in = pl.BlockSpec((1, length, c2), lay3, pipeline_mode=pl.Buffered(1))
    blk_out = pl.BlockSpec((1, length, c2), lay3)
    return pl.pallas_call(
        _spec_kernel,
        grid=(depth,),
        in_specs=[blk_in, blk_in, _resident((length, length), lambda l: (0, 0)),
                  _resident((length, length), lambda l: (0, 0))],
        out_specs=[blk_out, blk_out, pl.BlockSpec((1, 1, c2), lay3)],
        out_shape=[jax.ShapeDtypeStruct((depth, length, c2), F32),
                   jax.ShapeDtypeStruct((depth, length, c2), F32),
                   jax.ShapeDtypeStruct((depth, 1, c2), F32)],
        compiler_params=pltpu.CompilerParams(dimension_semantics=("arbitrary",),
                                             vmem_limit_bytes=HYENA_VMEM_LIMIT_BYTES),
    )(hsum, hdiff, *cs)


HY_ROWS = 256
HY_HALO = 8


HY_PAIR = 2


def _hyena_kernel(pv_ref, pg_ref, wv_ref, bv_ref, wg_ref, bg_ref, c_ref, s_ref, hc_ref, hs_ref, hn_ref, bias_ref,
                  o_ref, z_scr, xc_scr, xs_scr, yc_scr, ys_scr):
    length = pv_ref.shape[1]
    c = HY_DIM
    rows = min(HY_ROWS, length)
    chunks = [(r0, rows) for r0 in range(0, length, rows)]
    halves = [(i, slice(i * c, (i + 1) * c)) for i in range(HY_PAIR)]

    def short_conv_rows(p_ref, w_ref, b_ref, i, r0):
        lo = max(r0 - HY_HALO, 0)
        hi = min(r0 + rows + HY_HALO, length)
        u = p_ref[i, lo:hi, :].astype(F32)
        row = lax.broadcasted_iota(jnp.int32, u.shape, 0) + lo
        prev = jnp.where(row == 0, 0.0, pltpu.roll(u, 1, 0))
        nxt = jnp.where(row == length - 1, 0.0, pltpu.roll(u, hi - lo - 1, 0))
        y = prev * w_ref[0:1, :] + u * w_ref[1:2, :] + nxt * w_ref[2:3, :] + b_ref[...]
        return y[r0 - lo:r0 - lo + rows]

    @pl.when(pl.program_id(1) == 0)
    def _():
        for r0, n in chunks:
            for i, lanes in halves:
                z_scr[r0:r0 + n, lanes] = short_conv_rows(pv_ref, wv_ref, bv_ref, i, r0)

    sign = _alternating_sign((rows, c))
    zb = z_scr[...].astype(BF16)
    xc_scr[...] = _dot(c_ref[...], zb)
    xs_scr[...] = _dot(s_ref[...], zb)
    xn = [jnp.zeros((1, c), F32) for _ in halves]
    for r0, n in chunks:
        hc, hs = hc_ref[0, r0:r0 + n, :], hs_ref[0, r0:r0 + n, :]
        for i, lanes in halves:
            xc, xs = xc_scr[r0:r0 + n, lanes], xs_scr[r0:r0 + n, lanes]
            yc_scr[r0:r0 + n, lanes] = (xc * hc - xs * hs).astype(BF16)
            ys_scr[r0:r0 + n, lanes] = (xc * hs + xs * hc).astype(BF16)
            xn[i] = xn[i] + jnp.sum(sign * z_scr[r0:r0 + n, lanes], axis=0, keepdims=True)
    xc_scr[...] = _dot(c_ref[...], yc_scr[...])
    xs_scr[...] = _dot(s_ref[...], ys_scr[...])
    bias = bias_ref[0]
    for r0, n in chunks:
        for i, lanes in halves:
            z = z_scr[r0:r0 + n, lanes]
            conv = xc_scr[r0:r0 + n, lanes] + xs_scr[r0:r0 + n, lanes] + sign * (xn[i] * hn_ref[0])
            z = short_conv_rows(pg_ref, wg_ref, bg_ref, i, r0) * (conv + z * bias)
            z_scr[r0:r0 + n, lanes] = z
            o_ref[i, r0:r0 + n, :] = z.astype(o_ref.dtype)


def _hyena(p_hy, conv_w, conv_b, cs, hc, hs, hn, bias, layer):
    b, length, _ = p_hy.shape
    assert b % HY_PAIR == 0
    c = HY_DIM
    wide = HY_PAIR * c
    return pl.pallas_call(
        _hyena_kernel,
        grid=(b // HY_PAIR, 2),
        in_specs=[pl.BlockSpec((HY_PAIR, length, c), lambda i, o: (i, 0, 0), pipeline_mode=pl.Buffered(1)),
                  pl.BlockSpec((HY_PAIR, length, c), lambda i, o: (i, 0, o + 1)),
                  pl.BlockSpec((3, c), lambda i, o: (0, 0)),
                  pl.BlockSpec((1, c), lambda i, o: (0, 0)),
                  pl.BlockSpec((3, c), lambda i, o: (0, o + 1)),
                  pl.BlockSpec((1, c), lambda i, o: (0, o + 1)),
                  _resident((length, length), lambda i, o: (0, 0)),
                  _resident((length, length), lambda i, o: (0, 0)),
                  pl.BlockSpec((1, length, c), lambda i, o: (layer, 0, o), pipeline_mode=pl.Buffered(1)),
                  pl.BlockSpec((1, length, c), lambda i, o: (layer, 0, o), pipeline_mode=pl.Buffered(1)),
                  pl.BlockSpec((1, 1, c), lambda i, o: (layer, 0, o)),
                  pl.BlockSpec((1, 1, c), lambda i, o: (2 * layer + o, 0, 0))],
        out_specs=pl.BlockSpec((HY_PAIR, length, c), lambda i, o: (i, 0, 0)),
        out_shape=jax.ShapeDtypeStruct((b, length, c), BF16),
        scratch_shapes=[pltpu.VMEM((length, wide), F32), pltpu.VMEM((length, wide), F32),
                        pltpu.VMEM((length, wide), F32), pltpu.VMEM((length, wide), BF16),
                        pltpu.VMEM((length, wide), BF16)],
        compiler_params=pltpu.CompilerParams(dimension_semantics=("arbitrary", "arbitrary"),
                                             vmem_limit_bytes=HYENA_VMEM_LIMIT_BYTES),
    )(p_hy, p_hy, conv_w, conv_b, conv_w, conv_b, *cs, hc, hs, hn, bias)


def _na_prep_kernel(q_ref, k_ref, v_ref, gq_ref, gk_ref, bd_ref, qo_ref, ko_ref, vo_ref):
    bd = bd_ref[...]

    def head_rms(x, g):
        sq = x * x
        hi = sq.astype(BF16)
        lo = (sq - hi.astype(F32)).astype(BF16)
        ms = (_dot(hi, bd) + _dot(lo, bd)) * (1.0 / NA_HEAD_DIM)
        return x * lax.rsqrt(ms + EPS) * g

    q = (head_rms(q_ref[0], gq_ref[...]) * (NA_HEAD_DIM ** -0.5 * LOG2E)).astype(BF16)
    k = head_rms(k_ref[0], gk_ref[...]).astype(BF16)
    for h in range(NA_HEADS):
        sl = slice(h * NA_HEAD_DIM, (h + 1) * NA_HEAD_DIM)
        qo_ref[0, h] = q[:, sl]
        ko_ref[0, h] = k[:, sl]
    vo_ref[0] = v_ref[0].T.astype(BF16)


def _na_prep(p_hn, g_q, g_k, tm):
    b, t, _ = p_hn.shape
    tm = _tile(t, tm)
    gq = jnp.tile(g_q, NA_HEADS)[None, :]
    gk = jnp.tile(g_k, NA_HEADS)[None, :]
    head = jnp.arange(NA_DIM) // NA_HEAD_DIM
    bd = (head[:, None] == head[None, :]).astype(BF16)
    out = jax.ShapeDtypeStruct((b, NA_HEADS, t, NA_HEAD_DIM), BF16)
    ospec = pl.BlockSpec((1, NA_HEADS, tm, NA_HEAD_DIM), lambda i, j: (i, 0, j, 0))
    out_t = jax.ShapeDtypeStruct((b, NA_DIM, t), BF16)
    ospec_t = pl.BlockSpec((1, NA_DIM, tm), lambda i, j: (i, 0, j))
    return pl.pallas_call(
        _na_prep_kernel,
        grid=(b, t // tm),
        in_specs=[pl.BlockSpec((1, tm, NA_DIM), lambda i, j: (i, j, 0)),
                  pl.BlockSpec((1, tm, NA_DIM), lambda i, j: (i, j, 1)),
                  pl.BlockSpec((1, tm, NA_DIM), lambda i, j: (i, j, 2)),
                  pl.BlockSpec((1, NA_DIM), lambda i, j: (0, 0)),
                  pl.BlockSpec((1, NA_DIM), lambda i, j: (0, 0)),
                  pl.BlockSpec((NA_DIM, NA_DIM), lambda i, j: (0, 0))],
        out_specs=[ospec, ospec, ospec_t],
        out_shape=[out, out, out_t],
        compiler_params=_params(2),
    )(p_hn, p_hn, p_hn, gq, gk, bd)


NA_QROWS = 8
NA_KROWS = NA_QROWS + NA_WIN_ROWS


def _rpb_expand_kernel(r_ref, oh_ref, m_ref, o_ref):
    o_ref[...] = (_dot_hi(r_ref[...], oh_ref[...]) + m_ref[...]) * LOG2E


def _na_first_key_row(rb, rows):
    return jnp.clip(NA_QROWS * rb - NA_WIN_ROWS // 2, 0, rows - NA_KROWS)


def _na_block_layouts(rows):
    nrb = rows // NA_QROWS

    def layout(rb):
        w0 = min(max(NA_QROWS * rb - NA_WIN_ROWS // 2, 0), rows - NA_KROWS)
        out = []
        for j in range(NA_QROWS):
            r = NA_QROWS * rb + j
            row0 = min(max(r - NA_WIN_ROWS // 2, 0), rows - NA_WIN_ROWS)
            assert w0 <= row0 and row0 + NA_WIN_ROWS <= w0 + NA_KROWS
            out.append((r - w0, row0 - w0))
        return tuple(out)

    assert all(layout(rb) == layout(1) for rb in range(1, nrb - 1))
    return [layout(0), layout(min(1, nrb - 1)), layout(nrb - 1)]


def _na_table_kernel(t_ref, o_ref, *, layouts):
    neg = jnp.full((GRID_W, GRID_W), NEG_INF, F32)
    for pos, layout in enumerate(layouts):
        for i in range(NA_KROWS):
            pieces = [t_ref[0, i - r_rel + NA_WIN_ROWS - 1] if row0_rel <= i < row0_rel + NA_WIN_ROWS else neg
                      for r_rel, row0_rel in layout]
            o_ref[0, pos, i * GRID_W:(i + 1) * GRID_W, :] = jnp.concatenate(pieces, axis=1)


def _na_bias_table(rpb, rows):
    depth, heads, n_dr, n_dc = rpb.shape
    kc = jnp.arange(GRID_W)[:, None]
    qc = jnp.arange(GRID_W)[None, :]
    dc = jnp.clip(kc - qc + NA_WIN_COLS - 1, 0, n_dc - 1)
    cstart = jnp.clip(qc - NA_WIN_COLS // 2, 0, GRID_W - NA_WIN_COLS)
    valid = ((kc >= cstart) & (kc < cstart + NA_WIN_COLS)).reshape(1, GRID_W * GRID_W)
    n_pad = 32
    onehot = ((dc.reshape(1, -1) == jnp.arange(n_pad)[:, None]) & valid).astype(F32)
    mask = jnp.where(valid, 0.0, NEG_INF).astype(F32)
    rows = depth * heads * n_dr
    rpb2 = jnp.pad(rpb.reshape(rows, n_dc).astype(F32), ((0, 0), (0, n_pad - n_dc)))
    t1 = pl.pallas_call(
        _rpb_expand_kernel,
        out_shape=jax.ShapeDtypeStruct((rows, GRID_W * GRID_W), F32),
        compiler_params=pltpu.CompilerParams(vmem_limit_bytes=VMEM_LIMIT_BYTES),
    )(rpb2, onehot, mask).reshape(depth * heads, n_dr, GRID_W, GRID_W)
    nk, nq = NA_KROWS * GRID_W, NA_QROWS * GRID_W
    tab = pl.pallas_call(
        functools.partial(_na_table_kernel, layouts=_na_block_layouts(rows)),
        grid=(depth * heads,),
        in_specs=[pl.BlockSpec((1, n_dr, GRID_W, GRID_W), lambda g: (g, 0, 0, 0))],
        out_specs=pl.BlockSpec((1, 3, nk, nq), lambda g: (g, 0, 0, 0)),
        out_shape=jax.ShapeDtypeStruct((depth * heads, 3, nk, nq), F32),
        compiler_params=_params(1),
    )(t1)
    return tab.reshape(depth, heads, 3, nk, nq)


ATTN_CHUNK_ELEMS = 64 * 1024
ATTN_ONES_ROWS = 16
LOG2E = math.log2(math.e)


def _attend_heads(heads, q_of, sets_of, s_scrs):
    row_max = [None] * heads
    outs = [None] * heads

    def score_phase(h):
        q = q_of(h)
        scr = s_scrs[h % 2]
        off, m = 0, None
        for n, k_fn, _, bias_fn in sets_of(h):
            s = _dot_nt(k_fn(), q)
            if bias_fn is not None:
                s = s + bias_fn()
            scr[off:off + n, :] = s
            mj = jnp.max(s, axis=0, keepdims=True)
            m = mj if m is None else jnp.maximum(m, mj)
            off += n
            yield
        row_max[h] = m

    def value_phase(h):
        scr = s_scrs[h % 2]
        m = row_max[h]
        base, acc = 0, None
        for n_set, _, vt_fn, _ in sets_of(h):
            for off, n in _key_chunks(n_set, ATTN_CHUNK_ELEMS // scr.shape[1]):
                p = jnp.exp2(scr[base + off:base + off + n, :] - m).astype(BF16)
                vt = vt_fn(off, n)
                lhs = jnp.concatenate([vt, jnp.ones((ATTN_ONES_ROWS, n), BF16)], axis=0)
                oj = _dot(lhs, p)
                acc = oj if acc is None else acc + oj
                yield
            base += n_set
        dv = acc.shape[0] - ATTN_ONES_ROWS
        outs[h] = acc[:dv] / acc[dv:dv + 1]

    for _ in score_phase(0):
        pass
    for h in range(heads):
        nxt = score_phase(h + 1) if h + 1 < heads else iter(())
        cur = value_phase(h)
        done_n = done_c = False
        while not (done_n and done_c):
            if not done_n:
                done_n = next(nxt, "end") == "end"
            if not done_c:
                done_c = next(cur, "end") == "end"
    return jnp.concatenate(outs, axis=0)


def _key_chunks(total, chunk):
    sizes = [chunk] * (total // chunk)
    if total % chunk:
        sizes.append(total % chunk)
    offs = [sum(sizes[:i]) for i in range(len(sizes))]
    return list(zip(offs, sizes))


def _na_kernel(q_ref, k_ref, vt_ref, kc_ref, vct_ref, b_ref, o_ref, s0_scr, s1_scr, *, rows):
    start = pl.multiple_of(_na_first_key_row(pl.program_id(0), rows) * GRID_W, NA_WIN_ROWS // 2 * GRID_W)
    nk = NA_KROWS * GRID_W
    dh = NA_HEAD_DIM

    def sets_of(h):
        hs = slice(h * dh, (h + 1) * dh)
        window = (nk,
                  lambda: k_ref[0, h, pl.ds(start, nk), :],
                  lambda off, n: vt_ref[0, hs, pl.ds(pl.multiple_of(start + off, 128), n)],
                  lambda: b_ref[0, h, 0])
        context = (kc_ref.shape[2],
                   lambda: kc_ref[0, h],
                   lambda off, n: vct_ref[0, hs, off:off + n],
                   None)
        return [window, context]

    o_t = _attend_heads(NA_HEADS, lambda h: q_ref[0, h], sets_of, (s0_scr, s1_scr))
    o_ref[0] = o_t.T.astype(o_ref.dtype)


def _na_latent(q, k, vt, kc, vct, bias_tab, layer):
    b, heads, s, dh = q.shape
    ctx = kc.shape[2]
    rows = s // GRID_W
    assert rows % NA_QROWS == 0 and rows >= NA_KROWS and (rows - NA_KROWS) % (NA_WIN_ROWS // 2) == 0
    nrb = rows // NA_QROWS
    tq = NA_QROWS * GRID_W

    def bias_index(rb, i):
        return (layer, 0, jnp.where(rb == 0, 0, jnp.where(rb == nrb - 1, 2, 1)), 0, 0)

    full4 = lambda rb, i: (i, 0, 0, 0)
    full3 = lambda rb, i: (i, 0, 0)
    return pl.pallas_call(
        functools.partial(_na_kernel, rows=rows),
        grid=(nrb, b),
        in_specs=[pl.BlockSpec((1, heads, tq, dh), lambda rb, i: (i, 0, rb, 0)),
                  pl.BlockSpec((1, heads, s, dh), full4),
                  pl.BlockSpec((1, heads * dh, s), full3),
                  pl.BlockSpec((1, heads, ctx, dh), full4),
                  pl.BlockSpec((1, heads * dh, ctx), full3),
                  pl.BlockSpec((1, heads, 1, NA_KROWS * GRID_W, tq), bias_index)],
        out_specs=pl.BlockSpec((1, tq, heads * dh), lambda rb, i: (i, rb, 0)),
        out_shape=jax.ShapeDtypeStruct((b, s, heads * dh), BF16),
        scratch_shapes=[pltpu.VMEM((NA_KROWS * GRID_W + ctx, tq), F32)] * 2,
        compiler_params=_params(2),
    )(q, k, vt, kc, vct, bias_tab)


def _attn_kernel(*refs, heads, dv, two_sets):
    if two_sets:
        q_ref, k1_ref, vt1_ref, k2_ref, vt2_ref, o_ref, s0_scr, s1_scr = refs
        key_sets = ((k1_ref, vt1_ref), (k2_ref, vt2_ref))
    else:
        q_ref, k1_ref, vt1_ref, o_ref, s0_scr, s1_scr = refs
        key_sets = ((k1_ref, vt1_ref),)

    def sets_of(h):
        hs = slice(h * dv, (h + 1) * dv)
        return [(k_ref.shape[2],
                 lambda k_ref=k_ref: k_ref[0, h],
                 lambda off, n, vt_ref=vt_ref: vt_ref[0, hs, off:off + n],
                 None) for k_ref, vt_ref in key_sets]

    o_t = _attend_heads(heads, lambda h: q_ref[0, h], sets_of, (s0_scr, s1_scr))
    o_ref[0] = o_t.T.astype(o_ref.dtype)


def _attention(q, k1, vt1, k2=None, vt2=None, tq=512):
    b, heads, t, dq = q.shape
    dv = vt1.shape[1] // heads
    tq = _tile(t, tq)
    two_sets = k2 is not None
    n_keys = k1.shape[2] + (k2.shape[2] if two_sets else 0)
    full4 = lambda i, j: (i, 0, 0, 0)
    full3 = lambda i, j: (i, 0, 0)
    in_specs = [pl.BlockSpec((1, heads, tq, dq), lambda i, j: (i, 0, j, 0)),
                pl.BlockSpec((1,) + k1.shape[1:], full4),
                pl.BlockSpec((1,) + vt1.shape[1:], full3)]
    args = [q, k1, vt1]
    if two_sets:
        in_specs += [pl.BlockSpec((1,) + k2.shape[1:], full4),
                     pl.BlockSpec((1,) + vt2.shape[1:], full3)]
        args += [k2, vt2]
    return pl.pallas_call(
        functools.partial(_attn_kernel, heads=heads, dv=dv, two_sets=two_sets),
        grid=(b, t // tq),
        in_specs=in_specs,
        out_specs=pl.BlockSpec((1, tq, heads * dv), lambda i, j: (i, j, 0)),
        out_shape=jax.ShapeDtypeStruct((b, t, heads * dv), BF16),
        scratch_shapes=[pltpu.VMEM((n_keys, tq), F32)] * 2,
        compiler_params=_params(2),
    )(*args)


def _mla_prep_kernel(*refs, rope):
    if rope:
        (p_ref, gqa_ref, gkva_ref, wq_ref, wkn_ref, wv_ref, g_ref, wqs_ref, cos_ref, sin_ref,
         qo_ref, ko_ref, vto_ref) = refs
    else:
        p_ref, gqa_ref, gkva_ref, wq_ref, wkn_ref, wv_ref, g_ref, qo_ref, ko_ref, vto_ref = refs
    p = p_ref[0]
    a, b_ = MLA_Q_RANK, MLA_Q_RANK + MLA_KV_RANK
    cq, ckv = p[:, :a], p[:, a:b_]
    kr, krs = p[:, b_:b_ + MLA_SLOT], p[:, b_ + MLA_SLOT:b_ + 2 * MLA_SLOT]

    def rms(x, g):
        return x * lax.rsqrt(jnp.mean(x * x, axis=-1, keepdims=True) + EPS) * g

    cqn = rms(cq, gqa_ref[...]).astype(BF16)
    ckvn = rms(ckv, gkva_ref[...]).astype(BF16)
    qa = _dot(cqn, wq_ref[...])
    kn = _dot(ckvn, wkn_ref[...])
    g = g_ref[...]
    aq, ak = g[0:1], g[2:3]
    if rope:
        qs = _dot(cqn, wqs_ref[...])
        cos_t, sin_t = cos_ref[...], sin_ref[...]
        aq, bq = aq * cos_t, g[1:2] * sin_t
        ak, k_rot = ak * cos_t, krs * (g[3:4] * sin_t)

    def inv_rms(x):
        return lax.rsqrt(jnp.sum(x * x, axis=-1, keepdims=True) * (1.0 / MLA_QK) + EPS)

    for h in range(MLA_HEADS):
        sl = slice(h * MLA_SLOT, (h + 1) * MLA_SLOT)
        xq = qa[:, sl]
        yq = xq * aq
        if rope:
            yq = yq + qs[:, sl] * bq
        qo_ref[0, h] = (yq * (inv_rms(xq) * (MLA_QK ** -0.5 * LOG2E))).astype(BF16)
        xk = kn[:, sl] + kr
        yk = xk * ak
        if rope:
            yk = yk + k_rot
        ko_ref[0, h] = (yk * inv_rms(xk)).astype(BF16)
    vto_ref[0] = _dot_nt(wv_ref[...], ckvn).astype(BF16)


_ROPE_SWAP = tuple(list(range(8, 16)) + list(range(0, 8)) + list(range(24, 32)) + list(range(16, 24)))


def _slot(nope, rope_part):
    lead = (nope if nope is not None else rope_part).shape[:-1]
    dt = (nope if nope is not None else rope_part).dtype
    z = lambda n: jnp.zeros(lead + (n,), dt)
    return jnp.concatenate([nope if nope is not None else z(MLA_NOPE),
                            rope_part if rope_part is not None else z(MLA_ROPE),
                            z(MLA_SLOT - MLA_QK)], axis=-1)


def _rope_tables(s):
    t = jnp.arange(s)
    pos = jnp.stack([t // GRID_W, t % GRID_W], axis=-1).astype(F32)
    inv = ROPE_BASE ** (-jnp.arange(ROPE_FREQS, dtype=F32) / ROPE_FREQS)
    ang = pos[:, :, None] * inv
    cos, sin = jnp.cos(ang), jnp.sin(ang)
    cos_t = jnp.concatenate([cos[:, 0], cos[:, 0], cos[:, 1], cos[:, 1]], axis=-1)
    sin_t = jnp.concatenate([-sin[:, 0], sin[:, 0], -sin[:, 1], sin[:, 1]], axis=-1)
    return _slot(jnp.ones((s, MLA_NOPE), F32), cos_t), _slot(None, sin_t)


def _mla_weights(w_q_up, w_kv_up, g_q, g_k):
    swap = jnp.array(_ROPE_SWAP)
    wq = w_q_up.reshape(MLA_Q_RANK, MLA_HEADS, MLA_QK)
    wkv = w_kv_up.reshape(MLA_KV_RANK, MLA_HEADS, MLA_NOPE + MLA_V)
    flat = lambda w: w.reshape(w.shape[0], -1).astype(BF16)
    wq_slot = flat(_slot(wq[..., :MLA_NOPE], wq[..., MLA_NOPE:]))
    wqs_slot = flat(_slot(None, wq[..., MLA_NOPE:][..., swap]))
    wkn_slot = flat(_slot(wkv[..., :MLA_NOPE], None))
    wv = flat(wkv[..., MLA_NOPE:]).T
    gains = jnp.stack([_slot(g_q[:MLA_NOPE], g_q[MLA_NOPE:]), _slot(None, g_q[MLA_NOPE:][swap]),
                       _slot(g_k[:MLA_NOPE], g_k[MLA_NOPE:]), _slot(None, g_k[MLA_NOPE:][swap])])
    return wq_slot, wkn_slot, wv, gains, wqs_slot


def _mla_prep(p_mla, g_qa, g_kva, weights, rope_tabs, tm):
    b, t, n = p_mla.shape
    tm = _tile(t, tm)
    wq_slot, wkn_slot, wv, gains, wqs_slot = weights
    rope = rope_tabs is not None
    const = lambda i, j: (0, 0)
    in_specs = [pl.BlockSpec((1, tm, n), lambda i, j: (i, j, 0)),
                pl.BlockSpec((1, MLA_Q_RANK), const),
                pl.BlockSpec((1, MLA_KV_RANK), const),
                pl.BlockSpec(wq_slot.shape, const), pl.BlockSpec(wkn_slot.shape, const),
                pl.BlockSpec(wv.shape, const), pl.BlockSpec(gains.shape, const)]
    args = [p_mla, g_qa[None, :], g_kva[None, :], wq_slot, wkn_slot, wv, gains]
    if rope:
        in_specs += [pl.BlockSpec(wqs_slot.shape, const),
                     pl.BlockSpec((tm, MLA_SLOT), lambda i, j: (j, 0)),
                     pl.BlockSpec((tm, MLA_SLOT), lambda i, j: (j, 0))]
        args += [wqs_slot] + list(rope_tabs)
    qk = jax.ShapeDtypeStruct((b, MLA_HEADS, t, MLA_SLOT), BF16)
    vt = jax.ShapeDtypeStruct((b, MLA_HEADS * MLA_V, t), BF16)
    qk_spec = pl.BlockSpec((1, MLA_HEADS, tm, MLA_SLOT), lambda i, j: (i, 0, j, 0))
    vt_spec = pl.BlockSpec((1, MLA_HEADS * MLA_V, tm), lambda i, j: (i, 0, j))
    return pl.pallas_call(
        functools.partial(_mla_prep_kernel, rope=rope),
        grid=(b, t // tm),
        in_specs=in_specs,
        out_specs=[qk_spec, qk_spec, vt_spec],
        out_shape=[qk, qk, vt],
        compiler_params=_params(2),
    )(*args)


def _mix_mlp_kernel(hy_ref, na_ref, mla_ref, x_ref, gate1_ref, sh_ref, sc_ref, gate2_ref, g_ref,
                    w_hy_ref, w_na_ref, w_mla_ref, w1_ref, b1_ref, w2_ref, b2_ref, o_ref, *, chunk):
    mix = (_dot(hy_ref[0], w_hy_ref[0]) + _dot(na_ref[0], w_na_ref[0]) + _dot(mla_ref[0], w_mla_ref[0]))
    x = x_ref[0] + gate1_ref[0, 0] * mix
    h = _adaln(x, g_ref[...], sh_ref[0, 0], sc_ref[0, 0]).astype(BF16)
    d_ff = w1_ref.shape[2]
    acc = jnp.zeros(x.shape, F32)
    for c0 in range(0, d_ff, chunk):
        a = jnp.maximum(_dot(h, w1_ref[0, :, c0:c0 + chunk]) + b1_ref[:, c0:c0 + chunk], 0.0)
        acc = acc + _dot((a * a).astype(BF16), w2_ref[0, c0:c0 + chunk, :])
    o_ref[0] = x + gate2_ref[0, 0] * (acc + b2_ref[...])


def _mix_mlp(o_hy, o_na, o_mla, x, mod, w_out, g, w1, b1, w2, b2, layer, tm):
    b, t, d = x.shape
    d_ff = w1.shape[2]
    n_hy, n_na, n_mla = o_hy.shape[-1], o_na.shape[-1], o_mla.shape[-1]
    assert n_hy == n_na and n_mla == n_hy + n_na
    tm = _tile(t, tm)
    tok = lambda n: pl.BlockSpec((1, tm, n), lambda i, j: (i, j, 0))
    modspec = lambda k: pl.BlockSpec((1, 1, 1, d), lambda i, j: (i, k, 0, 0))
    const = lambda i, j: (0, 0)
    lay = lambda i, j: (layer, 0, 0)
    return pl.pallas_call(
        functools.partial(_mix_mlp_kernel, chunk=_tile(d_ff, 1024)),
        grid=(b, t // tm),
        in_specs=[tok(n_hy), tok(n_na), tok(n_mla), tok(d),
                  modspec(2), modspec(3), modspec(4), modspec(5),
                  pl.BlockSpec((1, d), const),
                  _resident((1, n_hy, d), lay),
                  _resident((1, n_na, d), lambda i, j: (layer, 1, 0)),
                  _resident((1, n_mla, d), lambda i, j: (layer, 1, 0)),
                  _resident((1, d, d_ff), lay), pl.BlockSpec((1, d_ff), const),
                  _resident((1, d_ff, d), lay), pl.BlockSpec((1, d), const)],
        out_specs=tok(d),
        out_shape=jax.ShapeDtypeStruct((b, t, d), F32),
        compiler_params=_params(2),
    )(o_hy, o_na, o_mla, x, mod, mod, mod, mod, g, w_out, w_out, w_out, w1, b1, w2, b2)


def kernel(x, c, ctx, c_ctx, w_mod, b_mod, g_norm1, w_in, hy_conv_w, hy_conv_b, hy_f_w1, hy_f_b1, hy_f_w2, hy_f_b2, hy_f_w3, hy_f_b3, hy_freq, hy_bias, na_g_q, na_g_k, na_rpb, mla_g_qa, mla_g_kva, mla_w_q_up, mla_w_kv_up, mla_g_q, mla_g_k, w_out, g_norm2, w_ff1, b_ff1, w_ff2, b_ff2):
    b, s, d = x.shape
    lc = ctx.shape[1]
    depth = w_mod.shape[0]
    in_hn = 3 * HY_DIM + 3 * NA_DIM
    swap = jnp.array(_ROPE_SWAP)

    n_cond = -(-(b + 1) // 8) * 8
    cond = jnp.zeros((n_cond, d), F32).at[:b].set(c).at[b].set(c_ctx)
    mods = _modulation(cond, w_mod, b_mod).reshape(depth, n_cond, 6, 1, d)

    rope_tabs = _rope_tables(s)
    cs_x = _dft_matrices(s)
    cs_c = _dft_matrices(lc)
    filt = (hy_f_w1, hy_f_b1, hy_f_w2, hy_f_b2, hy_f_w3, hy_f_b3, hy_freq)
    spec_x = _hyena_spectra(s, cs_x, *filt)
    spec_c = _hyena_spectra(lc, cs_c, *filt)
    bias_tab = _na_bias_table(na_rpb, s // GRID_W)
    hy_bias = hy_bias.reshape(2 * depth, 1, HY_DIM)

    w_in_b, w_out_b = w_in.astype(BF16), w_out.astype(BF16)
    w_ff1_b, w_ff2_b = w_ff1.astype(BF16), w_ff2.astype(BF16)
    w_kr = w_in[:, :, in_hn + MLA_Q_RANK + MLA_KV_RANK:]
    w_m_b = jnp.concatenate([w_in[:, :, in_hn:in_hn + MLA_Q_RANK + MLA_KV_RANK], _slot(None, w_kr),
                             _slot(None, w_kr[:, :, swap])], axis=-1).astype(BF16)

    cx = ctx
    for i in range(depth):
        last = i == depth - 1
        mod_x = mods[i, :b]
        mod_c = jnp.broadcast_to(mods[i, b], (b, 6, 1, d))
        g1 = g_norm1[i][None, :]
        px_hy, px_na, px_mla = _inproj(x, mod_x, g1, w_in_b, w_m_b, i, 512)
        pc_hy, pc_na, pc_mla = _inproj(cx, mod_c, g1, w_in_b, w_m_b, i, 256)

        qx, kx, vtx = _na_prep(px_na, na_g_q[i], na_g_k[i], 512)
        qc, kc, vtc = _na_prep(pc_na, na_g_q[i], na_g_k[i], 256)
        o_na = _na_latent(qx, kx, vtx, kc, vtc, bias_tab, i)

        mla_w = _mla_weights(mla_w_q_up[i], mla_w_kv_up[i], mla_g_q[i], mla_g_k[i])
        mqx, mkx, mvx = _mla_prep(px_mla, mla_g_qa[i], mla_g_kva[i], mla_w, rope_tabs, 512)
        mqc, mkc, mvc = _mla_prep(pc_mla, mla_g_qa[i], mla_g_kva[i], mla_w, None, 256)
        o_mla = _attention(mqx, mkx, mvx, mkc, mvc)

        o_hy = _hyena(px_hy, hy_conv_w[i], hy_conv_b[i][None, :], cs_x, *spec_x, hy_bias, i)

        g2 = g_norm2[i][None, :]
        b1, b2 = b_ff1[i][None, :], b_ff2[i][None, :]
        x = _mix_mlp(o_hy, o_na, o_mla, x, mod_x, w_out_b, g2, w_ff1_b, b1, w_ff2_b, b2, i, 512)

        if not last:
            oc_hy = _hyena(pc_hy, hy_conv_w[i], hy_conv_b[i][None, :], cs_c, *spec_c, hy_bias, i)
            oc_na = _attention(qc, kc, vtc)
            oc_mla = _attention(mqc, mkc, mvc)
            cx = _mix_mlp(oc_hy, oc_na, oc_mla, cx, mod_c, w_out_b, g2, w_ff1_b, b1, w_ff2_b, b2, i, 256)
    return x
```

```python
import functools
import math

import jax
import jax.numpy as jnp
from jax import lax
from jax.experimental import pallas as pl
from jax.experimental.pallas import tpu as pltpu

F32 = jnp.float32
BF16 = jnp.bfloat16
HIGHEST = lax.Precision.HIGHEST

EPS = 1e-6
NEG_INF = -1e9
GRID_W = 64

HY_DIM = 256
HY_BANDS = 8
HY_DECAY_TARGET = 1e-2
HY_FAST = 0.3
HY_SLOW = 1.5

NA_HEADS = 4
NA_HEAD_DIM = 64
NA_DIM = NA_HEADS * NA_HEAD_DIM
NA_WIN_ROWS = 8
NA_WIN_COLS = 16

MLA_HEADS = 8
MLA_Q_RANK = 256
MLA_KV_RANK = 128
MLA_NOPE = 64
MLA_ROPE = 32
MLA_V = 64
MLA_QK = MLA_NOPE + MLA_ROPE
MLA_SLOT = 128
ROPE_BASE = 10000.0
ROPE_FREQS = MLA_ROPE // 4

VMEM_LIMIT_BYTES = 56 * 1024 * 1024
HYENA_VMEM_LIMIT_BYTES = 60 * 1024 * 1024

def _params(n_grid_dims):
    return pltpu.CompilerParams(dimension_semantics=("arbitrary",) * n_grid_dims,
                                vmem_limit_bytes=VMEM_LIMIT_BYTES)


def _tile(total, preferred):
    t = min(total, preferred)
    while total % t:
        t //= 2
    return t


def _resident(shape, index_map):
    return pl.BlockSpec(shape, index_map, pipeline_mode=pl.Buffered(1))


def _dot(a, b):
    return jnp.dot(a, b, preferred_element_type=F32)


def _split_bf16(x):
    hi = x.astype(BF16)
    return hi, (x - hi.astype(F32)).astype(BF16)


def _dot_3x(a, b):
    a_hi, a_lo = _split_bf16(a)
    b_hi, b_lo = _split_bf16(b)
    return _dot(a_hi, b_hi) + (_dot(a_hi, b_lo) + _dot(a_lo, b_hi))


def _dot_hi(a, b):
    return jnp.dot(a, b, preferred_element_type=F32, precision=HIGHEST)


def _dot_nt(a, b):
    return lax.dot_general(a, b, (((1,), (1,)), ((), ())), preferred_element_type=F32)


def _mod_kernel(cond_ref, w_ref, b_ref, o_ref):
    a = cond_ref[...]
    a = a / (1.0 + jnp.exp(-a))
    n = a.shape[0]
    a_hi, a_lo = _split_bf16(a)
    w_hi, w_lo = _split_bf16(w_ref[0])
    y = _dot(jnp.concatenate([a_hi, a_lo], axis=0), w_hi)
    o_ref[0] = y[:n] + y[n:] + _dot(a_hi, w_lo) + b_ref[0]


def _modulation(cond, w_mod, b_mod):
    depth, d, d6 = w_mod.shape
    n = cond.shape[0]
    tn = _tile(d6, 1536)
    return pl.pallas_call(
        _mod_kernel,
        grid=(depth, d6 // tn),
        in_specs=[pl.BlockSpec((n, d), lambda l, j: (0, 0)),
                  pl.BlockSpec((1, d, tn), lambda l, j: (l, 0, j)),
                  pl.BlockSpec((1, 1, tn), lambda l, j: (l, 0, j))],
        out_specs=pl.BlockSpec((1, n, tn), lambda l, j: (l, 0, j)),
        out_shape=jax.ShapeDtypeStruct((depth, n, d6), F32),
        compiler_params=_params(2),
    )(cond, w_mod, b_mod.reshape(depth, 1, d6))


def _adaln(x, g, shift, scale):
    y = x * lax.rsqrt(jnp.mean(x * x, axis=-1, keepdims=True) + EPS) * g
    return y * (1.0 + scale) + shift


def _dft_matrices(length):
    n = 2 * length
    blk = min(128, length)
    f = jnp.arange(length, dtype=jnp.int32)[:, None, None]
    t0 = (blk * jnp.arange(length // blk, dtype=jnp.int32))[None, :, None]
    dt = jnp.arange(blk, dtype=jnp.int32)[None, None, :]
    a = ((f * t0) % n).astype(F32) * (2.0 * math.pi / n)
    b = ((f * dt) % n).astype(F32) * (2.0 * math.pi / n)
    ca, sa, cb, sb = jnp.cos(a), jnp.sin(a), jnp.cos(b), jnp.sin(b)
    cos_m = (ca * cb - sa * sb).reshape(length, length)
    sin_m = (sa * cb + ca * sb).reshape(length, length)
    return cos_m.astype(BF16), sin_m.astype(BF16)


def _filt_kernel(z_ref, dec_ref, w1_ref, b1_ref, w2_ref, b2_ref, w3_ref, b3_ref, fr_ref, hsum_ref, hdiff_ref):
    z = z_ref[...]
    fr = fr_ref[0]
    a = jnp.sin(fr[0:1] * (_dot_3x(z, w1_ref[0]) + b1_ref[0]))
    a = jnp.sin(fr[1:2] * (_dot_3x(a, w2_ref[0]) + b2_ref[0]))
    h = _dot_3x(a, w3_ref[0]) + b3_ref[0]
    dec = dec_ref[...]
    c = HY_DIM
    row0 = lax.broadcasted_iota(jnp.int32, dec.shape, 0) == 0
    for o in range(2):
        hf = h[:, (2 * o) * c:(2 * o + 1) * c] * dec
        hb = h[:, (2 * o + 1) * c:(2 * o + 2) * c] * dec
        nrm = (jnp.sum(jnp.abs(hf), axis=0, keepdims=True)
               + jnp.sum(jnp.abs(hb), axis=0, keepdims=True) + EPS)
        hf = hf / nrm
        hb = jnp.where(row0, 0.0, hb / nrm)
        hsum_ref[0, :, o * c:(o + 1) * c] = hf + hb
        hdiff_ref[0, :, o * c:(o + 1) * c] = hf - hb


def _alternating_sign(shape):
    row = lax.broadcasted_iota(jnp.int32, shape, 0)
    return (1 - 2 * (row % 2)).astype(F32)


def _spec_kernel(hsum_ref, hdiff_ref, c_ref, s_ref, hc_ref, hs_ref, hn_ref):
    length = hsum_ref.shape[1]
    hsum = hsum_ref[0]
    hdiff = hdiff_ref[0]

    row = lax.broadcasted_iota(jnp.int32, hsum.shape, 0)
    wf = jnp.where(row == 0, 0.5 / length, 1.0 / length)
    hc_ref[0] = _dot(c_ref[...], hsum.astype(BF16)) * wf
    hs_ref[0] = _dot(s_ref[...], hdiff.astype(BF16)) * wf
    hn_ref[0] = jnp.sum(_alternating_sign(hsum.shape) * hsum, axis=0, keepdims=True) * (0.5 / length)


def _hyena_spectra(length, cs, f_w1, f_b1, f_w2, f_b2, f_w3, f_b3, freq):
    depth = f_w1.shape[0]
    pos_dim = f_w1.shape[1]
    hid = f_w1.shape[2]
    kpad = 32
    t = jnp.linspace(0.0, 1.0, length, dtype=F32)[:, None]
    w = (2.0 * math.pi / length) * jnp.arange(length, dtype=F32)[:, None]
    bands = jnp.linspace(1e-4, HY_BANDS - 1, HY_BANDS, dtype=F32)
    z = jnp.concatenate([t, jnp.cos(bands * w), -jnp.sin(bands * w)], axis=-1)
    z = jnp.pad(z, ((0, 0), (0, kpad - pos_dim)))
    w1 = jnp.pad(f_w1, ((0, 0), (0, kpad - pos_dim), (0, 0)))
    deltas = jnp.linspace(math.log(HY_DECAY_TARGET) / HY_SLOW, math.log(HY_DECAY_TARGET) / HY_FAST,
                          HY_DIM, dtype=F32)
    decay = jnp.exp(-t * jnp.abs(deltas))
    c2 = 2 * HY_DIM
    c4 = 4 * HY_DIM
    lay3 = lambda l: (l, 0, 0)
    hsum, hdiff = pl.pallas_call(
        _filt_kernel,
        grid=(depth,),
        in_specs=[pl.BlockSpec((length, kpad), lambda l: (0, 0)),
                  pl.BlockSpec((length, HY_DIM), lambda l: (0, 0)),
                  pl.BlockSpec((1, kpad, hid), lay3),
                  pl.BlockSpec((1, 1, hid), lay3),
                  pl.BlockSpec((1, hid, hid), lay3),
                  pl.BlockSpec((1, 1, hid), lay3),
                  pl.BlockSpec((1, hid, c4), lay3),
                  pl.BlockSpec((1, 1, c4), lay3),
                  pl.BlockSpec((1, 2, hid), lay3)],
        out_specs=[pl.BlockSpec((1, length, c2), lay3),
                   pl.BlockSpec((1, length, c2), lay3)],
        out_shape=[jax.ShapeDtypeStruct((depth, length, c2), F32),
                   jax.ShapeDtypeStruct((depth, length, c2), F32)],
        compiler_params=_params(1),
    )(z, decay, w1, f_b1.reshape(depth, 1, hid), f_w2, f_b2.reshape(depth, 1, hid),
      f_w3, f_b3.reshape(depth, 1, c4), freq)
    blk_in = pl.BlockSpec((1, length, c2), lay3, pipeline_mode=pl.Buffered(1))
    blk_out = pl.BlockSpec((1, length, c2), lay3)
    return pl.pallas_call(
        _spec_kernel,
        grid=(depth,),
        in_specs=[blk_in, blk_in, _resident((length, length), lambda l: (0, 0)),
                  _resident((length, length), lambda l: (0, 0))],
        out_specs=[blk_out, blk_out, pl.BlockSpec((1, 1, c2), lay3)],
        out_shape=[jax.ShapeDtypeStruct((depth, length, c2), F32),
                   jax.ShapeDtypeStruct((depth, length, c2), F32),
                   jax.ShapeDtypeStruct((depth, 1, c2), F32)],
        compiler_params=pltpu.CompilerParams(dimension_semantics=("arbitrary",),
                                             vmem_limit_bytes=HYENA_VMEM_LIMIT_BYTES),
    )(hsum, hdiff, *cs)


HY_ROWS = 256
HY_HALO = 8


HY_PAIR = 2


def _hyena_kernel(pv_ref, pg_ref, wv_ref, bv_ref, wg_ref, bg_ref, c_ref, s_ref, hc_ref, hs_ref, hn_ref, bias_ref,
                  o_ref, z_scr, xc_scr, xs_scr, yc_scr, ys_scr):
    length = pv_ref.shape[1]
    c = HY_DIM
    rows = min(HY_ROWS, length)
    chunks = [(r0, rows) for r0 in range(0, length, rows)]
    halves = [(i, slice(i * c, (i + 1) * c)) for i in range(HY_PAIR)]

    def short_conv_rows(p_ref, w_ref, b_ref, i, r0):
        lo = max(r0 - HY_HALO, 0)
        hi = min(r0 + rows + HY_HALO, length)
        u = p_ref[i, lo:hi, :].astype(F32)
        row = lax.broadcasted_iota(jnp.int32, u.shape, 0) + lo
        prev = jnp.where(row == 0, 0.0, pltpu.roll(u, 1, 0))
        nxt = jnp.where(row == length - 1, 0.0, pltpu.roll(u, hi - lo - 1, 0))
        y = prev * w_ref[0:1, :] + u * w_ref[1:2, :] + nxt * w_ref[2:3, :] + b_ref[...]
        return y[r0 - lo:r0 - lo + rows]

    @pl.when(pl.program_id(1) == 0)
    def _():
        for r0, n in chunks:
            for i, lanes in halves:
                z_scr[r0:r0 + n, lanes] = short_conv_rows(pv_ref, wv_ref, bv_ref, i, r0)

    sign = _alternating_sign((rows, c))
    zb = z_scr[...].astype(BF16)
    xc_scr[...] = _dot(c_ref[...], zb)
    xs_scr[...] = _dot(s_ref[...], zb)
    xn = [jnp.zeros((1, c), F32) for _ in halves]
    for r0, n in chunks:
        hc, hs = hc_ref[0, r0:r0 + n, :], hs_ref[0, r0:r0 + n, :]
        for i, lanes in halves:
            xc, xs = xc_scr[r0:r0 + n, lanes], xs_scr[r0:r0 + n, lanes]
            yc_scr[r0:r0 + n, lanes] = (xc * hc - xs * hs).astype(BF16)
            ys_scr[r0:r0 + n, lanes] = (xc * hs + xs * hc).astype(BF16)
            xn[i] = xn[i] + jnp.sum(sign * z_scr[r0:r0 + n, lanes], axis=0, keepdims=True)
    xc_scr[...] = _dot(c_ref[...], yc_scr[...])
    xs_scr[...] = _dot(s_ref[...], ys_scr[...])
    bias = bias_ref[0]
    for r0, n in chunks:
        for i, lanes in halves:
            z = z_scr[r0:r0 + n, lanes]
            conv = xc_scr[r0:r0 + n, lanes] + xs_scr[r0:r0 + n, lanes] + sign * (xn[i] * hn_ref[0])
            z = short_conv_rows(pg_ref, wg_ref, bg_ref, i, r0) * (conv + z * bias)
            z_scr[r0:r0 + n, lanes] = z
            o_ref[i, r0:r0 + n, :] = z.astype(o_ref.dtype)


def _hyena(p_hy, conv_w, conv_b, cs, hc, hs, hn, bias, layer):
    b, length, _ = p_hy.shape
    assert b % HY_PAIR == 0
    c = HY_DIM
    wide = HY_PAIR * c
    return pl.pallas_call(
        _hyena_kernel,
        grid=(b // HY_PAIR, 2),
        in_specs=[pl.BlockSpec((HY_PAIR, length, c), lambda i, o: (i, 0, 0), pipeline_mode=pl.Buffered(1)),
                  pl.BlockSpec((HY_PAIR, length, c), lambda i, o: (i, 0, o + 1)),
                  pl.BlockSpec((3, c), lambda i, o: (0, 0)),
                  pl.BlockSpec((1, c), lambda i, o: (0, 0)),
                  pl.BlockSpec((3, c), lambda i, o: (0, o + 1)),
                  pl.BlockSpec((1, c), lambda i, o: (0, o + 1)),
                  _resident((length, length), lambda i, o: (0, 0)),
                  _resident((length, length), lambda i, o: (0, 0)),
                  pl.BlockSpec((1, length, c), lambda i, o: (layer, 0, o), pipeline_mode=pl.Buffered(1)),
                  pl.BlockSpec((1, length, c), lambda i, o: (layer, 0, o), pipeline_mode=pl.Buffered(1)),
                  pl.BlockSpec((1, 1, c), lambda i, o: (layer, 0, o)),
                  pl.BlockSpec((1, 1, c), lambda i, o: (2 * layer + o, 0, 0))],
        out_specs=pl.BlockSpec((HY_PAIR, length, c), lambda i, o: (i, 0, 0)),
        out_shape=jax.ShapeDtypeStruct((b, length, c), BF16),
        scratch_shapes=[pltpu.VMEM((length, wide), F32), pltpu.VMEM((length, wide), F32),
                        pltpu.VMEM((length, wide), F32), pltpu.VMEM((length, wide), BF16),
                        pltpu.VMEM((length, wide), BF16)],
        compiler_params=pltpu.CompilerParams(dimension_semantics=("arbitrary", "arbitrary"),
                                             vmem_limit_bytes=HYENA_VMEM_LIMIT_BYTES),
    )(p_hy, p_hy, conv_w, conv_b, conv_w, conv_b, *cs, hc, hs, hn, bias)


def _na_prep_math(p, gq, gk, bd, qo_ref, ko_ref, vo_ref):
    def head_rms(x, g):
        sq = x * x
        hi = sq.astype(BF16)
        lo = (sq - hi.astype(F32)).astype(BF16)
        ms = (_dot(hi, bd) + _dot(lo, bd)) * (1.0 / NA_HEAD_DIM)
        return x * lax.rsqrt(ms + EPS) * g

    q = (head_rms(p[:, :NA_DIM], gq) * (NA_HEAD_DIM ** -0.5 * LOG2E)).astype(BF16)
    k = head_rms(p[:, NA_DIM:2 * NA_DIM], gk).astype(BF16)
    for h in range(NA_HEADS):
        sl = slice(h * NA_HEAD_DIM, (h + 1) * NA_HEAD_DIM)
        qo_ref[0, h] = q[:, sl]
        ko_ref[0, h] = k[:, sl]
    vo_ref[0] = p[:, 2 * NA_DIM:].T.astype(BF16)


NA_QROWS = 8
NA_KROWS = NA_QROWS + NA_WIN_ROWS


def _rpb_expand_kernel(r_ref, oh_ref, m_ref, o_ref):
    o_ref[...] = (_dot_hi(r_ref[...], oh_ref[...]) + m_ref[...]) * LOG2E


def _na_first_key_row(rb, rows):
    return jnp.clip(NA_QROWS * rb - NA_WIN_ROWS // 2, 0, rows - NA_KROWS)


def _na_block_layouts(rows):
    nrb = rows // NA_QROWS

    def layout(rb):
        w0 = min(max(NA_QROWS * rb - NA_WIN_ROWS // 2, 0), rows - NA_KROWS)
        out = []
        for j in range(NA_QROWS):
            r = NA_QROWS * rb + j
            row0 = min(max(r - NA_WIN_ROWS // 2, 0), rows - NA_WIN_ROWS)
            assert w0 <= row0 and row0 + NA_WIN_ROWS <= w0 + NA_KROWS
            out.append((r - w0, row0 - w0))
        return tuple(out)

    assert all(layout(rb) == layout(1) for rb in range(1, nrb - 1))
    return [layout(0), layout(min(1, nrb - 1)), layout(nrb - 1)]


def _na_table_kernel(t_ref, o_ref, *, layouts):
    neg = jnp.full((GRID_W, GRID_W), NEG_INF, F32)
    for pos, layout in enumerate(layouts):
        for i in range(NA_KROWS):
            pieces = [t_ref[0, i - r_rel + NA_WIN_ROWS - 1] if row0_rel <= i < row0_rel + NA_WIN_ROWS else neg
                      for r_rel, row0_rel in layout]
            o_ref[0, pos, i * GRID_W:(i + 1) * GRID_W, :] = jnp.concatenate(pieces, axis=1)


def _na_bias_table(rpb, rows):
    depth, heads, n_dr, n_dc = rpb.shape
    kc = jnp.arange(GRID_W)[:, None]
    qc = jnp.arange(GRID_W)[None, :]
    dc = jnp.clip(kc - qc + NA_WIN_COLS - 1, 0, n_dc - 1)
    cstart = jnp.clip(qc - NA_WIN_COLS // 2, 0, GRID_W - NA_WIN_COLS)
    valid = ((kc >= cstart) & (kc < cstart + NA_WIN_COLS)).reshape(1, GRID_W * GRID_W)
    n_pad = 32
    onehot = ((dc.reshape(1, -1) == jnp.arange(n_pad)[:, None]) & valid).astype(F32)
    mask = jnp.where(valid, 0.0, NEG_INF).astype(F32)
    rows = depth * heads * n_dr
    rpb2 = jnp.pad(rpb.reshape(rows, n_dc).astype(F32), ((0, 0), (0, n_pad - n_dc)))
    t1 = pl.pallas_call(
        _rpb_expand_kernel,
        out_shape=jax.ShapeDtypeStruct((rows, GRID_W * GRID_W), F32),
        compiler_params=pltpu.CompilerParams(vmem_limit_bytes=VMEM_LIMIT_BYTES),
    )(rpb2, onehot, mask).reshape(depth * heads, n_dr, GRID_W, GRID_W)
    nk, nq = NA_KROWS * GRID_W, NA_QROWS * GRID_W
    tab = pl.pallas_call(
        functools.partial(_na_table_kernel, layouts=_na_block_layouts(rows)),
        grid=(depth * heads,),
        in_specs=[pl.BlockSpec((1, n_dr, GRID_W, GRID_W), lambda g: (g, 0, 0, 0))],
        out_specs=pl.BlockSpec((1, 3, nk, nq), lambda g: (g, 0, 0, 0)),
        out_shape=jax.ShapeDtypeStruct((depth * heads, 3, nk, nq), F32),
        compiler_params=_params(1),
    )(t1)
    return tab.reshape(depth, heads, 3, nk, nq)


ATTN_CHUNK_ELEMS = 64 * 1024
ATTN_ONES_ROWS = 16
LOG2E = math.log2(math.e)


def _attend_heads(heads, q_of, sets_of, s_scrs):
    row_max = [None] * heads
    outs = [None] * heads

    def score_phase(h):
        q = q_of(h)
        scr = s_scrs[h % 2]
        off, m = 0, None
        for n, k_fn, _, bias_fn in sets_of(h):
            s = _dot_nt(k_fn(), q)
            if bias_fn is not None:
                s = s + bias_fn()
            scr[off:off + n, :] = s
            mj = jnp.max(s, axis=0, keepdims=True)
            m = mj if m is None else jnp.maximum(m, mj)
            off += n
            yield
        row_max[h] = m

    def value_phase(h):
        scr = s_scrs[h % 2]
        m = row_max[h]
        base, acc = 0, None
        for n_set, _, vt_fn, _ in sets_of(h):
            for off, n in _key_chunks(n_set, ATTN_CHUNK_ELEMS // scr.shape[1]):
                p = jnp.exp2(scr[base + off:base + off + n, :] - m).astype(BF16)
                vt = vt_fn(off, n)
                lhs = jnp.concatenate([vt, jnp.ones((ATTN_ONES_ROWS, n), BF16)], axis=0)
                oj = _dot(lhs, p)
                acc = oj if acc is None else acc + oj
                yield
            base += n_set
        dv = acc.shape[0] - ATTN_ONES_ROWS
        outs[h] = acc[:dv] / acc[dv:dv + 1]

    for _ in score_phase(0):
        pass
    for h in range(heads):
        nxt = score_phase(h + 1) if h + 1 < heads else iter(())
        cur = value_phase(h)
        done_n = done_c = False
        while not (done_n and done_c):
            if not done_n:
                done_n = next(nxt, "end") == "end"
            if not done_c:
                done_c = next(cur, "end") == "end"
    return jnp.concatenate(outs, axis=0)


def _key_chunks(total, chunk):
    sizes = [chunk] * (total // chunk)
    if total % chunk:
        sizes.append(total % chunk)
    offs = [sum(sizes[:i]) for i in range(len(sizes))]
    return list(zip(offs, sizes))


def _na_kernel(q_ref, k_ref, vt_ref, kc_ref, vct_ref, b_ref, o_ref, s0_scr, s1_scr, *, rows):
    start = pl.multiple_of(_na_first_key_row(pl.program_id(0), rows) * GRID_W, NA_WIN_ROWS // 2 * GRID_W)
    nk = NA_KROWS * GRID_W
    dh = NA_HEAD_DIM

    def sets_of(h):
        hs = slice(h * dh, (h + 1) * dh)
        window = (nk,
                  lambda: k_ref[0, h, pl.ds(start, nk), :],
                  lambda off, n: vt_ref[0, hs, pl.ds(pl.multiple_of(start + off, 128), n)],
                  lambda: b_ref[0, h, 0])
        context = (kc_ref.shape[2],
                   lambda: kc_ref[0, h],
                   lambda off, n: vct_ref[0, hs, off:off + n],
                   None)
        return [window, context]

    o_t = _attend_heads(NA_HEADS, lambda h: q_ref[0, h], sets_of, (s0_scr, s1_scr))
    o_ref[0] = o_t.T.astype(o_ref.dtype)


def _na_latent(q, k, vt, kc, vct, bias_tab, layer):
    b, heads, s, dh = q.shape
    ctx = kc.shape[2]
    rows = s // GRID_W
    assert rows % NA_QROWS == 0 and rows >= NA_KROWS and (rows - NA_KROWS) % (NA_WIN_ROWS // 2) == 0
    nrb = rows // NA_QROWS
    tq = NA_QROWS * GRID_W

    def bias_index(rb, i):
        return (layer, 0, jnp.where(rb == 0, 0, jnp.where(rb == nrb - 1, 2, 1)), 0, 0)

    full4 = lambda rb, i: (i, 0, 0, 0)
    full3 = lambda rb, i: (i, 0, 0)
    return pl.pallas_call(
        functools.partial(_na_kernel, rows=rows),
        grid=(nrb, b),
        in_specs=[pl.BlockSpec((1, heads, tq, dh), lambda rb, i: (i, 0, rb, 0)),
                  pl.BlockSpec((1, heads, s, dh), full4),
                  pl.BlockSpec((1, heads * dh, s), full3),
                  pl.BlockSpec((1, heads, ctx, dh), full4),
                  pl.BlockSpec((1, heads * dh, ctx), full3),
                  pl.BlockSpec((1, heads, 1, NA_KROWS * GRID_W, tq), bias_index)],
        out_specs=pl.BlockSpec((1, tq, heads * dh), lambda rb, i: (i, rb, 0)),
        out_shape=jax.ShapeDtypeStruct((b, s, heads * dh), BF16),
        scratch_shapes=[pltpu.VMEM((NA_KROWS * GRID_W + ctx, tq), F32)] * 2,
        compiler_params=_params(2),
    )(q, k, vt, kc, vct, bias_tab)


def _attn_kernel(*refs, heads, dv, two_sets):
    if two_sets:
        q_ref, k1_ref, vt1_ref, k2_ref, vt2_ref, o_ref, s0_scr, s1_scr = refs
        key_sets = ((k1_ref, vt1_ref), (k2_ref, vt2_ref))
    else:
        q_ref, k1_ref, vt1_ref, o_ref, s0_scr, s1_scr = refs
        key_sets = ((k1_ref, vt1_ref),)

    def sets_of(h):
        hs = slice(h * dv, (h + 1) * dv)
        return [(k_ref.shape[2],
                 lambda k_ref=k_ref: k_ref[0, h],
                 lambda off, n, vt_ref=vt_ref: vt_ref[0, hs, off:off + n],
                 None) for k_ref, vt_ref in key_sets]

    o_t = _attend_heads(heads, lambda h: q_ref[0, h], sets_of, (s0_scr, s1_scr))
    o_ref[0] = o_t.T.astype(o_ref.dtype)


def _attention(q, k1, vt1, k2=None, vt2=None, tq=512):
    b, heads, t, dq = q.shape
    dv = vt1.shape[1] // heads
    tq = _tile(t, tq)
    two_sets = k2 is not None
    n_keys = k1.shape[2] + (k2.shape[2] if two_sets else 0)
    full4 = lambda i, j: (i, 0, 0, 0)
    full3 = lambda i, j: (i, 0, 0)
    in_specs = [pl.BlockSpec((1, heads, tq, dq), lambda i, j: (i, 0, j, 0)),
                pl.BlockSpec((1,) + k1.shape[1:], full4),
                pl.BlockSpec((1,) + vt1.shape[1:], full3)]
    args = [q, k1, vt1]
    if two_sets:
        in_specs += [pl.BlockSpec((1,) + k2.shape[1:], full4),
                     pl.BlockSpec((1,) + vt2.shape[1:], full3)]
        args += [k2, vt2]
    return pl.pallas_call(
        functools.partial(_attn_kernel, heads=heads, dv=dv, two_sets=two_sets),
        grid=(b, t // tq),
        in_specs=in_specs,
        out_specs=pl.BlockSpec((1, tq, heads * dv), lambda i, j: (i, j, 0)),
        out_shape=jax.ShapeDtypeStruct((b, t, heads * dv), BF16),
        scratch_shapes=[pltpu.VMEM((n_keys, tq), F32)] * 2,
        compiler_params=_params(2),
    )(*args)


def _mla_prep_math(p, gqa, gkva, wq_ref, wkn_ref, wv_ref, g, rope_refs, qo_ref, ko_ref, vto_ref):
    rope = rope_refs is not None
    a, b_ = MLA_Q_RANK, MLA_Q_RANK + MLA_KV_RANK
    cq, ckv = p[:, :a], p[:, a:b_]
    kr, krs = p[:, b_:b_ + MLA_SLOT], p[:, b_ + MLA_SLOT:b_ + 2 * MLA_SLOT]

    def rms(x, g_):
        return x * lax.rsqrt(jnp.mean(x * x, axis=-1, keepdims=True) + EPS) * g_

    cqn = rms(cq, gqa).astype(BF16)
    ckvn = rms(ckv, gkva).astype(BF16)
    qa = _dot(cqn, wq_ref[...])
    kn = _dot(ckvn, wkn_ref[...])
    aq, ak = g[0:1], g[2:3]
    if rope:
        wqs_ref, cos_ref, sin_ref = rope_refs
        qs = _dot(cqn, wqs_ref[...])
        cos_t, sin_t = cos_ref[...], sin_ref[...]
        aq, bq = aq * cos_t, g[1:2] * sin_t
        ak, k_rot = ak * cos_t, krs * (g[3:4] * sin_t)

    def inv_rms(x):
        return lax.rsqrt(jnp.sum(x * x, axis=-1, keepdims=True) * (1.0 / MLA_QK) + EPS)

    for h in range(MLA_HEADS):
        sl = slice(h * MLA_SLOT, (h + 1) * MLA_SLOT)
        xq = qa[:, sl]
        yq = xq * aq
        if rope:
            yq = yq + qs[:, sl] * bq
        qo_ref[0, h] = (yq * (inv_rms(xq) * (MLA_QK ** -0.5 * LOG2E))).astype(BF16)
        xk = kn[:, sl] + kr
        yk = xk * ak
        if rope:
            yk = yk + k_rot
        ko_ref[0, h] = (yk * inv_rms(xk)).astype(BF16)
    vto_ref[0] = _dot_nt(wv_ref[...], ckvn).astype(BF16)


_ROPE_SWAP = tuple(list(range(8, 16)) + list(range(0, 8)) + list(range(24, 32)) + list(range(16, 24)))


def _slot(nope, rope_part):
    lead = (nope if nope is not None else rope_part).shape[:-1]
    dt = (nope if nope is not None else rope_part).dtype
    z = lambda n: jnp.zeros(lead + (n,), dt)
    return jnp.concatenate([nope if nope is not None else z(MLA_NOPE),
                            rope_part if rope_part is not None else z(MLA_ROPE),
                            z(MLA_SLOT - MLA_QK)], axis=-1)


def _rope_tables(s):
    t = jnp.arange(s)
    pos = jnp.stack([t // GRID_W, t % GRID_W], axis=-1).astype(F32)
    inv = ROPE_BASE ** (-jnp.arange(ROPE_FREQS, dtype=F32) / ROPE_FREQS)
    ang = pos[:, :, None] * inv
    cos, sin = jnp.cos(ang), jnp.sin(ang)
    cos_t = jnp.concatenate([cos[:, 0], cos[:, 0], cos[:, 1], cos[:, 1]], axis=-1)
    sin_t = jnp.concatenate([-sin[:, 0], sin[:, 0], -sin[:, 1], sin[:, 1]], axis=-1)
    return _slot(jnp.ones((s, MLA_NOPE), F32), cos_t), _slot(None, sin_t)


def _mla_weights(w_q_up, w_kv_up, g_q, g_k):
    swap = jnp.array(_ROPE_SWAP)
    wq = w_q_up.reshape(MLA_Q_RANK, MLA_HEADS, MLA_QK)
    wkv = w_kv_up.reshape(MLA_KV_RANK, MLA_HEADS, MLA_NOPE + MLA_V)
    flat = lambda w: w.reshape(w.shape[0], -1).astype(BF16)
    wq_slot = flat(_slot(wq[..., :MLA_NOPE], wq[..., MLA_NOPE:]))
    wqs_slot = flat(_slot(None, wq[..., MLA_NOPE:][..., swap]))
    wkn_slot = flat(_slot(wkv[..., :MLA_NOPE], None))
    wv = flat(wkv[..., MLA_NOPE:]).T
    gains = jnp.stack([_slot(g_q[:MLA_NOPE], g_q[MLA_NOPE:]), _slot(None, g_q[MLA_NOPE:][swap]),
                       _slot(g_k[:MLA_NOPE], g_k[MLA_NOPE:]), _slot(None, g_k[MLA_NOPE:][swap])])
    return wq_slot, wkn_slot, wv, gains, wqs_slot


def _project_kernel(*refs, rope):
    (x_ref, sh_ref, sc_ref, g_ref, w_hy_ref, w_na_ref, w_m_ref, gq_na_ref, gk_na_ref, bd_ref,
     gqa_ref, gkva_ref, wq_ref, wkn_ref, wv_ref, gains_ref) = refs[:16]
    rope_refs = refs[16:19] if rope else None
    o_hy_ref, qn_ref, kn_ref, vtn_ref, qm_ref, km_ref, vtm_ref = refs[-7:]
    h = _adaln(x_ref[0], g_ref[...], sh_ref[0, 0], sc_ref[0, 0]).astype(BF16)
    _mla_prep_math(_dot(h, w_m_ref[0]), gqa_ref[...], gkva_ref[...], wq_ref, wkn_ref, wv_ref, gains_ref[...],
                   rope_refs, qm_ref, km_ref, vtm_ref)
    _na_prep_math(_dot(h, w_na_ref[0]), gq_na_ref[...], gk_na_ref[...], bd_ref[...], qn_ref, kn_ref, vtn_ref)
    o_hy_ref[0] = _dot(h, w_hy_ref[0]).astype(BF16)


def _project(x, mod, g, w_in, w_m, layer, na_g_q, na_g_k, g_qa, g_kva, mla_w, rope_tabs, tm):
    b, t, d = x.shape
    tm = _tile(t, tm)
    n_hy, n_na, n_m = 3 * HY_DIM, 3 * NA_DIM, w_m.shape[2]
    assert n_hy == n_na
    wq_slot, wkn_slot, wv, gains, wqs_slot = mla_w
    rope = rope_tabs is not None
    gq = jnp.tile(na_g_q, NA_HEADS)[None, :]
    gk = jnp.tile(na_g_k, NA_HEADS)[None, :]
    head = jnp.arange(NA_DIM) // NA_HEAD_DIM
    bd = (head[:, None] == head[None, :]).astype(BF16)
    const = lambda i, j: (0, 0)
    in_specs = [pl.BlockSpec((1, tm, d), lambda i, j: (i, j, 0)),
                pl.BlockSpec((1, 1, 1, d), lambda i, j: (i, 0, 0, 0)),
                pl.BlockSpec((1, 1, 1, d), lambda i, j: (i, 1, 0, 0)),
                pl.BlockSpec((1, d), const),
                _resident((1, d, n_hy), lambda i, j: (layer, 0, 0)),
                _resident((1, d, n_na), lambda i, j: (layer, 0, 1)),
                _resident((1, d, n_m), lambda i, j: (layer, 0, 0)),
                pl.BlockSpec((1, NA_DIM), const), pl.BlockSpec((1, NA_DIM), const),
                pl.BlockSpec((NA_DIM, NA_DIM), const),
                pl.BlockSpec((1, MLA_Q_RANK), const), pl.BlockSpec((1, MLA_KV_RANK), const),
                pl.BlockSpec(wq_slot.shape, const), pl.BlockSpec(wkn_slot.shape, const),
                pl.BlockSpec(wv.shape, const), pl.BlockSpec(gains.shape, const)]
    args = [x, mod, mod, g, w_in, w_in, w_m, gq, gk, bd, g_qa[None, :], g_kva[None, :],
            wq_slot, wkn_slot, wv, gains]
    if rope:
        in_specs += [pl.BlockSpec(wqs_slot.shape, const),
                     pl.BlockSpec((tm, MLA_SLOT), lambda i, j: (j, 0)),
                     pl.BlockSpec((tm, MLA_SLOT), lambda i, j: (j, 0))]
        args += [wqs_slot] + list(rope_tabs)
    tok_major = lambda heads, width: (jax.ShapeDtypeStruct((b, heads, t, width), BF16),
                                      pl.BlockSpec((1, heads, tm, width), lambda i, j: (i, 0, j, 0)))
    transposed = lambda rows: (jax.ShapeDtypeStruct((b, rows, t), BF16),
                               pl.BlockSpec((1, rows, tm), lambda i, j: (i, 0, j)))
    outs = [(jax.ShapeDtypeStruct((b, t, n_hy), BF16), pl.BlockSpec((1, tm, n_hy), lambda i, j: (i, j, 0))),
            tok_major(NA_HEADS, NA_HEAD_DIM), tok_major(NA_HEADS, NA_HEAD_DIM), transposed(NA_DIM),
            tok_major(MLA_HEADS, MLA_SLOT), tok_major(MLA_HEADS, MLA_SLOT), transposed(MLA_HEADS * MLA_V)]
    return pl.pallas_call(
        functools.partial(_project_kernel, rope=rope),
        grid=(b, t // tm),
        in_specs=in_specs,
        out_specs=[spec for _, spec in outs],
        out_shape=[shape for shape, _ in outs],
        compiler_params=_params(2),
    )(*args)


def _mix_mlp_kernel(hy_ref, na_ref, mla_ref, x_ref, gate1_ref, sh_ref, sc_ref, gate2_ref, g_ref,
                    w_hy_ref, w_na_ref, w_mla_ref, w1_ref, b1_ref, w2_ref, b2_ref, o_ref, *, chunk):
    mix = (_dot(hy_ref[0], w_hy_ref[0]) + _dot(na_ref[0], w_na_ref[0]) + _dot(mla_ref[0], w_mla_ref[0]))
    x = x_ref[0] + gate1_ref[0, 0] * mix
    h = _adaln(x, g_ref[...], sh_ref[0, 0], sc_ref[0, 0]).astype(BF16)
    d_ff = w1_ref.shape[2]
    acc = jnp.zeros(x.shape, F32)
    for c0 in range(0, d_ff, chunk):
        a = jnp.maximum(_dot(h, w1_ref[0, :, c0:c0 + chunk]) + b1_ref[:, c0:c0 + chunk], 0.0)
        acc = acc + _dot((a * a).astype(BF16), w2_ref[0, c0:c0 + chunk, :])
    o_ref[0] = x + gate2_ref[0, 0] * (acc + b2_ref[...])


def _mix_mlp(o_hy, o_na, o_mla, x, mod, w_out, g, w1, b1, w2, b2, layer, tm):
    b, t, d = x.shape
    d_ff = w1.shape[2]
    n_hy, n_na, n_mla = o_hy.shape[-1], o_na.shape[-1], o_mla.shape[-1]
    assert n_hy == n_na and n_mla == n_hy + n_na
    tm = _tile(t, tm)
    tok = lambda n: pl.BlockSpec((1, tm, n), lambda i, j: (i, j, 0))
    modspec = lambda k: pl.BlockSpec((1, 1, 1, d), lambda i, j: (i, k, 0, 0))
    const = lambda i, j: (0, 0)
    lay = lambda i, j: (layer, 0, 0)
    return pl.pallas_call(
        functools.partial(_mix_mlp_kernel, chunk=_tile(d_ff, 1024)),
        grid=(b, t // tm),
        in_specs=[tok(n_hy), tok(n_na), tok(n_mla), tok(d),
                  modspec(2), modspec(3), modspec(4), modspec(5),
                  pl.BlockSpec((1, d), const),
                  _resident((1, n_hy, d), lay),
                  _resident((1, n_na, d), lambda i, j: (layer, 1, 0)),
                  _resident((1, n_mla, d), lambda i, j: (layer, 1, 0)),
                  _resident((1, d, d_ff), lay), pl.BlockSpec((1, d_ff), const),
                  _resident((1, d_ff, d), lay), pl.BlockSpec((1, d), const)],
        out_specs=tok(d),
        out_shape=jax.ShapeDtypeStruct((b, t, d), F32),
        compiler_params=_params(2),
    )(o_hy, o_na, o_mla, x, mod, mod, mod, mod, g, w_out, w_out, w_out, w1, b1, w2, b2)


def kernel(x, c, ctx, c_ctx, w_mod, b_mod, g_norm1, w_in, hy_conv_w, hy_conv_b, hy_f_w1, hy_f_b1, hy_f_w2, hy_f_b2, hy_f_w3, hy_f_b3, hy_freq, hy_bias, na_g_q, na_g_k, na_rpb, mla_g_qa, mla_g_kva, mla_w_q_up, mla_w_kv_up, mla_g_q, mla_g_k, w_out, g_norm2, w_ff1, b_ff1, w_ff2, b_ff2):
    b, s, d = x.shape
    lc = ctx.shape[1]
    depth = w_mod.shape[0]
    in_hn = 3 * HY_DIM + 3 * NA_DIM
    swap = jnp.array(_ROPE_SWAP)

    n_cond = -(-(b + 1) // 8) * 8
    cond = jnp.zeros((n_cond, d), F32).at[:b].set(c).at[b].set(c_ctx)
    mods = _modulation(cond, w_mod, b_mod).reshape(depth, n_cond, 6, 1, d)

    rope_tabs = _rope_tables(s)
    cs_x = _dft_matrices(s)
    cs_c = _dft_matrices(lc)
    filt = (hy_f_w1, hy_f_b1, hy_f_w2, hy_f_b2, hy_f_w3, hy_f_b3, hy_freq)
    spec_x = _hyena_spectra(s, cs_x, *filt)
    spec_c = _hyena_spectra(lc, cs_c, *filt)
    bias_tab = _na_bias_table(na_rpb, s // GRID_W)
    hy_bias = hy_bias.reshape(2 * depth, 1, HY_DIM)

    w_in_b, w_out_b = w_in.astype(BF16), w_out.astype(BF16)
    w_ff1_b, w_ff2_b = w_ff1.astype(BF16), w_ff2.astype(BF16)
    w_kr = w_in[:, :, in_hn + MLA_Q_RANK + MLA_KV_RANK:]
    w_m_b = jnp.concatenate([w_in[:, :, in_hn:in_hn + MLA_Q_RANK + MLA_KV_RANK], _slot(None, w_kr),
                             _slot(None, w_kr[:, :, swap])], axis=-1).astype(BF16)

    cx = ctx
    for i in range(depth):
        last = i == depth - 1
        mod_x = mods[i, :b]
        mod_c = jnp.broadcast_to(mods[i, b], (b, 6, 1, d))
        g1 = g_norm1[i][None, :]
        mla_w = _mla_weights(mla_w_q_up[i], mla_w_kv_up[i], mla_g_q[i], mla_g_k[i])
        prep = (i, na_g_q[i], na_g_k[i], mla_g_qa[i], mla_g_kva[i], mla_w)
        px_hy, qx, kx, vtx, mqx, mkx, mvx = _project(x, mod_x, g1, w_in_b, w_m_b, *prep, rope_tabs, 512)
        pc_hy, qc, kc, vtc, mqc, mkc, mvc = _project(cx, mod_c, g1, w_in_b, w_m_b, *prep, None, 256)

        o_na = _na_latent(qx, kx, vtx, kc, vtc, bias_tab, i)
        o_mla = _attention(mqx, mkx, mvx, mkc, mvc)

        o_hy = _hyena(px_hy, hy_conv_w[i], hy_conv_b[i][None, :], cs_x, *spec_x, hy_bias, i)

        g2 = g_norm2[i][None, :]
        b1, b2 = b_ff1[i][None, :], b_ff2[i][None, :]
        x = _mix_mlp(o_hy, o_na, o_mla, x, mod_x, w_out_b, g2, w_ff1_b, b1, w_ff2_b, b2, i, 512)

        if not last:
            oc_hy = _hyena(pc_hy, hy_conv_w[i], hy_conv_b[i][None, :], cs_c, *spec_c, hy_bias, i)
            oc_na = _attention(qc, kc, vtc)
            oc_mla = _attention(mqc, mkc, mvc)
            cx = _mix_mlp(oc_hy, oc_na, oc_mla, cx, mod_c, w_out_b, g2, w_ff1_b, b1, w_ff2_b, b2, i, 256)
    return x
```

```python
import functools
import math

import jax
import jax.numpy as jnp
from jax import lax
from jax.experimental import pallas as pl
from jax.experimental.pallas import tpu as pltpu

F32 = jnp.float32
BF16 = jnp.bfloat16
HIGHEST = lax.Precision.HIGHEST

EPS = 1e-6
NEG_INF = -1e9
GRID_W = 64

HY_DIM = 256
HY_BANDS = 8
HY_DECAY_TARGET = 1e-2
HY_FAST = 0.3
HY_SLOW = 1.5

NA_HEADS = 4
NA_HEAD_DIM = 64
NA_DIM = NA_HEADS * NA_HEAD_DIM
NA_WIN_ROWS = 8
NA_WIN_COLS = 16

MLA_HEADS = 8
MLA_Q_RANK = 256
MLA_KV_RANK = 128
MLA_NOPE = 64
MLA_ROPE = 32
MLA_V = 64
MLA_QK = MLA_NOPE + MLA_ROPE
MLA_SLOT = 128
ROPE_BASE = 10000.0
ROPE_FREQS = MLA_ROPE // 4

VMEM_LIMIT_BYTES = 56 * 1024 * 1024
HYENA_VMEM_LIMIT_BYTES = 60 * 1024 * 1024
CAST_BLOCK_BYTES = 8 * 1024 * 1024
PROJECT_TOKEN_TILE = 1024
MLP_TOKEN_TILE = 1024

def _params(n_grid_dims):
    return pltpu.CompilerParams(dimension_semantics=("arbitrary",) * n_grid_dims,
                                vmem_limit_bytes=VMEM_LIMIT_BYTES)


def _tile(total, preferred):
    t = min(total, preferred)
    while total % t:
        t //= 2
    return t


def _resident(shape, index_map):
    return pl.BlockSpec(shape, index_map, pipeline_mode=pl.Buffered(1))


def _dot(a, b):
    return jnp.dot(a, b, preferred_element_type=F32)


def _split_bf16(x):
    hi = x.astype(BF16)
    return hi, (x - hi.astype(F32)).astype(BF16)


def _dot_3x(a, b):
    a_hi, a_lo = _split_bf16(a)
    b_hi, b_lo = _split_bf16(b)
    return _dot(a_hi, b_hi) + (_dot(a_hi, b_lo) + _dot(a_lo, b_hi))


def _dot_hi(a, b):
    return jnp.dot(a, b, preferred_element_type=F32, precision=HIGHEST)


def _dot_nt(a, b):
    return lax.dot_general(a, b, (((1,), (1,)), ((), ())), preferred_element_type=F32)


def _cast_kernel(x_ref, o_ref):
    o_ref[...] = x_ref[...].astype(o_ref.dtype)


def _to_bf16(w):
    depth, r, c = w.shape
    rows = _tile(r, max(8, CAST_BLOCK_BYTES // (4 * c) // 8 * 8))
    return pl.pallas_call(
        _cast_kernel,
        grid=(depth, r // rows),
        in_specs=[pl.BlockSpec((1, rows, c), lambda l, j: (l, j, 0))],
        out_specs=pl.BlockSpec((1, rows, c), lambda l, j: (l, j, 0)),
        out_shape=jax.ShapeDtypeStruct(w.shape, BF16),
        compiler_params=_params(2),
    )(w)


def _mod_kernel(cond_ref, w_ref, b_ref, o_ref):
    a = cond_ref[...]
    a = a / (1.0 + jnp.exp(-a))
    n = a.shape[0]
    a_hi, a_lo = _split_bf16(a)
    w_hi, w_lo = _split_bf16(w_ref[0])
    y = _dot(jnp.concatenate([a_hi, a_lo], axis=0), w_hi)
    o_ref[0] = y[:n] + y[n:] + _dot(a_hi, w_lo) + b_ref[0]


def _modulation(cond, w_mod, b_mod):
    depth, d, d6 = w_mod.shape
    n = cond.shape[0]
    tn = _tile(d6, 1536)
    return pl.pallas_call(
        _mod_kernel,
        grid=(depth, d6 // tn),
        in_specs=[pl.BlockSpec((n, d), lambda l, j: (0, 0)),
                  pl.BlockSpec((1, d, tn), lambda l, j: (l, 0, j)),
                  pl.BlockSpec((1, 1, tn), lambda l, j: (l, 0, j))],
        out_specs=pl.BlockSpec((1, n, tn), lambda l, j: (l, 0, j)),
        out_shape=jax.ShapeDtypeStruct((depth, n, d6), F32),
        compiler_params=_params(2),
    )(cond, w_mod, b_mod.reshape(depth, 1, d6))


def _adaln(x, g, shift, scale):
    y = x * lax.rsqrt(jnp.mean(x * x, axis=-1, keepdims=True) + EPS) * g
    return y * (1.0 + scale) + shift


def _dft_matrices(length):
    n = 2 * length
    blk = min(128, length)
    f = jnp.arange(length, dtype=jnp.int32)[:, None, None]
    t0 = (blk * jnp.arange(length // blk, dtype=jnp.int32))[None, :, None]
    dt = jnp.arange(blk, dtype=jnp.int32)[None, None, :]
    a = ((f * t0) % n).astype(F32) * (2.0 * math.pi / n)
    b = ((f * dt) % n).astype(F32) * (2.0 * math.pi / n)
    ca, sa, cb, sb = jnp.cos(a), jnp.sin(a), jnp.cos(b), jnp.sin(b)
    cos_m = (ca * cb - sa * sb).reshape(length, length)
    sin_m = (sa * cb + ca * sb).reshape(length, length)
    return cos_m.astype(BF16), sin_m.astype(BF16)


def _filt_kernel(z_ref, dec_ref, w1_ref, b1_ref, w2_ref, b2_ref, w3_ref, b3_ref, fr_ref, hsum_ref, hdiff_ref):
    z = z_ref[...]
    fr = fr_ref[0]
    a = jnp.sin(fr[0:1] * (_dot_3x(z, w1_ref[0]) + b1_ref[0]))
    a = jnp.sin(fr[1:2] * (_dot_3x(a, w2_ref[0]) + b2_ref[0]))
    h = _dot_3x(a, w3_ref[0]) + b3_ref[0]
    dec = dec_ref[...]
    c = HY_DIM
    row0 = lax.broadcasted_iota(jnp.int32, dec.shape, 0) == 0
    for o in range(2):
        hf = h[:, (2 * o) * c:(2 * o + 1) * c] * dec
        hb = h[:, (2 * o + 1) * c:(2 * o + 2) * c] * dec
        nrm = (jnp.sum(jnp.abs(hf), axis=0, keepdims=True)
               + jnp.sum(jnp.abs(hb), axis=0, keepdims=True) + EPS)
        hf = hf / nrm
        hb = jnp.where(row0, 0.0, hb / nrm)
        hsum_ref[0, :, o * c:(o + 1) * c] = hf + hb
        hdiff_ref[0, :, o * c:(o + 1) * c] = hf - hb


def _alternating_sign(shape):
    row = lax.broadcasted_iota(jnp.int32, shape, 0)
    return (1 - 2 * (row % 2)).astype(F32)


def _spec_kernel(hsum_ref, hdiff_ref, c_ref, s_ref, hc_ref, hs_ref, hn_ref):
    length = hsum_ref.shape[1]
    hsum = hsum_ref[0]
    hdiff = hdiff_ref[0]

    row = lax.broadcasted_iota(jnp.int32, hsum.shape, 0)
    wf = jnp.where(row == 0, 0.5 / length, 1.0 / length)
    hc_ref[0] = _dot(c_ref[...], hsum.astype(BF16)) * wf
    hs_ref[0] = _dot(s_ref[...], hdiff.astype(BF16)) * wf
    hn_ref[0] = jnp.sum(_alternating_sign(hsum.shape) * hsum, axis=0, keepdims=True) * (0.5 / length)


def _hyena_spectra(length, cs, f_w1, f_b1, f_w2, f_b2, f_w3, f_b3, freq):
    depth = f_w1.shape[0]
    pos_dim = f_w1.shape[1]
    hid = f_w1.shape[2]
    kpad = 32
    t = jnp.linspace(0.0, 1.0, length, dtype=F32)[:, None]
    w = (2.0 * math.pi / length) * jnp.arange(length, dtype=F32)[:, None]
    bands = jnp.linspace(1e-4, HY_BANDS - 1, HY_BANDS, dtype=F32)
    z = jnp.concatenate([t, jnp.cos(bands * w), -jnp.sin(bands * w)], axis=-1)
    z = jnp.pad(z, ((0, 0), (0, kpad - pos_dim)))
    w1 = jnp.pad(f_w1, ((0, 0), (0, kpad - pos_dim), (0, 0)))
    deltas = jnp.linspace(math.log(HY_DECAY_TARGET) / HY_SLOW, math.log(HY_DECAY_TARGET) / HY_FAST,
                          HY_DIM, dtype=F32)
    decay = jnp.exp(-t * jnp.abs(deltas))
    c2 = 2 * HY_DIM
    c4 = 4 * HY_DIM
    lay3 = lambda l: (l, 0, 0)
    hsum, hdiff = pl.pallas_call(
        _filt_kernel,
        grid=(depth,),
        in_specs=[pl.BlockSpec((length, kpad), lambda l: (0, 0)),
                  pl.BlockSpec((length, HY_DIM), lambda l: (0, 0)),
                  pl.BlockSpec((1, kpad, hid), lay3),
                  pl.BlockSpec((1, 1, hid), lay3),
                  pl.BlockSpec((1, hid, hid), lay3),
                  pl.BlockSpec((1, 1, hid), lay3),
                  pl.BlockSpec((1, hid, c4), lay3),
                  pl.BlockSpec((1, 1, c4), lay3),
                  pl.BlockSpec((1, 2, hid), lay3)],
        out_specs=[pl.BlockSpec((1, length, c2), lay3),
                   pl.BlockSpec((1, length, c2), lay3)],
        out_shape=[jax.ShapeDtypeStruct((depth, length, c2), F32),
                   jax.ShapeDtypeStruct((depth, length, c2), F32)],
        compiler_params=_params(1),
    )(z, decay, w1, f_b1.reshape(depth, 1, hid), f_w2, f_b2.reshape(depth, 1, hid),
      f_w3, f_b3.reshape(depth, 1, c4), freq)
    blk_in = pl.BlockSpec((1, length, c2), lay3, pipeline_mode=pl.Buffered(1))
    blk_out = pl.BlockSpec((1, length, c2), lay3)
    return pl.pallas_call(
        _spec_kernel,
        grid=(depth,),
        in_specs=[blk_in, blk_in, _resident((length, length), lambda l: (0, 0)),
                  _resident((length, length), lambda l: (0, 0))],
        out_specs=[blk_out, blk_out, pl.BlockSpec((1, 1, c2), lay3)],
        out_shape=[jax.ShapeDtypeStruct((depth, length, c2), F32),
                   jax.ShapeDtypeStruct((depth, length, c2), F32),
                   jax.ShapeDtypeStruct((depth, 1, c2), F32)],
        compiler_params=pltpu.CompilerParams(dimension_semantics=("arbitrary",),
                                             vmem_limit_bytes=HYENA_VMEM_LIMIT_BYTES),
    )(hsum, hdiff, *cs)


HY_ROWS = 256
HY_HALO = 8


HY_PAIR = 2


def _hyena_kernel(pv_ref, pg_ref, wv_ref, bv_ref, wg_ref, bg_ref, c_ref, s_ref, hc_ref, hs_ref, hn_ref, bias_ref,
                  o_ref, z_scr, xc_scr, xs_scr, yc_scr, ys_scr):
    length = pv_ref.shape[1]
    c = HY_DIM
    rows = min(HY_ROWS, length)
    chunks = [(r0, rows) for r0 in range(0, length, rows)]
    halves = [(i, slice(i * c, (i + 1) * c)) for i in range(HY_PAIR)]

    def short_conv_rows(p_ref, w_ref, b_ref, i, r0):
        lo = max(r0 - HY_HALO, 0)
        hi = min(r0 + rows + HY_HALO, length)
        u = p_ref[i, lo:hi, :].astype(F32)
        row = lax.broadcasted_iota(jnp.int32, u.shape, 0) + lo
        prev = jnp.where(row == 0, 0.0, pltpu.roll(u, 1, 0))
        nxt = jnp.where(row == length - 1, 0.0, pltpu.roll(u, hi - lo - 1, 0))
        y = prev * w_ref[0:1, :] + u * w_ref[1:2, :] + nxt * w_ref[2:3, :] + b_ref[...]
        return y[r0 - lo:r0 - lo + rows]

    @pl.when(pl.program_id(1) == 0)
    def _():
        for r0, n in chunks:
            for i, lanes in halves:
                z_scr[r0:r0 + n, lanes] = short_conv_rows(pv_ref, wv_ref, bv_ref, i, r0)

    sign = _alternating_sign((rows, c))
    zb = z_scr[...].astype(BF16)
    xc_scr[...] = _dot(c_ref[...], zb)
    xs_scr[...] = _dot(s_ref[...], zb)
    xn = [jnp.zeros((1, c), F32) for _ in halves]
    for r0, n in chunks:
        hc, hs = hc_ref[0, r0:r0 + n, :], hs_ref[0, r0:r0 + n, :]
        for i, lanes in halves:
            xc, xs = xc_scr[r0:r0 + n, lanes], xs_scr[r0:r0 + n, lanes]
            yc_scr[r0:r0 + n, lanes] = (xc * hc - xs * hs).astype(BF16)
            ys_scr[r0:r0 + n, lanes] = (xc * hs + xs * hc).astype(BF16)
            xn[i] = xn[i] + jnp.sum(sign * z_scr[r0:r0 + n, lanes], axis=0, keepdims=True)
    xc_scr[...] = _dot(c_ref[...], yc_scr[...])
    xs_scr[...] = _dot(s_ref[...], ys_scr[...])
    bias = bias_ref[0]
    for r0, n in chunks:
        for i, lanes in halves:
            z = z_scr[r0:r0 + n, lanes]
            conv = xc_scr[r0:r0 + n, lanes] + xs_scr[r0:r0 + n, lanes] + sign * (xn[i] * hn_ref[0])
            z = short_conv_rows(pg_ref, wg_ref, bg_ref, i, r0) * (conv + z * bias)
            z_scr[r0:r0 + n, lanes] = z
            o_ref[i, r0:r0 + n, :] = z.astype(o_ref.dtype)


def _hyena(p_hy, conv_w, conv_b, cs, hc, hs, hn, bias, layer):
    b, length, _ = p_hy.shape
    assert b % HY_PAIR == 0
    c = HY_DIM
    wide = HY_PAIR * c
    return pl.pallas_call(
        _hyena_kernel,
        grid=(b // HY_PAIR, 2),
        in_specs=[pl.BlockSpec((HY_PAIR, length, c), lambda i, o: (i, 0, 0), pipeline_mode=pl.Buffered(1)),
                  pl.BlockSpec((HY_PAIR, length, c), lambda i, o: (i, 0, o + 1)),
                  pl.BlockSpec((3, c), lambda i, o: (0, 0)),
                  pl.BlockSpec((1, c), lambda i, o: (0, 0)),
                  pl.BlockSpec((3, c), lambda i, o: (0, o + 1)),
                  pl.BlockSpec((1, c), lambda i, o: (0, o + 1)),
                  _resident((length, length), lambda i, o: (0, 0)),
                  _resident((length, length), lambda i, o: (0, 0)),
                  pl.BlockSpec((1, length, c), lambda i, o: (layer, 0, o), pipeline_mode=pl.Buffered(1)),
                  pl.BlockSpec((1, length, c), lambda i, o: (layer, 0, o), pipeline_mode=pl.Buffered(1)),
                  pl.BlockSpec((1, 1, c), lambda i, o: (layer, 0, o)),
                  pl.BlockSpec((1, 1, c), lambda i, o: (2 * layer + o, 0, 0))],
        out_specs=pl.BlockSpec((HY_PAIR, length, c), lambda i, o: (i, 0, 0)),
        out_shape=jax.ShapeDtypeStruct((b, length, c), BF16),
        scratch_shapes=[pltpu.VMEM((length, wide), F32), pltpu.VMEM((length, wide), F32),
                        pltpu.VMEM((length, wide), F32), pltpu.VMEM((length, wide), BF16),
                        pltpu.VMEM((length, wide), BF16)],
        compiler_params=pltpu.CompilerParams(dimension_semantics=("arbitrary", "arbitrary"),
                                             vmem_limit_bytes=HYENA_VMEM_LIMIT_BYTES),
    )(p_hy, p_hy, conv_w, conv_b, conv_w, conv_b, *cs, hc, hs, hn, bias)


def _na_prep_math(p, gq, gk, bd, qo_ref, ko_ref, vo_ref):
    def head_rms(x, g):
        sq = x * x
        hi = sq.astype(BF16)
        lo = (sq - hi.astype(F32)).astype(BF16)
        ms = (_dot(hi, bd) + _dot(lo, bd)) * (1.0 / NA_HEAD_DIM)
        return x * lax.rsqrt(ms + EPS) * g

    q = (head_rms(p[:, :NA_DIM], gq) * (NA_HEAD_DIM ** -0.5 * LOG2E)).astype(BF16)
    k = head_rms(p[:, NA_DIM:2 * NA_DIM], gk).astype(BF16)
    for h in range(NA_HEADS):
        sl = slice(h * NA_HEAD_DIM, (h + 1) * NA_HEAD_DIM)
        qo_ref[0, h] = q[:, sl]
        ko_ref[0, h] = k[:, sl]
    vo_ref[0] = p[:, 2 * NA_DIM:].T.astype(BF16)


NA_QROWS = 8
NA_KROWS = NA_QROWS + NA_WIN_ROWS


def _rpb_expand_kernel(r_ref, oh_ref, m_ref, o_ref):
    o_ref[...] = (_dot_hi(r_ref[...], oh_ref[...]) + m_ref[...]) * LOG2E


def _na_first_key_row(rb, rows):
    return jnp.clip(NA_QROWS * rb - NA_WIN_ROWS // 2, 0, rows - NA_KROWS)


def _na_block_layouts(rows):
    nrb = rows // NA_QROWS

    def layout(rb):
        w0 = min(max(NA_QROWS * rb - NA_WIN_ROWS // 2, 0), rows - NA_KROWS)
        out = []
        for j in range(NA_QROWS):
            r = NA_QROWS * rb + j
            row0 = min(max(r - NA_WIN_ROWS // 2, 0), rows - NA_WIN_ROWS)
            assert w0 <= row0 and row0 + NA_WIN_ROWS <= w0 + NA_KROWS
            out.append((r - w0, row0 - w0))
        return tuple(out)

    assert all(layout(rb) == layout(1) for rb in range(1, nrb - 1))
    return [layout(0), layout(min(1, nrb - 1)), layout(nrb - 1)]


def _na_table_kernel(t_ref, o_ref, *, layouts):
    neg = jnp.full((GRID_W, GRID_W), NEG_INF, F32)
    for pos, layout in enumerate(layouts):
        for i in range(NA_KROWS):
            pieces = [t_ref[0, i - r_rel + NA_WIN_ROWS - 1] if row0_rel <= i < row0_rel + NA_WIN_ROWS else neg
                      for r_rel, row0_rel in layout]
            o_ref[0, pos, i * GRID_W:(i + 1) * GRID_W, :] = jnp.concatenate(pieces, axis=1)


def _na_bias_table(rpb, rows):
    depth, heads, n_dr, n_dc = rpb.shape
    kc = jnp.arange(GRID_W)[:, None]
    qc = jnp.arange(GRID_W)[None, :]
    dc = jnp.clip(kc - qc + NA_WIN_COLS - 1, 0, n_dc - 1)
    cstart = jnp.clip(qc - NA_WIN_COLS // 2, 0, GRID_W - NA_WIN_COLS)
    valid = ((kc >= cstart) & (kc < cstart + NA_WIN_COLS)).reshape(1, GRID_W * GRID_W)
    n_pad = 32
    onehot = ((dc.reshape(1, -1) == jnp.arange(n_pad)[:, None]) & valid).astype(F32)
    mask = jnp.where(valid, 0.0, NEG_INF).astype(F32)
    rows = depth * heads * n_dr
    rpb2 = jnp.pad(rpb.reshape(rows, n_dc).astype(F32), ((0, 0), (0, n_pad - n_dc)))
    t1 = pl.pallas_call(
        _rpb_expand_kernel,
        out_shape=jax.ShapeDtypeStruct((rows, GRID_W * GRID_W), F32),
        compiler_params=pltpu.CompilerParams(vmem_limit_bytes=VMEM_LIMIT_BYTES),
    )(rpb2, onehot, mask).reshape(depth * heads, n_dr, GRID_W, GRID_W)
    nk, nq = NA_KROWS * GRID_W, NA_QROWS * GRID_W
    tab = pl.pallas_call(
        functools.partial(_na_table_kernel, layouts=_na_block_layouts(rows)),
        grid=(depth * heads,),
        in_specs=[pl.BlockSpec((1, n_dr, GRID_W, GRID_W), lambda g: (g, 0, 0, 0))],
        out_specs=pl.BlockSpec((1, 3, nk, nq), lambda g: (g, 0, 0, 0)),
        out_shape=jax.ShapeDtypeStruct((depth * heads, 3, nk, nq), F32),
        compiler_params=_params(1),
    )(t1)
    return tab.reshape(depth, heads, 3, nk, nq)


ATTN_CHUNK_ELEMS = 64 * 1024
ATTN_ONES_ROWS = 16
LOG2E = math.log2(math.e)


def _attend_heads(heads, q_of, sets_of, s_scrs):
    row_max = [None] * heads
    outs = [None] * heads

    def score_phase(h):
        q = q_of(h)
        scr = s_scrs[h % 2]
        off, m = 0, None
        for n, k_fn, _, bias_fn in sets_of(h):
            s = _dot_nt(k_fn(), q)
            if bias_fn is not None:
                s = s + bias_fn()
            scr[off:off + n, :] = s
            mj = jnp.max(s, axis=0, keepdims=True)
            m = mj if m is None else jnp.maximum(m, mj)
            off += n
            yield
        row_max[h] = m

    def value_phase(h):
        scr = s_scrs[h % 2]
        m = row_max[h]
        base, acc = 0, None
        for n_set, _, vt_fn, _ in sets_of(h):
            for off, n in _key_chunks(n_set, ATTN_CHUNK_ELEMS // scr.shape[1]):
                p = jnp.exp2(scr[base + off:base + off + n, :] - m).astype(BF16)
                vt = vt_fn(off, n)
                lhs = jnp.concatenate([vt, jnp.ones((ATTN_ONES_ROWS, n), BF16)], axis=0)
                oj = _dot(lhs, p)
                acc = oj if acc is None else acc + oj
                yield
            base += n_set
        dv = acc.shape[0] - ATTN_ONES_ROWS
        outs[h] = acc[:dv] / acc[dv:dv + 1]

    for _ in score_phase(0):
        pass
    for h in range(heads):
        nxt = score_phase(h + 1) if h + 1 < heads else iter(())
        cur = value_phase(h)
        done_n = done_c = False
        while not (done_n and done_c):
            if not done_n:
                done_n = next(nxt, "end") == "end"
            if not done_c:
                done_c = next(cur, "end") == "end"
    return jnp.concatenate(outs, axis=0)


def _key_chunks(total, chunk):
    sizes = [chunk] * (total // chunk)
    if total % chunk:
        sizes.append(total % chunk)
    offs = [sum(sizes[:i]) for i in range(len(sizes))]
    return list(zip(offs, sizes))


def _na_kernel(q_ref, k_ref, vt_ref, kc_ref, vct_ref, b_ref, o_ref, s0_scr, s1_scr, *, rows):
    start = pl.multiple_of(_na_first_key_row(pl.program_id(0), rows) * GRID_W, NA_WIN_ROWS // 2 * GRID_W)
    nk = NA_KROWS * GRID_W
    dh = NA_HEAD_DIM

    def sets_of(h):
        hs = slice(h * dh, (h + 1) * dh)
        window = (nk,
                  lambda: k_ref[0, h, pl.ds(start, nk), :],
                  lambda off, n: vt_ref[0, hs, pl.ds(pl.multiple_of(start + off, 128), n)],
                  lambda: b_ref[0, h, 0])
        context = (kc_ref.shape[2],
                   lambda: kc_ref[0, h],
                   lambda off, n: vct_ref[0, hs, off:off + n],
                   None)
        return [window, context]

    o_t = _attend_heads(NA_HEADS, lambda h: q_ref[0, h], sets_of, (s0_scr, s1_scr))
    o_ref[0] = o_t.T.astype(o_ref.dtype)


def _na_latent(q, k, vt, kc, vct, bias_tab, layer):
    b, heads, s, dh = q.shape
    ctx = kc.shape[2]
    rows = s // GRID_W
    assert rows % NA_QROWS == 0 and rows >= NA_KROWS and (rows - NA_KROWS) % (NA_WIN_ROWS // 2) == 0
    nrb = rows // NA_QROWS
    tq = NA_QROWS * GRID_W

    def bias_index(rb, i):
        return (layer, 0, jnp.where(rb == 0, 0, jnp.where(rb == nrb - 1, 2, 1)), 0, 0)

    full4 = lambda rb, i: (i, 0, 0, 0)
    full3 = lambda rb, i: (i, 0, 0)
    return pl.pallas_call(
        functools.partial(_na_kernel, rows=rows),
        grid=(nrb, b),
        in_specs=[pl.BlockSpec((1, heads, tq, dh), lambda rb, i: (i, 0, rb, 0)),
                  pl.BlockSpec((1, heads, s, dh), full4),
                  pl.BlockSpec((1, heads * dh, s), full3),
                  pl.BlockSpec((1, heads, ctx, dh), full4),
                  pl.BlockSpec((1, heads * dh, ctx), full3),
                  pl.BlockSpec((1, heads, 1, NA_KROWS * GRID_W, tq), bias_index)],
        out_specs=pl.BlockSpec((1, tq, heads * dh), lambda rb, i: (i, rb, 0)),
        out_shape=jax.ShapeDtypeStruct((b, s, heads * dh), BF16),
        scratch_shapes=[pltpu.VMEM((NA_KROWS * GRID_W + ctx, tq), F32)] * 2,
        compiler_params=_params(2),
    )(q, k, vt, kc, vct, bias_tab)


def _attn_kernel(*refs, heads, dv, two_sets):
    if two_sets:
        q_ref, k1_ref, vt1_ref, k2_ref, vt2_ref, o_ref, s0_scr, s1_scr = refs
        key_sets = ((k1_ref, vt1_ref), (k2_ref, vt2_ref))
    else:
        q_ref, k1_ref, vt1_ref, o_ref, s0_scr, s1_scr = refs
        key_sets = ((k1_ref, vt1_ref),)

    def sets_of(h):
        hs = slice(h * dv, (h + 1) * dv)
        return [(k_ref.shape[2],
                 lambda k_ref=k_ref: k_ref[0, h],
                 lambda off, n, vt_ref=vt_ref: vt_ref[0, hs, off:off + n],
                 None) for k_ref, vt_ref in key_sets]

    o_t = _attend_heads(heads, lambda h: q_ref[0, h], sets_of, (s0_scr, s1_scr))
    o_ref[0] = o_t.T.astype(o_ref.dtype)


def _attention(q, k1, vt1, k2=None, vt2=None, tq=512):
    b, heads, t, dq = q.shape
    dv = vt1.shape[1] // heads
    tq = _tile(t, tq)
    two_sets = k2 is not None
    n_keys = k1.shape[2] + (k2.shape[2] if two_sets else 0)
    full4 = lambda i, j: (i, 0, 0, 0)
    full3 = lambda i, j: (i, 0, 0)
    in_specs = [pl.BlockSpec((1, heads, tq, dq), lambda i, j: (i, 0, j, 0)),
                pl.BlockSpec((1,) + k1.shape[1:], full4),
                pl.BlockSpec((1,) + vt1.shape[1:], full3)]
    args = [q, k1, vt1]
    if two_sets:
        in_specs += [pl.BlockSpec((1,) + k2.shape[1:], full4),
                     pl.BlockSpec((1,) + vt2.shape[1:], full3)]
        args += [k2, vt2]
    return pl.pallas_call(
        functools.partial(_attn_kernel, heads=heads, dv=dv, two_sets=two_sets),
        grid=(b, t // tq),
        in_specs=in_specs,
        out_specs=pl.BlockSpec((1, tq, heads * dv), lambda i, j: (i, j, 0)),
        out_shape=jax.ShapeDtypeStruct((b, t, heads * dv), BF16),
        scratch_shapes=[pltpu.VMEM((n_keys, tq), F32)] * 2,
        compiler_params=_params(2),
    )(*args)


def _mla_prep_math(p, gqa, gkva, wq_ref, wkn_ref, wv_ref, g, rope_refs, qo_ref, ko_ref, vto_ref):
    rope = rope_refs is not None
    a, b_ = MLA_Q_RANK, MLA_Q_RANK + MLA_KV_RANK
    cq, ckv = p[:, :a], p[:, a:b_]
    kr, krs = p[:, b_:b_ + MLA_SLOT], p[:, b_ + MLA_SLOT:b_ + 2 * MLA_SLOT]

    def rms(x, g_):
        return x * lax.rsqrt(jnp.mean(x * x, axis=-1, keepdims=True) + EPS) * g_

    cqn = rms(cq, gqa).astype(BF16)
    ckvn = rms(ckv, gkva).astype(BF16)
    qa = _dot(cqn, wq_ref[...])
    kn = _dot(ckvn, wkn_ref[...])
    aq, ak = g[0:1], g[2:3]
    if rope:
        wqs_ref, cos_ref, sin_ref = rope_refs
        qs = _dot(cqn, wqs_ref[...])
        cos_t, sin_t = cos_ref[...], sin_ref[...]
        aq, bq = aq * cos_t, g[1:2] * sin_t
        ak, k_rot = ak * cos_t, krs * (g[3:4] * sin_t)

    def inv_rms(x):
        return lax.rsqrt(jnp.sum(x * x, axis=-1, keepdims=True) * (1.0 / MLA_QK) + EPS)

    for h in range(MLA_HEADS):
        sl = slice(h * MLA_SLOT, (h + 1) * MLA_SLOT)
        xq = qa[:, sl]
        yq = xq * aq
        if rope:
            yq = yq + qs[:, sl] * bq
        qo_ref[0, h] = (yq * (inv_rms(xq) * (MLA_QK ** -0.5 * LOG2E))).astype(BF16)
        xk = kn[:, sl] + kr
        yk = xk * ak
        if rope:
            yk = yk + k_rot
        ko_ref[0, h] = (yk * inv_rms(xk)).astype(BF16)
    vto_ref[0] = _dot_nt(wv_ref[...], ckvn).astype(BF16)


_ROPE_SWAP = tuple(list(range(8, 16)) + list(range(0, 8)) + list(range(24, 32)) + list(range(16, 24)))


def _slot(nope, rope_part):
    lead = (nope if nope is not None else rope_part).shape[:-1]
    dt = (nope if nope is not None else rope_part).dtype
    z = lambda n: jnp.zeros(lead + (n,), dt)
    return jnp.concatenate([nope if nope is not None else z(MLA_NOPE),
                            rope_part if rope_part is not None else z(MLA_ROPE),
                            z(MLA_SLOT - MLA_QK)], axis=-1)


def _rope_tables(s):
    t = jnp.arange(s)
    pos = jnp.stack([t // GRID_W, t % GRID_W], axis=-1).astype(F32)
    inv = ROPE_BASE ** (-jnp.arange(ROPE_FREQS, dtype=F32) / ROPE_FREQS)
    ang = pos[:, :, None] * inv
    cos, sin = jnp.cos(ang), jnp.sin(ang)
    cos_t = jnp.concatenate([cos[:, 0], cos[:, 0], cos[:, 1], cos[:, 1]], axis=-1)
    sin_t = jnp.concatenate([-sin[:, 0], sin[:, 0], -sin[:, 1], sin[:, 1]], axis=-1)
    return _slot(jnp.ones((s, MLA_NOPE), F32), cos_t), _slot(None, sin_t)


def _mla_weights(w_q_up, w_kv_up, g_q, g_k):
    swap = jnp.array(_ROPE_SWAP)
    wq = w_q_up.reshape(MLA_Q_RANK, MLA_HEADS, MLA_QK)
    wkv = w_kv_up.reshape(MLA_KV_RANK, MLA_HEADS, MLA_NOPE + MLA_V)
    flat = lambda w: w.reshape(w.shape[0], -1).astype(BF16)
    wq_slot = flat(_slot(wq[..., :MLA_NOPE], wq[..., MLA_NOPE:]))
    wqs_slot = flat(_slot(None, wq[..., MLA_NOPE:][..., swap]))
    wkn_slot = flat(_slot(wkv[..., :MLA_NOPE], None))
    wv = flat(wkv[..., MLA_NOPE:]).T
    gains = jnp.stack([_slot(g_q[:MLA_NOPE], g_q[MLA_NOPE:]), _slot(None, g_q[MLA_NOPE:][swap]),
                       _slot(g_k[:MLA_NOPE], g_k[MLA_NOPE:]), _slot(None, g_k[MLA_NOPE:][swap])])
    return wq_slot, wkn_slot, wv, gains, wqs_slot


def _project_kernel(*refs, rope):
    (x_ref, sh_ref, sc_ref, g_ref, w_hy_ref, w_na_ref, w_m_ref, gq_na_ref, gk_na_ref, bd_ref,
     gqa_ref, gkva_ref, wq_ref, wkn_ref, wv_ref, gains_ref) = refs[:16]
    rope_refs = refs[16:19] if rope else None
    o_hy_ref, qn_ref, kn_ref, vtn_ref, qm_ref, km_ref, vtm_ref = refs[-7:]
    h = _adaln(x_ref[0], g_ref[...], sh_ref[0, 0], sc_ref[0, 0]).astype(BF16)
    _mla_prep_math(_dot(h, w_m_ref[0]), gqa_ref[...], gkva_ref[...], wq_ref, wkn_ref, wv_ref, gains_ref[...],
                   rope_refs, qm_ref, km_ref, vtm_ref)
    _na_prep_math(_dot(h, w_na_ref[0]), gq_na_ref[...], gk_na_ref[...], bd_ref[...], qn_ref, kn_ref, vtn_ref)
    o_hy_ref[0] = _dot(h, w_hy_ref[0]).astype(BF16)


def _project(x, mod, g, w_in, w_m, layer, na_g_q, na_g_k, g_qa, g_kva, mla_w, rope_tabs, tm):
    b, t, d = x.shape
    tm = _tile(t, tm)
    n_hy, n_na, n_m = 3 * HY_DIM, 3 * NA_DIM, w_m.shape[2]
    assert n_hy == n_na
    wq_slot, wkn_slot, wv, gains, wqs_slot = mla_w
    rope = rope_tabs is not None
    gq = jnp.tile(na_g_q, NA_HEADS)[None, :]
    gk = jnp.tile(na_g_k, NA_HEADS)[None, :]
    head = jnp.arange(NA_DIM) // NA_HEAD_DIM
    bd = (head[:, None] == head[None, :]).astype(BF16)
    const = lambda i, j: (0, 0)
    in_specs = [pl.BlockSpec((1, tm, d), lambda i, j: (i, j, 0)),
                pl.BlockSpec((1, 1, 1, d), lambda i, j: (i, 0, 0, 0)),
                pl.BlockSpec((1, 1, 1, d), lambda i, j: (i, 1, 0, 0)),
                pl.BlockSpec((1, d), const),
                _resident((1, d, n_hy), lambda i, j: (layer, 0, 0)),
                _resident((1, d, n_na), lambda i, j: (layer, 0, 1)),
                _resident((1, d, n_m), lambda i, j: (layer, 0, 0)),
                pl.BlockSpec((1, NA_DIM), const), pl.BlockSpec((1, NA_DIM), const),
                pl.BlockSpec((NA_DIM, NA_DIM), const),
                pl.BlockSpec((1, MLA_Q_RANK), const), pl.BlockSpec((1, MLA_KV_RANK), const),
                pl.BlockSpec(wq_slot.shape, const), pl.BlockSpec(wkn_slot.shape, const),
                pl.BlockSpec(wv.shape, const), pl.BlockSpec(gains.shape, const)]
    args = [x, mod, mod, g, w_in, w_in, w_m, gq, gk, bd, g_qa[None, :], g_kva[None, :],
            wq_slot, wkn_slot, wv, gains]
    if rope:
        in_specs += [pl.BlockSpec(wqs_slot.shape, const),
                     pl.BlockSpec((tm, MLA_SLOT), lambda i, j: (j, 0)),
                     pl.BlockSpec((tm, MLA_SLOT), lambda i, j: (j, 0))]
        args += [wqs_slot] + list(rope_tabs)
    tok_major = lambda heads, width: (jax.ShapeDtypeStruct((b, heads, t, width), BF16),
                                      pl.BlockSpec((1, heads, tm, width), lambda i, j: (i, 0, j, 0)))
    transposed = lambda rows: (jax.ShapeDtypeStruct((b, rows, t), BF16),
                               pl.BlockSpec((1, rows, tm), lambda i, j: (i, 0, j)))
    outs = [(jax.ShapeDtypeStruct((b, t, n_hy), BF16), pl.BlockSpec((1, tm, n_hy), lambda i, j: (i, j, 0))),
            tok_major(NA_HEADS, NA_HEAD_DIM), tok_major(NA_HEADS, NA_HEAD_DIM), transposed(NA_DIM),
            tok_major(MLA_HEADS, MLA_SLOT), tok_major(MLA_HEADS, MLA_SLOT), transposed(MLA_HEADS * MLA_V)]
    return pl.pallas_call(
        functools.partial(_project_kernel, rope=rope),
        grid=(b, t // tm),
        in_specs=in_specs,
        out_specs=[spec for _, spec in outs],
        out_shape=[shape for shape, _ in outs],
        compiler_params=_params(2),
    )(*args)


def _mix_mlp_kernel(hy_ref, na_ref, mla_ref, x_ref, gate1_ref, sh_ref, sc_ref, gate2_ref, g_ref,
                    w_hy_ref, w_na_ref, w_mla_ref, w1_ref, b1_ref, w2_ref, b2_ref, o_ref, *, chunk):
    mix = (_dot(hy_ref[0], w_hy_ref[0]) + _dot(na_ref[0], w_na_ref[0]) + _dot(mla_ref[0], w_mla_ref[0]))
    x = x_ref[0] + gate1_ref[0, 0] * mix
    h = _adaln(x, g_ref[...], sh_ref[0, 0], sc_ref[0, 0]).astype(BF16)
    d_ff = w1_ref.shape[2]
    acc = jnp.zeros(x.shape, F32)
    for c0 in range(0, d_ff, chunk):
        a = jnp.maximum(_dot(h, w1_ref[0, :, c0:c0 + chunk]) + b1_ref[:, c0:c0 + chunk], 0.0)
        acc = acc + _dot((a * a).astype(BF16), w2_ref[0, c0:c0 + chunk, :])
    o_ref[0] = x + gate2_ref[0, 0] * (acc + b2_ref[...])


def _mix_mlp(o_hy, o_na, o_mla, x, mod, w_out, g, w1, b1, w2, b2, layer, tm):
    b, t, d = x.shape
    d_ff = w1.shape[2]
    n_hy, n_na, n_mla = o_hy.shape[-1], o_na.shape[-1], o_mla.shape[-1]
    assert n_hy == n_na and n_mla == n_hy + n_na
    tm = _tile(t, tm)
    tok = lambda n: pl.BlockSpec((1, tm, n), lambda i, j: (i, j, 0))
    modspec = lambda k: pl.BlockSpec((1, 1, 1, d), lambda i, j: (i, k, 0, 0))
    const = lambda i, j: (0, 0)
    lay = lambda i, j: (layer, 0, 0)
    return pl.pallas_call(
        functools.partial(_mix_mlp_kernel, chunk=_tile(d_ff, 1024)),
        grid=(b, t // tm),
        in_specs=[tok(n_hy), tok(n_na), tok(n_mla), tok(d),
                  modspec(2), modspec(3), modspec(4), modspec(5),
                  pl.BlockSpec((1, d), const),
                  _resident((1, n_hy, d), lay),
                  _resident((1, n_na, d), lambda i, j: (layer, 1, 0)),
                  _resident((1, n_mla, d), lambda i, j: (layer, 1, 0)),
                  _resident((1, d, d_ff), lay), pl.BlockSpec((1, d_ff), const),
                  _resident((1, d_ff, d), lay), pl.BlockSpec((1, d), const)],
        out_specs=tok(d),
        out_shape=jax.ShapeDtypeStruct((b, t, d), F32),
        compiler_params=_params(2),
    )(o_hy, o_na, o_mla, x, mod, mod, mod, mod, g, w_out, w_out, w_out, w1, b1, w2, b2)


def kernel(x, c, ctx, c_ctx, w_mod, b_mod, g_norm1, w_in, hy_conv_w, hy_conv_b, hy_f_w1, hy_f_b1, hy_f_w2, hy_f_b2, hy_f_w3, hy_f_b3, hy_freq, hy_bias, na_g_q, na_g_k, na_rpb, mla_g_qa, mla_g_kva, mla_w_q_up, mla_w_kv_up, mla_g_q, mla_g_k, w_out, g_norm2, w_ff1, b_ff1, w_ff2, b_ff2):
    b, s, d = x.shape
    lc = ctx.shape[1]
    depth = w_mod.shape[0]
    in_hn = 3 * HY_DIM + 3 * NA_DIM
    swap = jnp.array(_ROPE_SWAP)

    n_cond = -(-(b + 1) // 8) * 8
    cond = jnp.zeros((n_cond, d), F32).at[:b].set(c).at[b].set(c_ctx)
    mods = _modulation(cond, w_mod, b_mod).reshape(depth, n_cond, 6, 1, d)

    rope_tabs = _rope_tables(s)
    cs_x = _dft_matrices(s)
    cs_c = _dft_matrices(lc)
    filt = (hy_f_w1, hy_f_b1, hy_f_w2, hy_f_b2, hy_f_w3, hy_f_b3, hy_freq)
    spec_x = _hyena_spectra(s, cs_x, *filt)
    spec_c = _hyena_spectra(lc, cs_c, *filt)
    bias_tab = _na_bias_table(na_rpb, s // GRID_W)
    hy_bias = hy_bias.reshape(2 * depth, 1, HY_DIM)

    w_in_b, w_out_b = _to_bf16(w_in), _to_bf16(w_out)
    w_ff1_b, w_ff2_b = _to_bf16(w_ff1), _to_bf16(w_ff2)
    w_kr = w_in[:, :, in_hn + MLA_Q_RANK + MLA_KV_RANK:]
    w_m_b = jnp.concatenate([w_in[:, :, in_hn:in_hn + MLA_Q_RANK + MLA_KV_RANK], _slot(None, w_kr),
                             _slot(None, w_kr[:, :, swap])], axis=-1).astype(BF16)

    cx = ctx
    for i in range(depth):
        last = i == depth - 1
        mod_x = mods[i, :b]
        mod_c = jnp.broadcast_to(mods[i, b], (b, 6, 1, d))
        g1 = g_norm1[i][None, :]
        mla_w = _mla_weights(mla_w_q_up[i], mla_w_kv_up[i], mla_g_q[i], mla_g_k[i])
        prep = (i, na_g_q[i], na_g_k[i], mla_g_qa[i], mla_g_kva[i], mla_w)
        px_hy, qx, kx, vtx, mqx, mkx, mvx = _project(x, mod_x, g1, w_in_b, w_m_b, *prep, rope_tabs,
                                                     PROJECT_TOKEN_TILE)
        pc_hy, qc, kc, vtc, mqc, mkc, mvc = _project(cx, mod_c, g1, w_in_b, w_m_b, *prep, None,
                                                     PROJECT_TOKEN_TILE)

        o_na = _na_latent(qx, kx, vtx, kc, vtc, bias_tab, i)
        o_mla = _attention(mqx, mkx, mvx, mkc, mvc)

        o_hy = _hyena(px_hy, hy_conv_w[i], hy_conv_b[i][None, :], cs_x, *spec_x, hy_bias, i)

        g2 = g_norm2[i][None, :]
        b1, b2 = b_ff1[i][None, :], b_ff2[i][None, :]
        x = _mix_mlp(o_hy, o_na, o_mla, x, mod_x, w_out_b, g2, w_ff1_b, b1, w_ff2_b, b2, i, MLP_TOKEN_TILE)

        if not last:
            oc_hy = _hyena(pc_hy, hy_conv_w[i], hy_conv_b[i][None, :], cs_c, *spec_c, hy_bias, i)
            oc_na = _attention(qc, kc, vtc)
            oc_mla = _attention(mqc, mkc, mvc)
            cx = _mix_mlp(oc_hy, oc_na, oc_mla, cx, mod_c, w_out_b, g2, w_ff1_b, b1, w_ff2_b, b2, i,
                          MLP_TOKEN_TILE)
    return x
```

```python
import functools
import math

import jax
import jax.numpy as jnp
from jax import lax
from jax.experimental import pallas as pl
from jax.experimental.pallas import tpu as pltpu

F32 = jnp.float32
BF16 = jnp.bfloat16
HIGHEST = lax.Precision.HIGHEST

EPS = 1e-6
NEG_INF = -1e9
GRID_W = 64

HY_DIM = 256
HY_BANDS = 8
HY_DECAY_TARGET = 1e-2
HY_FAST = 0.3
HY_SLOW = 1.5

NA_HEADS = 4
NA_HEAD_DIM = 64
NA_DIM = NA_HEADS * NA_HEAD_DIM
NA_WIN_ROWS = 8
NA_WIN_COLS = 16

MLA_HEADS = 8
MLA_Q_RANK = 256
MLA_KV_RANK = 128
MLA_NOPE = 64
MLA_ROPE = 32
MLA_V = 64
MLA_QK = MLA_NOPE + MLA_ROPE
MLA_SLOT = 128
ROPE_BASE = 10000.0
ROPE_FREQS = MLA_ROPE // 4

VMEM_LIMIT_BYTES = 56 * 1024 * 1024
HYENA_VMEM_LIMIT_BYTES = 60 * 1024 * 1024
PROJECT_TOKEN_TILE = 1024
MLP_TOKEN_TILE = 1024

def _params(n_grid_dims):
    return pltpu.CompilerParams(dimension_semantics=("arbitrary",) * n_grid_dims,
                                vmem_limit_bytes=VMEM_LIMIT_BYTES)


def _tile(total, preferred):
    t = min(total, preferred)
    while total % t:
        t //= 2
    return t


def _resident(shape, index_map):
    return pl.BlockSpec(shape, index_map, pipeline_mode=pl.Buffered(1))


def _dot(a, b):
    return jnp.dot(a, b, preferred_element_type=F32)


def _split_bf16(x):
    hi = x.astype(BF16)
    return hi, (x - hi.astype(F32)).astype(BF16)


def _dot_3x(a, b):
    a_hi, a_lo = _split_bf16(a)
    b_hi, b_lo = _split_bf16(b)
    return _dot(a_hi, b_hi) + (_dot(a_hi, b_lo) + _dot(a_lo, b_hi))


def _dot_hi(a, b):
    return jnp.dot(a, b, preferred_element_type=F32, precision=HIGHEST)


def _dot_nt(a, b):
    return lax.dot_general(a, b, (((1,), (1,)), ((), ())), preferred_element_type=F32)


def _mod_kernel(cond_ref, w_ref, b_ref, o_ref):
    a = cond_ref[...]
    a = a / (1.0 + jnp.exp(-a))
    n = a.shape[0]
    a_hi, a_lo = _split_bf16(a)
    w_hi, w_lo = _split_bf16(w_ref[0])
    y = _dot(jnp.concatenate([a_hi, a_lo], axis=0), w_hi)
    o_ref[0] = y[:n] + y[n:] + _dot(a_hi, w_lo) + b_ref[0]


def _modulation(cond, w_mod, b_mod):
    depth, d, d6 = w_mod.shape
    n = cond.shape[0]
    tn = _tile(d6, 1536)
    return pl.pallas_call(
        _mod_kernel,
        grid=(depth, d6 // tn),
        in_specs=[pl.BlockSpec((n, d), lambda l, j: (0, 0)),
                  pl.BlockSpec((1, d, tn), lambda l, j: (l, 0, j)),
                  pl.BlockSpec((1, 1, tn), lambda l, j: (l, 0, j))],
        out_specs=pl.BlockSpec((1, n, tn), lambda l, j: (l, 0, j)),
        out_shape=jax.ShapeDtypeStruct((depth, n, d6), F32),
        compiler_params=_params(2),
    )(cond, w_mod, b_mod.reshape(depth, 1, d6))


def _adaln(x, g, shift, scale):
    y = x * lax.rsqrt(jnp.mean(x * x, axis=-1, keepdims=True) + EPS) * g
    return y * (1.0 + scale) + shift


def _dft_matrices(length):
    n = 2 * length
    blk = min(128, length)
    n_blk = length // blk
    f = jnp.arange(length, dtype=jnp.int32)[:, None]
    t0 = (blk * jnp.arange(n_blk, dtype=jnp.int32))[None, :]
    dt = jnp.arange(blk, dtype=jnp.int32)[None, :]
    a = ((f * t0) % n).astype(F32) * (2.0 * math.pi / n)
    b = ((f * dt) % n).astype(F32) * (2.0 * math.pi / n)
    rows = _tile(length, 256)

    def combine(ca_ref, sa_ref, cb_ref, sb_ref, c_ref, s_ref):
        cb, sb = cb_ref[...], sb_ref[...]
        for j in range(n_blk):
            ca, sa = ca_ref[:, j:j + 1], sa_ref[:, j:j + 1]
            c_ref[:, j * blk:(j + 1) * blk] = (ca * cb - sa * sb).astype(BF16)
            s_ref[:, j * blk:(j + 1) * blk] = (sa * cb + ca * sb).astype(BF16)

    coarse = pl.BlockSpec((rows, n_blk), lambda i: (i, 0))
    fine = pl.BlockSpec((rows, blk), lambda i: (i, 0))
    full = pl.BlockSpec((rows, length), lambda i: (i, 0))
    out = jax.ShapeDtypeStruct((length, length), BF16)
    return pl.pallas_call(
        combine,
        grid=(length // rows,),
        in_specs=[coarse, coarse, fine, fine],
        out_specs=[full, full],
        out_shape=[out, out],
        compiler_params=_params(1),
    )(jnp.cos(a), jnp.sin(a), jnp.cos(b), jnp.sin(b))


def _filt_kernel(z_ref, dec_ref, w1_ref, b1_ref, w2_ref, b2_ref, w3_ref, b3_ref, fr_ref, hsum_ref, hdiff_ref):
    z = z_ref[...]
    fr = fr_ref[0]
    a = jnp.sin(fr[0:1] * (_dot_3x(z, w1_ref[0]) + b1_ref[0]))
    a = jnp.sin(fr[1:2] * (_dot_3x(a, w2_ref[0]) + b2_ref[0]))
    h = _dot_3x(a, w3_ref[0]) + b3_ref[0]
    dec = dec_ref[...]
    c = HY_DIM
    row0 = lax.broadcasted_iota(jnp.int32, dec.shape, 0) == 0
    for o in range(2):
        hf = h[:, (2 * o) * c:(2 * o + 1) * c] * dec
        hb = h[:, (2 * o + 1) * c:(2 * o + 2) * c] * dec
        nrm = (jnp.sum(jnp.abs(hf), axis=0, keepdims=True)
               + jnp.sum(jnp.abs(hb), axis=0, keepdims=True) + EPS)
        hf = hf / nrm
        hb = jnp.where(row0, 0.0, hb / nrm)
        hsum_ref[0, :, o * c:(o + 1) * c] = hf + hb
        hdiff_ref[0, :, o * c:(o + 1) * c] = hf - hb


def _alternating_sign(shape):
    row = lax.broadcasted_iota(jnp.int32, shape, 0)
    return (1 - 2 * (row % 2)).astype(F32)


def _spec_kernel(hsum_ref, hdiff_ref, c_ref, s_ref, hc_ref, hs_ref, hn_ref):
    length = hsum_ref.shape[1]
    hsum = hsum_ref[0]
    hdiff = hdiff_ref[0]

    row = lax.broadcasted_iota(jnp.int32, hsum.shape, 0)
    wf = jnp.where(row == 0, 0.5 / length, 1.0 / length)
    hc_ref[0] = _dot(c_ref[...], hsum.astype(BF16)) * wf
    hs_ref[0] = _dot(s_ref[...], hdiff.astype(BF16)) * wf
    hn_ref[0] = jnp.sum(_alternating_sign(hsum.shape) * hsum, axis=0, keepdims=True) * (0.5 / length)


def _hyena_spectra(length, cs, f_w1, f_b1, f_w2, f_b2, f_w3, f_b3, freq):
    depth = f_w1.shape[0]
    pos_dim = f_w1.shape[1]
    hid = f_w1.shape[2]
    kpad = 32
    t = jnp.linspace(0.0, 1.0, length, dtype=F32)[:, None]
    w = (2.0 * math.pi / length) * jnp.arange(length, dtype=F32)[:, None]
    bands = jnp.linspace(1e-4, HY_BANDS - 1, HY_BANDS, dtype=F32)
    z = jnp.concatenate([t, jnp.cos(bands * w), -jnp.sin(bands * w)], axis=-1)
    z = jnp.pad(z, ((0, 0), (0, kpad - pos_dim)))
    w1 = jnp.pad(f_w1, ((0, 0), (0, kpad - pos_dim), (0, 0)))
    deltas = jnp.linspace(math.log(HY_DECAY_TARGET) / HY_SLOW, math.log(HY_DECAY_TARGET) / HY_FAST,
                          HY_DIM, dtype=F32)
    decay = jnp.exp(-t * jnp.abs(deltas))
    c2 = 2 * HY_DIM
    c4 = 4 * HY_DIM
    lay3 = lambda l: (l, 0, 0)
    hsum, hdiff = pl.pallas_call(
        _filt_kernel,
        grid=(depth,),
        in_specs=[pl.BlockSpec((length, kpad), lambda l: (0, 0)),
                  pl.BlockSpec((length, HY_DIM), lambda l: (0, 0)),
                  pl.BlockSpec((1, kpad, hid), lay3),
                  pl.BlockSpec((1, 1, hid), lay3),
                  pl.BlockSpec((1, hid, hid), lay3),
                  pl.BlockSpec((1, 1, hid), lay3),
                  pl.BlockSpec((1, hid, c4), lay3),
                  pl.BlockSpec((1, 1, c4), lay3),
                  pl.BlockSpec((1, 2, hid), lay3)],
        out_specs=[pl.BlockSpec((1, length, c2), lay3),
                   pl.BlockSpec((1, length, c2), lay3)],
        out_shape=[jax.ShapeDtypeStruct((depth, length, c2), F32),
                   jax.ShapeDtypeStruct((depth, length, c2), F32)],
        compiler_params=_params(1),
    )(z, decay, w1, f_b1.reshape(depth, 1, hid), f_w2, f_b2.reshape(depth, 1, hid),
      f_w3, f_b3.reshape(depth, 1, c4), freq)
    blk_in = pl.BlockSpec((1, length, c2), lay3, pipeline_mode=pl.Buffered(1))
    blk_out = pl.BlockSpec((1, length, c2), lay3)
    return pl.pallas_call(
        _spec_kernel,
        grid=(depth,),
        in_specs=[blk_in, blk_in, _resident((length, length), lambda l: (0, 0)),
                  _resident((length, length), lambda l: (0, 0))],
        out_specs=[blk_out, blk_out, pl.BlockSpec((1, 1, c2), lay3)],
        out_shape=[jax.ShapeDtypeStruct((depth, length, c2), F32),
                   jax.ShapeDtypeStruct((depth, length, c2), F32),
                   jax.ShapeDtypeStruct((depth, 1, c2), F32)],
        compiler_params=pltpu.CompilerParams(dimension_semantics=("arbitrary",),
                                             vmem_limit_bytes=HYENA_VMEM_LIMIT_BYTES),
    )(hsum, hdiff, *cs)


HY_ROWS = 256
HY_HALO = 8


HY_PAIR = 2


def _hyena_kernel(pv_ref, pg_ref, wv_ref, bv_ref, wg_ref, bg_ref, c_ref, s_ref, hc_ref, hs_ref, hn_ref, bias_ref,
                  o_ref, z_scr, xc_scr, xs_scr, yc_scr, ys_scr):
    length = pv_ref.shape[1]
    c = HY_DIM
    rows = min(HY_ROWS, length)
    chunks = [(r0, rows) for r0 in range(0, length, rows)]
    halves = [(i, slice(i * c, (i + 1) * c)) for i in range(HY_PAIR)]

    def short_conv_rows(p_ref, w_ref, b_ref, i, r0):
        lo = max(r0 - HY_HALO, 0)
        hi = min(r0 + rows + HY_HALO, length)
        u = p_ref[i, lo:hi, :].astype(F32)
        row = lax.broadcasted_iota(jnp.int32, u.shape, 0) + lo
        prev = jnp.where(row == 0, 0.0, pltpu.roll(u, 1, 0))
        nxt = jnp.where(row == length - 1, 0.0, pltpu.roll(u, hi - lo - 1, 0))
        y = prev * w_ref[0:1, :] + u * w_ref[1:2, :] + nxt * w_ref[2:3, :] + b_ref[...]
        return y[r0 - lo:r0 - lo + rows]

    @pl.when(pl.program_id(1) == 0)
    def _():
        for r0, n in chunks:
            for i, lanes in halves:
                z_scr[r0:r0 + n, lanes] = short_conv_rows(pv_ref, wv_ref, bv_ref, i, r0)

    sign = _alternating_sign((rows, c))
    zb = z_scr[...].astype(BF16)
    xc_scr[...] = _dot(c_ref[...], zb)
    xs_scr[...] = _dot(s_ref[...], zb)
    xn = [jnp.zeros((1, c), F32) for _ in halves]
    for r0, n in chunks:
        hc, hs = hc_ref[0, r0:r0 + n, :], hs_ref[0, r0:r0 + n, :]
        for i, lanes in halves:
            xc, xs = xc_scr[r0:r0 + n, lanes], xs_scr[r0:r0 + n, lanes]
            yc_scr[r0:r0 + n, lanes] = (xc * hc - xs * hs).astype(BF16)
            ys_scr[r0:r0 + n, lanes] = (xc * hs + xs * hc).astype(BF16)
            xn[i] = xn[i] + jnp.sum(sign * z_scr[r0:r0 + n, lanes], axis=0, keepdims=True)
    xc_scr[...] = _dot(c_ref[...], yc_scr[...])
    xs_scr[...] = _dot(s_ref[...], ys_scr[...])
    bias = bias_ref[0]
    for r0, n in chunks:
        for i, lanes in halves:
            z = z_scr[r0:r0 + n, lanes]
            conv = xc_scr[r0:r0 + n, lanes] + xs_scr[r0:r0 + n, lanes] + sign * (xn[i] * hn_ref[0])
            z = short_conv_rows(pg_ref, wg_ref, bg_ref, i, r0) * (conv + z * bias)
            z_scr[r0:r0 + n, lanes] = z
            o_ref[i, r0:r0 + n, :] = z.astype(o_ref.dtype)


def _hyena(p_hy, conv_w, conv_b, cs, hc, hs, hn, bias, layer):
    b, length, _ = p_hy.shape
    assert b % HY_PAIR == 0
    c = HY_DIM
    wide = HY_PAIR * c
    return pl.pallas_call(
        _hyena_kernel,
        grid=(b // HY_PAIR, 2),
        in_specs=[pl.BlockSpec((HY_PAIR, length, c), lambda i, o: (i, 0, 0), pipeline_mode=pl.Buffered(1)),
                  pl.BlockSpec((HY_PAIR, length, c), lambda i, o: (i, 0, o + 1)),
                  pl.BlockSpec((3, c), lambda i, o: (0, 0)),
                  pl.BlockSpec((1, c), lambda i, o: (0, 0)),
                  pl.BlockSpec((3, c), lambda i, o: (0, o + 1)),
                  pl.BlockSpec((1, c), lambda i, o: (0, o + 1)),
                  _resident((length, length), lambda i, o: (0, 0)),
                  _resident((length, length), lambda i, o: (0, 0)),
                  pl.BlockSpec((1, length, c), lambda i, o: (layer, 0, o), pipeline_mode=pl.Buffered(1)),
                  pl.BlockSpec((1, length, c), lambda i, o: (layer, 0, o), pipeline_mode=pl.Buffered(1)),
                  pl.BlockSpec((1, 1, c), lambda i, o: (layer, 0, o)),
                  pl.BlockSpec((1, 1, c), lambda i, o: (2 * layer + o, 0, 0))],
        out_specs=pl.BlockSpec((HY_PAIR, length, c), lambda i, o: (i, 0, 0)),
        out_shape=jax.ShapeDtypeStruct((b, length, c), BF16),
        scratch_shapes=[pltpu.VMEM((length, wide), F32), pltpu.VMEM((length, wide), F32),
                        pltpu.VMEM((length, wide), F32), pltpu.VMEM((length, wide), BF16),
                        pltpu.VMEM((length, wide), BF16)],
        compiler_params=pltpu.CompilerParams(dimension_semantics=("arbitrary", "arbitrary"),
                                             vmem_limit_bytes=HYENA_VMEM_LIMIT_BYTES),
    )(p_hy, p_hy, conv_w, conv_b, conv_w, conv_b, *cs, hc, hs, hn, bias)


def _na_prep_math(p, gq, gk, bd, qo_ref, ko_ref, vo_ref):
    def head_rms(x, g):
        sq = x * x
        hi = sq.astype(BF16)
        lo = (sq - hi.astype(F32)).astype(BF16)
        ms = (_dot(hi, bd) + _dot(lo, bd)) * (1.0 / NA_HEAD_DIM)
        return x * lax.rsqrt(ms + EPS) * g

    q = (head_rms(p[:, :NA_DIM], gq) * (NA_HEAD_DIM ** -0.5 * LOG2E)).astype(BF16)
    k = head_rms(p[:, NA_DIM:2 * NA_DIM], gk).astype(BF16)
    for h in range(NA_HEADS):
        sl = slice(h * NA_HEAD_DIM, (h + 1) * NA_HEAD_DIM)
        qo_ref[0, h] = q[:, sl]
        ko_ref[0, h] = k[:, sl]
    vo_ref[0] = p[:, 2 * NA_DIM:].T.astype(BF16)


NA_QROWS = 8
NA_KROWS = NA_QROWS + NA_WIN_ROWS


def _rpb_expand_kernel(r_ref, oh_ref, m_ref, o_ref):
    o_ref[...] = (_dot_hi(r_ref[...], oh_ref[...]) + m_ref[...]) * LOG2E


def _na_first_key_row(rb, rows):
    return jnp.clip(NA_QROWS * rb - NA_WIN_ROWS // 2, 0, rows - NA_KROWS)


def _na_block_layouts(rows):
    nrb = rows // NA_QROWS

    def layout(rb):
        w0 = min(max(NA_QROWS * rb - NA_WIN_ROWS // 2, 0), rows - NA_KROWS)
        out = []
        for j in range(NA_QROWS):
            r = NA_QROWS * rb + j
            row0 = min(max(r - NA_WIN_ROWS // 2, 0), rows - NA_WIN_ROWS)
            assert w0 <= row0 and row0 + NA_WIN_ROWS <= w0 + NA_KROWS
            out.append((r - w0, row0 - w0))
        return tuple(out)

    assert all(layout(rb) == layout(1) for rb in range(1, nrb - 1))
    return [layout(0), layout(min(1, nrb - 1)), layout(nrb - 1)]


def _na_table_kernel(t_ref, o_ref, *, layouts):
    neg = jnp.full((GRID_W, GRID_W), NEG_INF, F32)
    for pos, layout in enumerate(layouts):
        for i in range(NA_KROWS):
            pieces = [t_ref[0, i - r_rel + NA_WIN_ROWS - 1] if row0_rel <= i < row0_rel + NA_WIN_ROWS else neg
                      for r_rel, row0_rel in layout]
            o_ref[0, pos, i * GRID_W:(i + 1) * GRID_W, :] = jnp.concatenate(pieces, axis=1)


def _na_bias_table(rpb, rows):
    depth, heads, n_dr, n_dc = rpb.shape
    kc = jnp.arange(GRID_W)[:, None]
    qc = jnp.arange(GRID_W)[None, :]
    dc = jnp.clip(kc - qc + NA_WIN_COLS - 1, 0, n_dc - 1)
    cstart = jnp.clip(qc - NA_WIN_COLS // 2, 0, GRID_W - NA_WIN_COLS)
    valid = ((kc >= cstart) & (kc < cstart + NA_WIN_COLS)).reshape(1, GRID_W * GRID_W)
    n_pad = 32
    onehot = ((dc.reshape(1, -1) == jnp.arange(n_pad)[:, None]) & valid).astype(F32)
    mask = jnp.where(valid, 0.0, NEG_INF).astype(F32)
    rows = depth * heads * n_dr
    rpb2 = jnp.pad(rpb.reshape(rows, n_dc).astype(F32), ((0, 0), (0, n_pad - n_dc)))
    t1 = pl.pallas_call(
        _rpb_expand_kernel,
        out_shape=jax.ShapeDtypeStruct((rows, GRID_W * GRID_W), F32),
        compiler_params=pltpu.CompilerParams(vmem_limit_bytes=VMEM_LIMIT_BYTES),
    )(rpb2, onehot, mask).reshape(depth * heads, n_dr, GRID_W, GRID_W)
    nk, nq = NA_KROWS * GRID_W, NA_QROWS * GRID_W
    tab = pl.pallas_call(
        functools.partial(_na_table_kernel, layouts=_na_block_layouts(rows)),
        grid=(depth * heads,),
        in_specs=[pl.BlockSpec((1, n_dr, GRID_W, GRID_W), lambda g: (g, 0, 0, 0))],
        out_specs=pl.BlockSpec((1, 3, nk, nq), lambda g: (g, 0, 0, 0)),
        out_shape=jax.ShapeDtypeStruct((depth * heads, 3, nk, nq), F32),
        compiler_params=_params(1),
    )(t1)
    return tab.reshape(depth, heads, 3, nk, nq)


ATTN_CHUNK_ELEMS = 64 * 1024
ATTN_ONES_ROWS = 16
LOG2E = math.log2(math.e)


def _attend_heads(heads, q_of, sets_of, s_scrs):
    row_max = [None] * heads
    outs = [None] * heads

    def score_phase(h):
        q = q_of(h)
        scr = s_scrs[h % 2]
        off, m = 0, None
        for n, k_fn, _, bias_fn in sets_of(h):
            s = _dot_nt(k_fn(), q)
            if bias_fn is not None:
                s = s + bias_fn()
            scr[off:off + n, :] = s
            mj = jnp.max(s, axis=0, keepdims=True)
            m = mj if m is None else jnp.maximum(m, mj)
            off += n
            yield
        row_max[h] = m

    def value_phase(h):
        scr = s_scrs[h % 2]
        m = row_max[h]
        base, acc = 0, None
        for n_set, _, vt_fn, _ in sets_of(h):
            for off, n in _key_chunks(n_set, ATTN_CHUNK_ELEMS // scr.shape[1]):
                p = jnp.exp2(scr[base + off:base + off + n, :] - m).astype(BF16)
                vt = vt_fn(off, n)
                lhs = jnp.concatenate([vt, jnp.ones((ATTN_ONES_ROWS, n), BF16)], axis=0)
                oj = _dot(lhs, p)
                acc = oj if acc is None else acc + oj
                yield
            base += n_set
        dv = acc.shape[0] - ATTN_ONES_ROWS
        outs[h] = acc[:dv] / acc[dv:dv + 1]

    for _ in score_phase(0):
        pass
    for h in range(heads):
        nxt = score_phase(h + 1) if h + 1 < heads else iter(())
        cur = value_phase(h)
        done_n = done_c = False
        while not (done_n and done_c):
            if not done_n:
                done_n = next(nxt, "end") == "end"
            if not done_c:
                done_c = next(cur, "end") == "end"
    return jnp.concatenate(outs, axis=0)


def _key_chunks(total, chunk):
    sizes = [chunk] * (total // chunk)
    if total % chunk:
        sizes.append(total % chunk)
    offs = [sum(sizes[:i]) for i in range(len(sizes))]
    return list(zip(offs, sizes))


def _na_kernel(q_ref, k_ref, vt_ref, kc_ref, vct_ref, b_ref, o_ref, s0_scr, s1_scr, *, rows):
    start = pl.multiple_of(_na_first_key_row(pl.program_id(0), rows) * GRID_W, NA_WIN_ROWS // 2 * GRID_W)
    nk = NA_KROWS * GRID_W
    dh = NA_HEAD_DIM

    def sets_of(h):
        hs = slice(h * dh, (h + 1) * dh)
        window = (nk,
                  lambda: k_ref[0, h, pl.ds(start, nk), :],
                  lambda off, n: vt_ref[0, hs, pl.ds(pl.multiple_of(start + off, 128), n)],
                  lambda: b_ref[0, h, 0])
        context = (kc_ref.shape[2],
                   lambda: kc_ref[0, h],
                   lambda off, n: vct_ref[0, hs, off:off + n],
                   None)
        return [window, context]

    o_t = _attend_heads(NA_HEADS, lambda h: q_ref[0, h], sets_of, (s0_scr, s1_scr))
    o_ref[0] = o_t.T.astype(o_ref.dtype)


def _na_latent(q, k, vt, kc, vct, bias_tab, layer):
    b, heads, s, dh = q.shape
    ctx = kc.shape[2]
    rows = s // GRID_W
    assert rows % NA_QROWS == 0 and rows >= NA_KROWS and (rows - NA_KROWS) % (NA_WIN_ROWS // 2) == 0
    nrb = rows // NA_QROWS
    tq = NA_QROWS * GRID_W

    def bias_index(rb, i):
        return (layer, 0, jnp.where(rb == 0, 0, jnp.where(rb == nrb - 1, 2, 1)), 0, 0)

    full4 = lambda rb, i: (i, 0, 0, 0)
    full3 = lambda rb, i: (i, 0, 0)
    return pl.pallas_call(
        functools.partial(_na_kernel, rows=rows),
        grid=(nrb, b),
        in_specs=[pl.BlockSpec((1, heads, tq, dh), lambda rb, i: (i, 0, rb, 0)),
                  pl.BlockSpec((1, heads, s, dh), full4),
                  pl.BlockSpec((1, heads * dh, s), full3),
                  pl.BlockSpec((1, heads, ctx, dh), full4),
                  pl.BlockSpec((1, heads * dh, ctx), full3),
                  pl.BlockSpec((1, heads, 1, NA_KROWS * GRID_W, tq), bias_index)],
        out_specs=pl.BlockSpec((1, tq, heads * dh), lambda rb, i: (i, rb, 0)),
        out_shape=jax.ShapeDtypeStruct((b, s, heads * dh), BF16),
        scratch_shapes=[pltpu.VMEM((NA_KROWS * GRID_W + ctx, tq), F32)] * 2,
        compiler_params=_params(2),
    )(q, k, vt, kc, vct, bias_tab)


def _attn_kernel(*refs, heads, dv, two_sets):
    if two_sets:
        q_ref, k1_ref, vt1_ref, k2_ref, vt2_ref, o_ref, s0_scr, s1_scr = refs
        key_sets = ((k1_ref, vt1_ref), (k2_ref, vt2_ref))
    else:
        q_ref, k1_ref, vt1_ref, o_ref, s0_scr, s1_scr = refs
        key_sets = ((k1_ref, vt1_ref),)

    def sets_of(h):
        hs = slice(h * dv, (h + 1) * dv)
        return [(k_ref.shape[2],
                 lambda k_ref=k_ref: k_ref[0, h],
                 lambda off, n, vt_ref=vt_ref: vt_ref[0, hs, off:off + n],
                 None) for k_ref, vt_ref in key_sets]

    o_t = _attend_heads(heads, lambda h: q_ref[0, h], sets_of, (s0_scr, s1_scr))
    o_ref[0] = o_t.T.astype(o_ref.dtype)


def _attention(q, k1, vt1, k2=None, vt2=None, tq=512):
    b, heads, t, dq = q.shape
    dv = vt1.shape[1] // heads
    tq = _tile(t, tq)
    two_sets = k2 is not None
    n_keys = k1.shape[2] + (k2.shape[2] if two_sets else 0)
    full4 = lambda i, j: (i, 0, 0, 0)
    full3 = lambda i, j: (i, 0, 0)
    in_specs = [pl.BlockSpec((1, heads, tq, dq), lambda i, j: (i, 0, j, 0)),
                pl.BlockSpec((1,) + k1.shape[1:], full4),
                pl.BlockSpec((1,) + vt1.shape[1:], full3)]
    args = [q, k1, vt1]
    if two_sets:
        in_specs += [pl.BlockSpec((1,) + k2.shape[1:], full4),
                     pl.BlockSpec((1,) + vt2.shape[1:], full3)]
        args += [k2, vt2]
    return pl.pallas_call(
        functools.partial(_attn_kernel, heads=heads, dv=dv, two_sets=two_sets),
        grid=(b, t // tq),
        in_specs=in_specs,
        out_specs=pl.BlockSpec((1, tq, heads * dv), lambda i, j: (i, j, 0)),
        out_shape=jax.ShapeDtypeStruct((b, t, heads * dv), BF16),
        scratch_shapes=[pltpu.VMEM((n_keys, tq), F32)] * 2,
        compiler_params=_params(2),
    )(*args)


def _mla_prep_math(p, gqa, gkva, wq_ref, wkn_ref, wv_ref, g, rope_refs, qo_ref, ko_ref, vto_ref):
    rope = rope_refs is not None
    a, b_ = MLA_Q_RANK, MLA_Q_RANK + MLA_KV_RANK
    cq, ckv = p[:, :a], p[:, a:b_]
    kr, krs = p[:, b_:b_ + MLA_SLOT], p[:, b_ + MLA_SLOT:b_ + 2 * MLA_SLOT]

    def rms(x, g_):
        return x * lax.rsqrt(jnp.mean(x * x, axis=-1, keepdims=True) + EPS) * g_

    cqn = rms(cq, gqa).astype(BF16)
    ckvn = rms(ckv, gkva).astype(BF16)
    qa = _dot(cqn, wq_ref[0])
    kn = _dot(ckvn, wkn_ref[0])
    aq, ak = g[0:1], g[2:3]
    if rope:
        wqs_ref, cos_ref, sin_ref = rope_refs
        qs = _dot(cqn, wqs_ref[0])
        cos_t, sin_t = cos_ref[...], sin_ref[...]
        aq, bq = aq * cos_t, g[1:2] * sin_t
        ak, k_rot = ak * cos_t, krs * (g[3:4] * sin_t)

    def inv_rms(x):
        return lax.rsqrt(jnp.sum(x * x, axis=-1, keepdims=True) * (1.0 / MLA_QK) + EPS)

    for h in range(MLA_HEADS):
        sl = slice(h * MLA_SLOT, (h + 1) * MLA_SLOT)
        xq = qa[:, sl]
        yq = xq * aq
        if rope:
            yq = yq + qs[:, sl] * bq
        qo_ref[0, h] = (yq * (inv_rms(xq) * (MLA_QK ** -0.5 * LOG2E))).astype(BF16)
        xk = kn[:, sl] + kr
        yk = xk * ak
        if rope:
            yk = yk + k_rot
        ko_ref[0, h] = (yk * inv_rms(xk)).astype(BF16)
    vto_ref[0] = _dot_nt(wv_ref[0], ckvn).astype(BF16)


_ROPE_SWAP = tuple(list(range(8, 16)) + list(range(0, 8)) + list(range(24, 32)) + list(range(16, 24)))


def _slot(nope, rope_part):
    lead = (nope if nope is not None else rope_part).shape[:-1]
    dt = (nope if nope is not None else rope_part).dtype
    z = lambda n: jnp.zeros(lead + (n,), dt)
    return jnp.concatenate([nope if nope is not None else z(MLA_NOPE),
                            rope_part if rope_part is not None else z(MLA_ROPE),
                            z(MLA_SLOT - MLA_QK)], axis=-1)


def _rope_tables(s):
    t = jnp.arange(s)
    pos = jnp.stack([t // GRID_W, t % GRID_W], axis=-1).astype(F32)
    inv = ROPE_BASE ** (-jnp.arange(ROPE_FREQS, dtype=F32) / ROPE_FREQS)
    ang = pos[:, :, None] * inv
    cos, sin = jnp.cos(ang), jnp.sin(ang)
    cos_t = jnp.concatenate([cos[:, 0], cos[:, 0], cos[:, 1], cos[:, 1]], axis=-1)
    sin_t = jnp.concatenate([-sin[:, 0], sin[:, 0], -sin[:, 1], sin[:, 1]], axis=-1)
    return _slot(jnp.ones((s, MLA_NOPE), F32), cos_t), _slot(None, sin_t)


def _mla_weights(w_q_up, w_kv_up, g_q, g_k):
    depth = w_q_up.shape[0]
    swap = jnp.array(_ROPE_SWAP)
    wq = w_q_up.reshape(depth, MLA_Q_RANK, MLA_HEADS, MLA_QK)
    wkv = w_kv_up.reshape(depth, MLA_KV_RANK, MLA_HEADS, MLA_NOPE + MLA_V)
    flat = lambda w: w.reshape(depth, w.shape[1], -1).astype(BF16)
    wq_slot = flat(_slot(wq[..., :MLA_NOPE], wq[..., MLA_NOPE:]))
    wqs_slot = flat(_slot(None, wq[..., MLA_NOPE:][..., swap]))
    wkn_slot = flat(_slot(wkv[..., :MLA_NOPE], None))
    wv = jnp.swapaxes(flat(wkv[..., MLA_NOPE:]), 1, 2)
    gains = jnp.stack([_slot(g_q[:, :MLA_NOPE], g_q[:, MLA_NOPE:]), _slot(None, g_q[:, MLA_NOPE:][:, swap]),
                       _slot(g_k[:, :MLA_NOPE], g_k[:, MLA_NOPE:]), _slot(None, g_k[:, MLA_NOPE:][:, swap])],
                      axis=1)
    return wq_slot, wkn_slot, wv, gains, wqs_slot


def _project_kernel(*refs, rope):
    (x_ref, sh_ref, sc_ref, g_ref, w_hy_ref, w_na_ref, w_m_ref, gq_na_ref, gk_na_ref, bd_ref,
     gqa_ref, gkva_ref, wq_ref, wkn_ref, wv_ref, gains_ref) = refs[:16]
    rope_refs = refs[16:19] if rope else None
    o_hy_ref, qn_ref, kn_ref, vtn_ref, qm_ref, km_ref, vtm_ref = refs[-7:]
    h = _adaln(x_ref[0], g_ref[...], sh_ref[0, 0], sc_ref[0, 0]).astype(BF16)
    _mla_prep_math(_dot(h, w_m_ref[0]), gqa_ref[...], gkva_ref[...], wq_ref, wkn_ref, wv_ref, gains_ref[0],
                   rope_refs, qm_ref, km_ref, vtm_ref)
    _na_prep_math(_dot(h, w_na_ref[0]), gq_na_ref[...], gk_na_ref[...], bd_ref[...], qn_ref, kn_ref, vtn_ref)
    o_hy_ref[0] = _dot(h, w_hy_ref[0]).astype(BF16)


def _project(x, mod, g, w_in, w_m, layer, na_g_q, na_g_k, g_qa, g_kva, mla_w, rope_tabs, tm):
    b, t, d = x.shape
    tm = _tile(t, tm)
    n_hy, n_na, n_m = 3 * HY_DIM, 3 * NA_DIM, w_m.shape[2]
    assert n_hy == n_na
    wq_slot, wkn_slot, wv, gains, wqs_slot = mla_w
    rope = rope_tabs is not None
    gq = jnp.tile(na_g_q, NA_HEADS)[None, :]
    gk = jnp.tile(na_g_k, NA_HEADS)[None, :]
    head = jnp.arange(NA_DIM) // NA_HEAD_DIM
    bd = (head[:, None] == head[None, :]).astype(BF16)
    const = lambda i, j: (0, 0)
    per_layer = lambda w: pl.BlockSpec((1,) + w.shape[1:], lambda i, j: (layer, 0, 0))
    in_specs = [pl.BlockSpec((1, tm, d), lambda i, j: (i, j, 0)),
                pl.BlockSpec((1, 1, 1, d), lambda i, j: (i, 0, 0, 0)),
                pl.BlockSpec((1, 1, 1, d), lambda i, j: (i, 1, 0, 0)),
                pl.BlockSpec((1, d), const),
                _resident((1, d, n_hy), lambda i, j: (layer, 0, 0)),
                _resident((1, d, n_na), lambda i, j: (layer, 0, 1)),
                _resident((1, d, n_m), lambda i, j: (layer, 0, 0)),
                pl.BlockSpec((1, NA_DIM), const), pl.BlockSpec((1, NA_DIM), const),
                pl.BlockSpec((NA_DIM, NA_DIM), const),
                pl.BlockSpec((1, MLA_Q_RANK), const), pl.BlockSpec((1, MLA_KV_RANK), const),
                per_layer(wq_slot), per_layer(wkn_slot), per_layer(wv), per_layer(gains)]
    args = [x, mod, mod, g, w_in, w_in, w_m, gq, gk, bd, g_qa[None, :], g_kva[None, :],
            wq_slot, wkn_slot, wv, gains]
    if rope:
        in_specs += [per_layer(wqs_slot),
                     pl.BlockSpec((tm, MLA_SLOT), lambda i, j: (j, 0)),
                     pl.BlockSpec((tm, MLA_SLOT), lambda i, j: (j, 0))]
        args += [wqs_slot] + list(rope_tabs)
    tok_major = lambda heads, width: (jax.ShapeDtypeStruct((b, heads, t, width), BF16),
                                      pl.BlockSpec((1, heads, tm, width), lambda i, j: (i, 0, j, 0)))
    transposed = lambda rows: (jax.ShapeDtypeStruct((b, rows, t), BF16),
                               pl.BlockSpec((1, rows, tm), lambda i, j: (i, 0, j)))
    outs = [(jax.ShapeDtypeStruct((b, t, n_hy), BF16), pl.BlockSpec((1, tm, n_hy), lambda i, j: (i, j, 0))),
            tok_major(NA_HEADS, NA_HEAD_DIM), tok_major(NA_HEADS, NA_HEAD_DIM), transposed(NA_DIM),
            tok_major(MLA_HEADS, MLA_SLOT), tok_major(MLA_HEADS, MLA_SLOT), transposed(MLA_HEADS * MLA_V)]
    return pl.pallas_call(
        functools.partial(_project_kernel, rope=rope),
        grid=(b, t // tm),
        in_specs=in_specs,
        out_specs=[spec for _, spec in outs],
        out_shape=[shape for shape, _ in outs],
        compiler_params=_params(2),
    )(*args)


def _mix_mlp_kernel(hy_ref, na_ref, mla_ref, x_ref, gate1_ref, sh_ref, sc_ref, gate2_ref, g_ref,
                    w_hy_ref, w_na_ref, w_mla_ref, w1_ref, b1_ref, w2_ref, b2_ref, o_ref, *, chunk):
    mix = (_dot(hy_ref[0], w_hy_ref[0]) + _dot(na_ref[0], w_na_ref[0]) + _dot(mla_ref[0], w_mla_ref[0]))
    x = x_ref[0] + gate1_ref[0, 0] * mix
    h = _adaln(x, g_ref[...], sh_ref[0, 0], sc_ref[0, 0]).astype(BF16)
    d_ff = w1_ref.shape[2]
    acc = jnp.zeros(x.shape, F32)
    for c0 in range(0, d_ff, chunk):
        a = jnp.maximum(_dot(h, w1_ref[0, :, c0:c0 + chunk]) + b1_ref[:, c0:c0 + chunk], 0.0)
        acc = acc + _dot((a * a).astype(BF16), w2_ref[0, c0:c0 + chunk, :])
    o_ref[0] = x + gate2_ref[0, 0] * (acc + b2_ref[...])


def _mix_mlp(o_hy, o_na, o_mla, x, mod, w_out, g, w1, b1, w2, b2, layer, tm):
    b, t, d = x.shape
    d_ff = w1.shape[2]
    n_hy, n_na, n_mla = o_hy.shape[-1], o_na.shape[-1], o_mla.shape[-1]
    assert n_hy == n_na and n_mla == n_hy + n_na
    tm = _tile(t, tm)
    tok = lambda n: pl.BlockSpec((1, tm, n), lambda i, j: (i, j, 0))
    modspec = lambda k: pl.BlockSpec((1, 1, 1, d), lambda i, j: (i, k, 0, 0))
    const = lambda i, j: (0, 0)
    lay = lambda i, j: (layer, 0, 0)
    return pl.pallas_call(
        functools.partial(_mix_mlp_kernel, chunk=_tile(d_ff, 1024)),
        grid=(b, t // tm),
        in_specs=[tok(n_hy), tok(n_na), tok(n_mla), tok(d),
                  modspec(2), modspec(3), modspec(4), modspec(5),
                  pl.BlockSpec((1, d), const),
                  _resident((1, n_hy, d), lay),
                  _resident((1, n_na, d), lambda i, j: (layer, 1, 0)),
                  _resident((1, n_mla, d), lambda i, j: (layer, 1, 0)),
                  _resident((1, d, d_ff), lay), pl.BlockSpec((1, d_ff), const),
                  _resident((1, d_ff, d), lay), pl.BlockSpec((1, d), const)],
        out_specs=tok(d),
        out_shape=jax.ShapeDtypeStruct((b, t, d), F32),
        compiler_params=_params(2),
    )(o_hy, o_na, o_mla, x, mod, mod, mod, mod, g, w_out, w_out, w_out, w1, b1, w2, b2)


def kernel(x, c, ctx, c_ctx, w_mod, b_mod, g_norm1, w_in, hy_conv_w, hy_conv_b, hy_f_w1, hy_f_b1, hy_f_w2, hy_f_b2, hy_f_w3, hy_f_b3, hy_freq, hy_bias, na_g_q, na_g_k, na_rpb, mla_g_qa, mla_g_kva, mla_w_q_up, mla_w_kv_up, mla_g_q, mla_g_k, w_out, g_norm2, w_ff1, b_ff1, w_ff2, b_ff2):
    b, s, d = x.shape
    lc = ctx.shape[1]
    depth = w_mod.shape[0]
    in_hn = 3 * HY_DIM + 3 * NA_DIM
    swap = jnp.array(_ROPE_SWAP)

    n_cond = -(-(b + 1) // 8) * 8
    cond = jnp.zeros((n_cond, d), F32).at[:b].set(c).at[b].set(c_ctx)
    mods = _modulation(cond, w_mod, b_mod).reshape(depth, n_cond, 6, 1, d)

    rope_tabs = _rope_tables(s)
    cs_x = _dft_matrices(s)
    cs_c = _dft_matrices(lc)
    filt = (hy_f_w1, hy_f_b1, hy_f_w2, hy_f_b2, hy_f_w3, hy_f_b3, hy_freq)
    spec_x = _hyena_spectra(s, cs_x, *filt)
    spec_c = _hyena_spectra(lc, cs_c, *filt)
    bias_tab = _na_bias_table(na_rpb, s // GRID_W)
    hy_bias = hy_bias.reshape(2 * depth, 1, HY_DIM)

    w_in_b, w_out_b = w_in.astype(BF16), w_out.astype(BF16)
    w_ff1_b, w_ff2_b = w_ff1.astype(BF16), w_ff2.astype(BF16)
    mla_w = _mla_weights(mla_w_q_up, mla_w_kv_up, mla_g_q, mla_g_k)
    w_kr = w_in[:, :, in_hn + MLA_Q_RANK + MLA_KV_RANK:]
    w_m_b = jnp.concatenate([w_in[:, :, in_hn:in_hn + MLA_Q_RANK + MLA_KV_RANK], _slot(None, w_kr),
                             _slot(None, w_kr[:, :, swap])], axis=-1).astype(BF16)

    cx = ctx
    for i in range(depth):
        last = i == depth - 1
        mod_x = mods[i, :b]
        mod_c = jnp.broadcast_to(mods[i, b], (b, 6, 1, d))
        g1 = g_norm1[i][None, :]
        prep = (i, na_g_q[i], na_g_k[i], mla_g_qa[i], mla_g_kva[i], mla_w)
        px_hy, qx, kx, vtx, mqx, mkx, mvx = _project(x, mod_x, g1, w_in_b, w_m_b, *prep, rope_tabs,
                                                     PROJECT_TOKEN_TILE)
        pc_hy, qc, kc, vtc, mqc, mkc, mvc = _project(cx, mod_c, g1, w_in_b, w_m_b, *prep, None,
                                                     PROJECT_TOKEN_TILE)

        o_na = _na_latent(qx, kx, vtx, kc, vtc, bias_tab, i)
        o_mla = _attention(mqx, mkx, mvx, mkc, mvc)

        o_hy = _hyena(px_hy, hy_conv_w[i], hy_conv_b[i][None, :], cs_x, *spec_x, hy_bias, i)

        g2 = g_norm2[i][None, :]
        b1, b2 = b_ff1[i][None, :], b_ff2[i][None, :]
        x = _mix_mlp(o_hy, o_na, o_mla, x, mod_x, w_out_b, g2, w_ff1_b, b1, w_ff2_b, b2, i, MLP_TOKEN_TILE)

        if not last:
            oc_hy = _hyena(pc_hy, hy_conv_w[i], hy_conv_b[i][None, :], cs_c, *spec_c, hy_bias, i)
            oc_na = _attention(qc, kc, vtc)
            oc_mla = _attention(mqc, mkc, mvc)
            cx = _mix_mlp(oc_hy, oc_na, oc_mla, cx, mod_c, w_out_b, g2, w_ff1_b, b1, w_ff2_b, b2, i,
                          MLP_TOKEN_TILE)
    return x
```

```python
import functools
import math

import jax
import jax.numpy as jnp
from jax import lax
from jax.experimental import pallas as pl
from jax.experimental.pallas import tpu as pltpu

F32 = jnp.float32
BF16 = jnp.bfloat16
HIGHEST = lax.Precision.HIGHEST

EPS = 1e-6
NEG_INF = -1e9
GRID_W = 64

HY_DIM = 256
HY_BANDS = 8
HY_DECAY_TARGET = 1e-2
HY_FAST = 0.3
HY_SLOW = 1.5

NA_HEADS = 4
NA_HEAD_DIM = 64
NA_DIM = NA_HEADS * NA_HEAD_DIM
NA_WIN_ROWS = 8
NA_WIN_COLS = 16

MLA_HEADS = 8
MLA_Q_RANK = 256
MLA_KV_RANK = 128
MLA_NOPE = 64
MLA_ROPE = 32
MLA_V = 64
MLA_QK = MLA_NOPE + MLA_ROPE
MLA_SLOT = 128
ROPE_BASE = 10000.0
ROPE_FREQS = MLA_ROPE // 4

VMEM_LIMIT_BYTES = 56 * 1024 * 1024
HYENA_VMEM_LIMIT_BYTES = 60 * 1024 * 1024
MLA_QUERY_BLOCKS = 4
PROJECT_TOKEN_TILE = 1024
MLP_TOKEN_TILE = 1024

def _params(n_grid_dims):
    return pltpu.CompilerParams(dimension_semantics=("arbitrary",) * n_grid_dims,
                                vmem_limit_bytes=VMEM_LIMIT_BYTES)


def _tile(total, preferred):
    t = min(total, preferred)
    while total % t:
        t //= 2
    return t


def _resident(shape, index_map):
    return pl.BlockSpec(shape, index_map, pipeline_mode=pl.Buffered(1))


def _dot(a, b):
    return jnp.dot(a, b, preferred_element_type=F32)


def _split_bf16(x):
    hi = x.astype(BF16)
    return hi, (x - hi.astype(F32)).astype(BF16)


def _dot_3x(a, b):
    a_hi, a_lo = _split_bf16(a)
    b_hi, b_lo = _split_bf16(b)
    return _dot(a_hi, b_hi) + (_dot(a_hi, b_lo) + _dot(a_lo, b_hi))


def _dot_hi(a, b):
    return jnp.dot(a, b, preferred_element_type=F32, precision=HIGHEST)


def _dot_nt(a, b):
    return lax.dot_general(a, b, (((1,), (1,)), ((), ())), preferred_element_type=F32)


def _mod_kernel(cond_ref, w_ref, b_ref, o_ref):
    a = cond_ref[...]
    a = a / (1.0 + jnp.exp(-a))
    n = a.shape[0]
    a_hi, a_lo = _split_bf16(a)
    w_hi, w_lo = _split_bf16(w_ref[0])
    y = _dot(jnp.concatenate([a_hi, a_lo], axis=0), w_hi)
    o_ref[0] = y[:n] + y[n:] + _dot(a_hi, w_lo) + b_ref[0]


def _modulation(cond, w_mod, b_mod):
    depth, d, d6 = w_mod.shape
    n = cond.shape[0]
    tn = _tile(d6, 1536)
    return pl.pallas_call(
        _mod_kernel,
        grid=(depth, d6 // tn),
        in_specs=[pl.BlockSpec((n, d), lambda l, j: (0, 0)),
                  pl.BlockSpec((1, d, tn), lambda l, j: (l, 0, j)),
                  pl.BlockSpec((1, 1, tn), lambda l, j: (l, 0, j))],
        out_specs=pl.BlockSpec((1, n, tn), lambda l, j: (l, 0, j)),
        out_shape=jax.ShapeDtypeStruct((depth, n, d6), F32),
        compiler_params=_params(2),
    )(cond, w_mod, b_mod.reshape(depth, 1, d6))


def _adaln(x, g, shift, scale):
    y = x * lax.rsqrt(jnp.mean(x * x, axis=-1, keepdims=True) + EPS) * g
    return y * (1.0 + scale) + shift


def _dft_matrices(length):
    n = 2 * length
    blk = min(128, length)
    n_blk = length // blk
    f = jnp.arange(length, dtype=jnp.int32)[:, None]
    t0 = (blk * jnp.arange(n_blk, dtype=jnp.int32))[None, :]
    dt = jnp.arange(blk, dtype=jnp.int32)[None, :]
    a = ((f * t0) % n).astype(F32) * (2.0 * math.pi / n)
    b = ((f * dt) % n).astype(F32) * (2.0 * math.pi / n)
    rows = _tile(length, 256)

    def combine(ca_ref, sa_ref, cb_ref, sb_ref, c_ref, s_ref):
        cb, sb = cb_ref[...], sb_ref[...]
        for j in range(n_blk):
            ca, sa = ca_ref[:, j:j + 1], sa_ref[:, j:j + 1]
            c_ref[:, j * blk:(j + 1) * blk] = (ca * cb - sa * sb).astype(BF16)
            s_ref[:, j * blk:(j + 1) * blk] = (sa * cb + ca * sb).astype(BF16)

    coarse = pl.BlockSpec((rows, n_blk), lambda i: (i, 0))
    fine = pl.BlockSpec((rows, blk), lambda i: (i, 0))
    full = pl.BlockSpec((rows, length), lambda i: (i, 0))
    out = jax.ShapeDtypeStruct((length, length), BF16)
    return pl.pallas_call(
        combine,
        grid=(length // rows,),
        in_specs=[coarse, coarse, fine, fine],
        out_specs=[full, full],
        out_shape=[out, out],
        compiler_params=_params(1),
    )(jnp.cos(a), jnp.sin(a), jnp.cos(b), jnp.sin(b))


def _filt_kernel(z_ref, dec_ref, w1_ref, b1_ref, w2_ref, b2_ref, w3_ref, b3_ref, fr_ref, hsum_ref, hdiff_ref):
    z = z_ref[...]
    fr = fr_ref[0]
    a = jnp.sin(fr[0:1] * (_dot_3x(z, w1_ref[0]) + b1_ref[0]))
    a = jnp.sin(fr[1:2] * (_dot_3x(a, w2_ref[0]) + b2_ref[0]))
    h = _dot_3x(a, w3_ref[0]) + b3_ref[0]
    dec = dec_ref[...]
    c = HY_DIM
    row0 = lax.broadcasted_iota(jnp.int32, dec.shape, 0) == 0
    for o in range(2):
        hf = h[:, (2 * o) * c:(2 * o + 1) * c] * dec
        hb = h[:, (2 * o + 1) * c:(2 * o + 2) * c] * dec
        nrm = (jnp.sum(jnp.abs(hf), axis=0, keepdims=True)
               + jnp.sum(jnp.abs(hb), axis=0, keepdims=True) + EPS)
        hf = hf / nrm
        hb = jnp.where(row0, 0.0, hb / nrm)
        hsum_ref[0, :, o * c:(o + 1) * c] = hf + hb
        hdiff_ref[0, :, o * c:(o + 1) * c] = hf - hb


def _alternating_sign(shape):
    row = lax.broadcasted_iota(jnp.int32, shape, 0)
    return (1 - 2 * (row % 2)).astype(F32)


def _spec_kernel(hsum_ref, hdiff_ref, c_ref, s_ref, hc_ref, hs_ref, hn_ref):
    length = hsum_ref.shape[1]
    hsum = hsum_ref[0]
    hdiff = hdiff_ref[0]

    row = lax.broadcasted_iota(jnp.int32, hsum.shape, 0)
    wf = jnp.where(row == 0, 0.5 / length, 1.0 / length)
    hc_ref[0] = _dot(c_ref[...], hsum.astype(BF16)) * wf
    hs_ref[0] = _dot(s_ref[...], hdiff.astype(BF16)) * wf
    hn_ref[0] = jnp.sum(_alternating_sign(hsum.shape) * hsum, axis=0, keepdims=True) * (0.5 / length)


def _hyena_spectra(length, cs, f_w1, f_b1, f_w2, f_b2, f_w3, f_b3, freq):
    depth = f_w1.shape[0]
    pos_dim = f_w1.shape[1]
    hid = f_w1.shape[2]
    kpad = 32
    t = jnp.linspace(0.0, 1.0, length, dtype=F32)[:, None]
    w = (2.0 * math.pi / length) * jnp.arange(length, dtype=F32)[:, None]
    bands = jnp.linspace(1e-4, HY_BANDS - 1, HY_BANDS, dtype=F32)
    z = jnp.concatenate([t, jnp.cos(bands * w), -jnp.sin(bands * w)], axis=-1)
    z = jnp.pad(z, ((0, 0), (0, kpad - pos_dim)))
    w1 = jnp.pad(f_w1, ((0, 0), (0, kpad - pos_dim), (0, 0)))
    deltas = jnp.linspace(math.log(HY_DECAY_TARGET) / HY_SLOW, math.log(HY_DECAY_TARGET) / HY_FAST,
                          HY_DIM, dtype=F32)
    decay = jnp.exp(-t * jnp.abs(deltas))
    c2 = 2 * HY_DIM
    c4 = 4 * HY_DIM
    lay3 = lambda l: (l, 0, 0)
    hsum, hdiff = pl.pallas_call(
        _filt_kernel,
        grid=(depth,),
        in_specs=[pl.BlockSpec((length, kpad), lambda l: (0, 0)),
                  pl.BlockSpec((length, HY_DIM), lambda l: (0, 0)),
                  pl.BlockSpec((1, kpad, hid), lay3),
                  pl.BlockSpec((1, 1, hid), lay3),
                  pl.BlockSpec((1, hid, hid), lay3),
                  pl.BlockSpec((1, 1, hid), lay3),
                  pl.BlockSpec((1, hid, c4), lay3),
                  pl.BlockSpec((1, 1, c4), lay3),
                  pl.BlockSpec((1, 2, hid), lay3)],
        out_specs=[pl.BlockSpec((1, length, c2), lay3),
                   pl.BlockSpec((1, length, c2), lay3)],
        out_shape=[jax.ShapeDtypeStruct((depth, length, c2), F32),
                   jax.ShapeDtypeStruct((depth, length, c2), F32)],
        compiler_params=_params(1),
    )(z, decay, w1, f_b1.reshape(depth, 1, hid), f_w2, f_b2.reshape(depth, 1, hid),
      f_w3, f_b3.reshape(depth, 1, c4), freq)
    blk_in = pl.BlockSpec((1, length, c2), lay3, pipeline_mode=pl.Buffered(1))
    blk_out = pl.BlockSpec((1, length, c2), lay3)
    return pl.pallas_call(
        _spec_kernel,
        grid=(depth,),
        in_specs=[blk_in, blk_in, _resident((length, length), lambda l: (0, 0)),
                  _resident((length, length), lambda l: (0, 0))],
        out_specs=[blk_out, blk_out, pl.BlockSpec((1, 1, c2), lay3)],
        out_shape=[jax.ShapeDtypeStruct((depth, length, c2), F32),
                   jax.ShapeDtypeStruct((depth, length, c2), F32),
                   jax.ShapeDtypeStruct((depth, 1, c2), F32)],
        compiler_params=pltpu.CompilerParams(dimension_semantics=("arbitrary",),
                                             vmem_limit_bytes=HYENA_VMEM_LIMIT_BYTES),
    )(hsum, hdiff, *cs)


HY_ROWS = 256
HY_HALO = 8


HY_PAIR = 2


def _hyena_kernel(pv_ref, pg_ref, wv_ref, bv_ref, wg_ref, bg_ref, c_ref, s_ref, hc_ref, hs_ref, hn_ref, bias_ref,
                  o_ref, z_scr, xc_scr, xs_scr, yc_scr, ys_scr):
    length = pv_ref.shape[1]
    c = HY_DIM
    rows = min(HY_ROWS, length)
    chunks = [(r0, rows) for r0 in range(0, length, rows)]
    halves = [(i, slice(i * c, (i + 1) * c)) for i in range(HY_PAIR)]

    def short_conv_rows(p_ref, w_ref, b_ref, i, r0):
        lo = max(r0 - HY_HALO, 0)
        hi = min(r0 + rows + HY_HALO, length)
        u = p_ref[i, lo:hi, :].astype(F32)
        row = lax.broadcasted_iota(jnp.int32, u.shape, 0) + lo
        prev = jnp.where(row == 0, 0.0, pltpu.roll(u, 1, 0))
        nxt = jnp.where(row == length - 1, 0.0, pltpu.roll(u, hi - lo - 1, 0))
        y = prev * w_ref[0:1, :] + u * w_ref[1:2, :] + nxt * w_ref[2:3, :] + b_ref[...]
        return y[r0 - lo:r0 - lo + rows]

    @pl.when(pl.program_id(1) == 0)
    def _():
        for r0, n in chunks:
            for i, lanes in halves:
                z_scr[r0:r0 + n, lanes] = short_conv_rows(pv_ref, wv_ref, bv_ref, i, r0)

    sign = _alternating_sign((rows, c))
    zb = z_scr[...].astype(BF16)
    xc_scr[...] = _dot(c_ref[...], zb)
    xs_scr[...] = _dot(s_ref[...], zb)
    xn = [jnp.zeros((1, c), F32) for _ in halves]
    for r0, n in chunks:
        hc, hs = hc_ref[0, r0:r0 + n, :], hs_ref[0, r0:r0 + n, :]
        for i, lanes in halves:
            xc, xs = xc_scr[r0:r0 + n, lanes], xs_scr[r0:r0 + n, lanes]
            yc_scr[r0:r0 + n, lanes] = (xc * hc - xs * hs).astype(BF16)
            ys_scr[r0:r0 + n, lanes] = (xc * hs + xs * hc).astype(BF16)
            xn[i] = xn[i] + jnp.sum(sign * z_scr[r0:r0 + n, lanes], axis=0, keepdims=True)
    xc_scr[...] = _dot(c_ref[...], yc_scr[...])
    xs_scr[...] = _dot(s_ref[...], ys_scr[...])
    bias = bias_ref[0]
    for r0, n in chunks:
        for i, lanes in halves:
            z = z_scr[r0:r0 + n, lanes]
            conv = xc_scr[r0:r0 + n, lanes] + xs_scr[r0:r0 + n, lanes] + sign * (xn[i] * hn_ref[0])
            z = short_conv_rows(pg_ref, wg_ref, bg_ref, i, r0) * (conv + z * bias)
            z_scr[r0:r0 + n, lanes] = z
            o_ref[i, r0:r0 + n, :] = z.astype(o_ref.dtype)


def _hyena(p_hy, conv_w, conv_b, cs, hc, hs, hn, bias, layer):
    b, length, _ = p_hy.shape
    assert b % HY_PAIR == 0
    c = HY_DIM
    wide = HY_PAIR * c
    return pl.pallas_call(
        _hyena_kernel,
        grid=(b // HY_PAIR, 2),
        in_specs=[pl.BlockSpec((HY_PAIR, length, c), lambda i, o: (i, 0, 0), pipeline_mode=pl.Buffered(1)),
                  pl.BlockSpec((HY_PAIR, length, c), lambda i, o: (i, 0, o + 1)),
                  pl.BlockSpec((3, c), lambda i, o: (0, 0)),
                  pl.BlockSpec((1, c), lambda i, o: (0, 0)),
                  pl.BlockSpec((3, c), lambda i, o: (0, o + 1)),
                  pl.BlockSpec((1, c), lambda i, o: (0, o + 1)),
                  _resident((length, length), lambda i, o: (0, 0)),
                  _resident((length, length), lambda i, o: (0, 0)),
                  pl.BlockSpec((1, length, c), lambda i, o: (layer, 0, o), pipeline_mode=pl.Buffered(1)),
                  pl.BlockSpec((1, length, c), lambda i, o: (layer, 0, o), pipeline_mode=pl.Buffered(1)),
                  pl.BlockSpec((1, 1, c), lambda i, o: (layer, 0, o)),
                  pl.BlockSpec((1, 1, c), lambda i, o: (2 * layer + o, 0, 0))],
        out_specs=pl.BlockSpec((HY_PAIR, length, c), lambda i, o: (i, 0, 0)),
        out_shape=jax.ShapeDtypeStruct((b, length, c), BF16),
        scratch_shapes=[pltpu.VMEM((length, wide), F32), pltpu.VMEM((length, wide), F32),
                        pltpu.VMEM((length, wide), F32), pltpu.VMEM((length, wide), BF16),
                        pltpu.VMEM((length, wide), BF16)],
        compiler_params=pltpu.CompilerParams(dimension_semantics=("arbitrary", "arbitrary"),
                                             vmem_limit_bytes=HYENA_VMEM_LIMIT_BYTES),
    )(p_hy, p_hy, conv_w, conv_b, conv_w, conv_b, *cs, hc, hs, hn, bias)


def _na_prep_math(p, gq, gk, bd, qo_ref, ko_ref, vo_ref):
    def head_rms(x, g):
        sq = x * x
        hi = sq.astype(BF16)
        lo = (sq - hi.astype(F32)).astype(BF16)
        ms = (_dot(hi, bd) + _dot(lo, bd)) * (1.0 / NA_HEAD_DIM)
        return x * lax.rsqrt(ms + EPS) * g

    q = (head_rms(p[:, :NA_DIM], gq) * (NA_HEAD_DIM ** -0.5 * LOG2E)).astype(BF16)
    k = head_rms(p[:, NA_DIM:2 * NA_DIM], gk).astype(BF16)
    for h in range(NA_HEADS):
        sl = slice(h * NA_HEAD_DIM, (h + 1) * NA_HEAD_DIM)
        qo_ref[0, h] = q[:, sl]
        ko_ref[0, h] = k[:, sl]
    vo_ref[0] = p[:, 2 * NA_DIM:].T.astype(BF16)


NA_QROWS = 8
NA_KROWS = NA_QROWS + NA_WIN_ROWS
NA_BATCH_PAIR = 4


def _rpb_expand_kernel(r_ref, oh_ref, m_ref, o_ref):
    o_ref[...] = (_dot_hi(r_ref[...], oh_ref[...]) + m_ref[...]) * LOG2E


def _na_first_key_row(rb, rows):
    return jnp.clip(NA_QROWS * rb - NA_WIN_ROWS // 2, 0, rows - NA_KROWS)


def _na_block_layouts(rows):
    nrb = rows // NA_QROWS

    def layout(rb):
        w0 = min(max(NA_QROWS * rb - NA_WIN_ROWS // 2, 0), rows - NA_KROWS)
        out = []
        for j in range(NA_QROWS):
            r = NA_QROWS * rb + j
            row0 = min(max(r - NA_WIN_ROWS // 2, 0), rows - NA_WIN_ROWS)
            assert w0 <= row0 and row0 + NA_WIN_ROWS <= w0 + NA_KROWS
            out.append((r - w0, row0 - w0))
        return tuple(out)

    assert all(layout(rb) == layout(1) for rb in range(1, nrb - 1))
    return [layout(0), layout(min(1, nrb - 1)), layout(nrb - 1)]


def _na_table_kernel(t_ref, o_ref, *, layouts):
    neg = jnp.full((GRID_W, GRID_W), NEG_INF, F32)
    for pos, layout in enumerate(layouts):
        for i in range(NA_KROWS):
            pieces = [t_ref[0, i - r_rel + NA_WIN_ROWS - 1] if row0_rel <= i < row0_rel + NA_WIN_ROWS else neg
                      for r_rel, row0_rel in layout]
            o_ref[0, pos, i * GRID_W:(i + 1) * GRID_W, :] = jnp.concatenate(pieces, axis=1)


def _na_bias_table(rpb, rows):
    depth, heads, n_dr, n_dc = rpb.shape
    kc = jnp.arange(GRID_W)[:, None]
    qc = jnp.arange(GRID_W)[None, :]
    dc = jnp.clip(kc - qc + NA_WIN_COLS - 1, 0, n_dc - 1)
    cstart = jnp.clip(qc - NA_WIN_COLS // 2, 0, GRID_W - NA_WIN_COLS)
    valid = ((kc >= cstart) & (kc < cstart + NA_WIN_COLS)).reshape(1, GRID_W * GRID_W)
    n_pad = 32
    onehot = ((dc.reshape(1, -1) == jnp.arange(n_pad)[:, None]) & valid).astype(F32)
    mask = jnp.where(valid, 0.0, NEG_INF).astype(F32)
    rows = depth * heads * n_dr
    rpb2 = jnp.pad(rpb.reshape(rows, n_dc).astype(F32), ((0, 0), (0, n_pad - n_dc)))
    t1 = pl.pallas_call(
        _rpb_expand_kernel,
        out_shape=jax.ShapeDtypeStruct((rows, GRID_W * GRID_W), F32),
        compiler_params=pltpu.CompilerParams(vmem_limit_bytes=VMEM_LIMIT_BYTES),
    )(rpb2, onehot, mask).reshape(depth * heads, n_dr, GRID_W, GRID_W)
    nk, nq = NA_KROWS * GRID_W, NA_QROWS * GRID_W
    tab = pl.pallas_call(
        functools.partial(_na_table_kernel, layouts=_na_block_layouts(rows)),
        grid=(depth * heads,),
        in_specs=[pl.BlockSpec((1, n_dr, GRID_W, GRID_W), lambda g: (g, 0, 0, 0))],
        out_specs=pl.BlockSpec((1, 3, nk, nq), lambda g: (g, 0, 0, 0)),
        out_shape=jax.ShapeDtypeStruct((depth * heads, 3, nk, nq), F32),
        compiler_params=_params(1),
    )(t1)
    return tab.reshape(depth, heads, 3, nk, nq)


ATTN_CHUNK_ELEMS = 64 * 1024
ATTN_ONES_ROWS = 16
LOG2E = math.log2(math.e)


def _attend_heads(heads, q_of, sets_of, s_scrs):
    row_max = [None] * heads
    outs = [None] * heads

    def score_phase(h):
        q = q_of(h)
        scr = s_scrs[h % 2]
        off, m = 0, None
        for n, k_fn, _, bias_fn in sets_of(h):
            s = _dot_nt(k_fn(), q)
            if bias_fn is not None:
                s = s + bias_fn()
            scr[off:off + n, :] = s
            mj = jnp.max(s, axis=0, keepdims=True)
            m = mj if m is None else jnp.maximum(m, mj)
            off += n
            yield
        row_max[h] = m

    def value_phase(h):
        scr = s_scrs[h % 2]
        m = row_max[h]
        base, acc = 0, None
        for n_set, _, vt_fn, _ in sets_of(h):
            for off, n in _key_chunks(n_set, ATTN_CHUNK_ELEMS // scr.shape[1]):
                p = jnp.exp2(scr[base + off:base + off + n, :] - m).astype(BF16)
                vt = vt_fn(off, n)
                lhs = jnp.concatenate([vt, jnp.ones((ATTN_ONES_ROWS, n), BF16)], axis=0)
                oj = _dot(lhs, p)
                acc = oj if acc is None else acc + oj
                yield
            base += n_set
        dv = acc.shape[0] - ATTN_ONES_ROWS
        outs[h] = acc[:dv] / acc[dv:dv + 1]

    for _ in score_phase(0):
        pass
    for h in range(heads):
        nxt = score_phase(h + 1) if h + 1 < heads else iter(())
        cur = value_phase(h)
        done_n = done_c = False
        while not (done_n and done_c):
            if not done_n:
                done_n = next(nxt, "end") == "end"
            if not done_c:
                done_c = next(cur, "end") == "end"
    return jnp.concatenate(outs, axis=0)


def _key_chunks(total, chunk):
    sizes = [chunk] * (total // chunk)
    if total % chunk:
        sizes.append(total % chunk)
    offs = [sum(sizes[:i]) for i in range(len(sizes))]
    return list(zip(offs, sizes))


def _na_kernel(q_ref, k_ref, vt_ref, kc_ref, vct_ref, b_ref, o_ref, s0_scr, s1_scr, *, rows):
    start = pl.multiple_of(_na_first_key_row(pl.program_id(0), rows) * GRID_W, NA_WIN_ROWS // 2 * GRID_W)
    nk = NA_KROWS * GRID_W
    dh = NA_HEAD_DIM

    def sets_of(u):
        i, h = divmod(u, NA_HEADS)
        hs = slice(h * dh, (h + 1) * dh)
        window = (nk,
                  lambda: k_ref[i, h, pl.ds(start, nk), :],
                  lambda off, n: vt_ref[i, hs, pl.ds(pl.multiple_of(start + off, 128), n)],
                  lambda: b_ref[0, h, 0])
        context = (kc_ref.shape[2],
                   lambda: kc_ref[i, h],
                   lambda off, n: vct_ref[i, hs, off:off + n],
                   None)
        return [window, context]

    n_pair = q_ref.shape[0]
    o_t = _attend_heads(n_pair * NA_HEADS, lambda u: q_ref[u // NA_HEADS, u % NA_HEADS], sets_of,
                        (s0_scr, s1_scr))
    for i in range(n_pair):
        o_ref[i] = o_t[i * NA_DIM:(i + 1) * NA_DIM].T.astype(o_ref.dtype)


def _na_latent(q, k, vt, kc, vct, bias_tab, layer):
    b, heads, s, dh = q.shape
    ctx = kc.shape[2]
    rows = s // GRID_W
    assert rows % NA_QROWS == 0 and rows >= NA_KROWS and (rows - NA_KROWS) % (NA_WIN_ROWS // 2) == 0
    nrb = rows // NA_QROWS
    tq = NA_QROWS * GRID_W

    def bias_index(rb, i):
        return (layer, 0, jnp.where(rb == 0, 0, jnp.where(rb == nrb - 1, 2, 1)), 0, 0)

    pair = NA_BATCH_PAIR if b % NA_BATCH_PAIR == 0 else 1
    full4 = lambda rb, i: (i, 0, 0, 0)
    full3 = lambda rb, i: (i, 0, 0)
    return pl.pallas_call(
        functools.partial(_na_kernel, rows=rows),
        grid=(nrb, b // pair),
        in_specs=[pl.BlockSpec((pair, heads, tq, dh), lambda rb, i: (i, 0, rb, 0)),
                  pl.BlockSpec((pair, heads, s, dh), full4),
                  pl.BlockSpec((pair, heads * dh, s), full3),
                  pl.BlockSpec((pair, heads, ctx, dh), full4),
                  pl.BlockSpec((pair, heads * dh, ctx), full3),
                  pl.BlockSpec((1, heads, 1, NA_KROWS * GRID_W, tq), bias_index)],
        out_specs=pl.BlockSpec((pair, tq, heads * dh), lambda rb, i: (i, rb, 0)),
        out_shape=jax.ShapeDtypeStruct((b, s, heads * dh), BF16),
        scratch_shapes=[pltpu.VMEM((NA_KROWS * GRID_W + ctx, tq), F32)] * 2,
        compiler_params=_params(2),
    )(q, k, vt, kc, vct, bias_tab)


def _attn_kernel(*refs, heads, dv, two_sets, tq):
    if two_sets:
        q_ref, k1_ref, vt1_ref, k2_ref, vt2_ref, o_ref, s0_scr, s1_scr = refs
        key_sets = ((k1_ref, vt1_ref), (k2_ref, vt2_ref))
    else:
        q_ref, k1_ref, vt1_ref, o_ref, s0_scr, s1_scr = refs
        key_sets = ((k1_ref, vt1_ref),)
    q_blocks = q_ref.shape[2] // tq

    def sets_of(u):
        h = u % heads
        hs = slice(h * dv, (h + 1) * dv)
        return [(k_ref.shape[2],
                 lambda k_ref=k_ref: k_ref[0, h],
                 lambda off, n, vt_ref=vt_ref: vt_ref[0, hs, off:off + n],
                 None) for k_ref, vt_ref in key_sets]

    def q_of(u):
        blk, h = divmod(u, heads)
        return q_ref[0, h, blk * tq:(blk + 1) * tq, :]

    o_t = _attend_heads(q_blocks * heads, q_of, sets_of, (s0_scr, s1_scr))
    for blk in range(q_blocks):
        o_ref[0, blk * tq:(blk + 1) * tq, :] = o_t[blk * heads * dv:(blk + 1) * heads * dv].T.astype(o_ref.dtype)


def _attention(q, k1, vt1, k2=None, vt2=None, tq=512, q_blocks=1):
    b, heads, t, dq = q.shape
    dv = vt1.shape[1] // heads
    tq = _tile(t, tq)
    q_blocks = q_blocks if t % (tq * q_blocks) == 0 else 1
    tstep = tq * q_blocks
    two_sets = k2 is not None
    n_keys = k1.shape[2] + (k2.shape[2] if two_sets else 0)
    full4 = lambda i, j: (i, 0, 0, 0)
    full3 = lambda i, j: (i, 0, 0)
    in_specs = [pl.BlockSpec((1, heads, tstep, dq), lambda i, j: (i, 0, j, 0)),
                pl.BlockSpec((1,) + k1.shape[1:], full4),
                pl.BlockSpec((1,) + vt1.shape[1:], full3)]
    args = [q, k1, vt1]
    if two_sets:
        in_specs += [pl.BlockSpec((1,) + k2.shape[1:], full4),
                     pl.BlockSpec((1,) + vt2.shape[1:], full3)]
        args += [k2, vt2]
    return pl.pallas_call(
        functools.partial(_attn_kernel, heads=heads, dv=dv, two_sets=two_sets, tq=tq),
        grid=(b, t // tstep),
        in_specs=in_specs,
        out_specs=pl.BlockSpec((1, tstep, heads * dv), lambda i, j: (i, j, 0)),
        out_shape=jax.ShapeDtypeStruct((b, t, heads * dv), BF16),
        scratch_shapes=[pltpu.VMEM((n_keys, tq), F32)] * 2,
        compiler_params=_params(2),
    )(*args)


def _mla_prep_math(p, gqa, gkva, wq_ref, wkn_ref, wv_ref, g, rope_refs, qo_ref, ko_ref, vto_ref):
    rope = rope_refs is not None
    a, b_ = MLA_Q_RANK, MLA_Q_RANK + MLA_KV_RANK
    cq, ckv = p[:, :a], p[:, a:b_]
    kr, krs = p[:, b_:b_ + MLA_SLOT], p[:, b_ + MLA_SLOT:b_ + 2 * MLA_SLOT]

    def rms(x, g_):
        return x * lax.rsqrt(jnp.mean(x * x, axis=-1, keepdims=True) + EPS) * g_

    cqn = rms(cq, gqa).astype(BF16)
    ckvn = rms(ckv, gkva).astype(BF16)
    qa = _dot(cqn, wq_ref[0])
    kn = _dot(ckvn, wkn_ref[0])
    aq, ak = g[0:1], g[2:3]
    if rope:
        wqs_ref, cos_ref, sin_ref = rope_refs
        qs = _dot(cqn, wqs_ref[0])
        cos_t, sin_t = cos_ref[...], sin_ref[...]
        aq, bq = aq * cos_t, g[1:2] * sin_t
        ak, k_rot = ak * cos_t, krs * (g[3:4] * sin_t)

    def inv_rms(x):
        return lax.rsqrt(jnp.sum(x * x, axis=-1, keepdims=True) * (1.0 / MLA_QK) + EPS)

    for h in range(MLA_HEADS):
        sl = slice(h * MLA_SLOT, (h + 1) * MLA_SLOT)
        xq = qa[:, sl]
        yq = xq * aq
        if rope:
            yq = yq + qs[:, sl] * bq
        qo_ref[0, h] = (yq * (inv_rms(xq) * (MLA_QK ** -0.5 * LOG2E))).astype(BF16)
        xk = kn[:, sl] + kr
        yk = xk * ak
        if rope:
            yk = yk + k_rot
        ko_ref[0, h] = (yk * inv_rms(xk)).astype(BF16)
    vto_ref[0] = _dot_nt(wv_ref[0], ckvn).astype(BF16)


_ROPE_SWAP = tuple(list(range(8, 16)) + list(range(0, 8)) + list(range(24, 32)) + list(range(16, 24)))


def _slot(nope, rope_part):
    lead = (nope if nope is not None else rope_part).shape[:-1]
    dt = (nope if nope is not None else rope_part).dtype
    z = lambda n: jnp.zeros(lead + (n,), dt)
    return jnp.concatenate([nope if nope is not None else z(MLA_NOPE),
                            rope_part if rope_part is not None else z(MLA_ROPE),
                            z(MLA_SLOT - MLA_QK)], axis=-1)


def _rope_tables(s):
    t = jnp.arange(s)
    pos = jnp.stack([t // GRID_W, t % GRID_W], axis=-1).astype(F32)
    inv = ROPE_BASE ** (-jnp.arange(ROPE_FREQS, dtype=F32) / ROPE_FREQS)
    ang = pos[:, :, None] * inv
    cos, sin = jnp.cos(ang), jnp.sin(ang)
    cos_t = jnp.concatenate([cos[:, 0], cos[:, 0], cos[:, 1], cos[:, 1]], axis=-1)
    sin_t = jnp.concatenate([-sin[:, 0], sin[:, 0], -sin[:, 1], sin[:, 1]], axis=-1)
    return _slot(jnp.ones((s, MLA_NOPE), F32), cos_t), _slot(None, sin_t)


def _mla_weights(w_q_up, w_kv_up, g_q, g_k):
    depth = w_q_up.shape[0]
    swap = jnp.array(_ROPE_SWAP)
    wq = w_q_up.reshape(depth, MLA_Q_RANK, MLA_HEADS, MLA_QK)
    wkv = w_kv_up.reshape(depth, MLA_KV_RANK, MLA_HEADS, MLA_NOPE + MLA_V)
    flat = lambda w: w.reshape(depth, w.shape[1], -1).astype(BF16)
    wq_slot = flat(_slot(wq[..., :MLA_NOPE], wq[..., MLA_NOPE:]))
    wqs_slot = flat(_slot(None, wq[..., MLA_NOPE:][..., swap]))
    wkn_slot = flat(_slot(wkv[..., :MLA_NOPE], None))
    wv = jnp.swapaxes(flat(wkv[..., MLA_NOPE:]), 1, 2)
    gains = jnp.stack([_slot(g_q[:, :MLA_NOPE], g_q[:, MLA_NOPE:]), _slot(None, g_q[:, MLA_NOPE:][:, swap]),
                       _slot(g_k[:, :MLA_NOPE], g_k[:, MLA_NOPE:]), _slot(None, g_k[:, MLA_NOPE:][:, swap])],
                      axis=1)
    return wq_slot, wkn_slot, wv, gains, wqs_slot


def _project_kernel(*refs, rope):
    (x_ref, sh_ref, sc_ref, g_ref, w_hy_ref, w_na_ref, w_m_ref, gq_na_ref, gk_na_ref, bd_ref,
     gqa_ref, gkva_ref, wq_ref, wkn_ref, wv_ref, gains_ref) = refs[:16]
    rope_refs = refs[16:19] if rope else None
    o_hy_ref, qn_ref, kn_ref, vtn_ref, qm_ref, km_ref, vtm_ref = refs[-7:]
    h = _adaln(x_ref[0], g_ref[...], sh_ref[0, 0], sc_ref[0, 0]).astype(BF16)
    _mla_prep_math(_dot(h, w_m_ref[0]), gqa_ref[...], gkva_ref[...], wq_ref, wkn_ref, wv_ref, gains_ref[0],
                   rope_refs, qm_ref, km_ref, vtm_ref)
    _na_prep_math(_dot(h, w_na_ref[0]), gq_na_ref[...], gk_na_ref[...], bd_ref[...], qn_ref, kn_ref, vtn_ref)
    o_hy_ref[0] = _dot(h, w_hy_ref[0]).astype(BF16)


def _project(x, mod, g, w_in, w_m, layer, na_g_q, na_g_k, g_qa, g_kva, mla_w, rope_tabs, tm):
    b, t, d = x.shape
    tm = _tile(t, tm)
    n_hy, n_na, n_m = 3 * HY_DIM, 3 * NA_DIM, w_m.shape[2]
    assert n_hy == n_na
    wq_slot, wkn_slot, wv, gains, wqs_slot = mla_w
    rope = rope_tabs is not None
    gq = jnp.tile(na_g_q, NA_HEADS)[None, :]
    gk = jnp.tile(na_g_k, NA_HEADS)[None, :]
    head = jnp.arange(NA_DIM) // NA_HEAD_DIM
    bd = (head[:, None] == head[None, :]).astype(BF16)
    const = lambda i, j: (0, 0)
    per_layer = lambda w: pl.BlockSpec((1,) + w.shape[1:], lambda i, j: (layer, 0, 0))
    in_specs = [pl.BlockSpec((1, tm, d), lambda i, j: (i, j, 0)),
                pl.BlockSpec((1, 1, 1, d), lambda i, j: (i, 0, 0, 0)),
                pl.BlockSpec((1, 1, 1, d), lambda i, j: (i, 1, 0, 0)),
                pl.BlockSpec((1, d), const),
                _resident((1, d, n_hy), lambda i, j: (layer, 0, 0)),
                _resident((1, d, n_na), lambda i, j: (layer, 0, 1)),
                _resident((1, d, n_m), lambda i, j: (layer, 0, 0)),
                pl.BlockSpec((1, NA_DIM), const), pl.BlockSpec((1, NA_DIM), const),
                pl.BlockSpec((NA_DIM, NA_DIM), const),
                pl.BlockSpec((1, MLA_Q_RANK), const), pl.BlockSpec((1, MLA_KV_RANK), const),
                per_layer(wq_slot), per_layer(wkn_slot), per_layer(wv), per_layer(gains)]
    args = [x, mod, mod, g, w_in, w_in, w_m, gq, gk, bd, g_qa[None, :], g_kva[None, :],
            wq_slot, wkn_slot, wv, gains]
    if rope:
        in_specs += [per_layer(wqs_slot),
                     pl.BlockSpec((tm, MLA_SLOT), lambda i, j: (j, 0)),
                     pl.BlockSpec((tm, MLA_SLOT), lambda i, j: (j, 0))]
        args += [wqs_slot] + list(rope_tabs)
    tok_major = lambda heads, width: (jax.ShapeDtypeStruct((b, heads, t, width), BF16),
                                      pl.BlockSpec((1, heads, tm, width), lambda i, j: (i, 0, j, 0)))
    transposed = lambda rows: (jax.ShapeDtypeStruct((b, rows, t), BF16),
                               pl.BlockSpec((1, rows, tm), lambda i, j: (i, 0, j)))
    outs = [(jax.ShapeDtypeStruct((b, t, n_hy), BF16), pl.BlockSpec((1, tm, n_hy), lambda i, j: (i, j, 0))),
            tok_major(NA_HEADS, NA_HEAD_DIM), tok_major(NA_HEADS, NA_HEAD_DIM), transposed(NA_DIM),
            tok_major(MLA_HEADS, MLA_SLOT), tok_major(MLA_HEADS, MLA_SLOT), transposed(MLA_HEADS * MLA_V)]
    return pl.pallas_call(
        functools.partial(_project_kernel, rope=rope),
        grid=(b, t // tm),
        in_specs=in_specs,
        out_specs=[spec for _, spec in outs],
        out_shape=[shape for shape, _ in outs],
        compiler_params=_params(2),
    )(*args)


def _mix_mlp_kernel(hy_ref, na_ref, mla_ref, x_ref, gate1_ref, sh_ref, sc_ref, gate2_ref, g_ref,
                    w_hy_ref, w_na_ref, w_mla_ref, w1_ref, b1_ref, w2_ref, b2_ref, o_ref, *, chunk):
    mix = (_dot(hy_ref[0], w_hy_ref[0]) + _dot(na_ref[0], w_na_ref[0]) + _dot(mla_ref[0], w_mla_ref[0]))
    x = x_ref[0] + gate1_ref[0, 0] * mix
    h = _adaln(x, g_ref[...], sh_ref[0, 0], sc_ref[0, 0]).astype(BF16)
    d_ff = w1_ref.shape[2]
    acc = jnp.zeros(x.shape, F32)
    for c0 in range(0, d_ff, chunk):
        a = jnp.maximum(_dot(h, w1_ref[0, :, c0:c0 + chunk]) + b1_ref[:, c0:c0 + chunk], 0.0)
        acc = acc + _dot((a * a).astype(BF16), w2_ref[0, c0:c0 + chunk, :])
    o_ref[0] = x + gate2_ref[0, 0] * (acc + b2_ref[...])


def _mix_mlp(o_hy, o_na, o_mla, x, mod, w_out, g, w1, b1, w2, b2, layer, tm):
    b, t, d = x.shape
    d_ff = w1.shape[2]
    n_hy, n_na, n_mla = o_hy.shape[-1], o_na.shape[-1], o_mla.shape[-1]
    assert n_hy == n_na and n_mla == n_hy + n_na
    tm = _tile(t, tm)
    tok = lambda n: pl.BlockSpec((1, tm, n), lambda i, j: (i, j, 0))
    modspec = lambda k: pl.BlockSpec((1, 1, 1, d), lambda i, j: (i, k, 0, 0))
    const = lambda i, j: (0, 0)
    lay = lambda i, j: (layer, 0, 0)
    return pl.pallas_call(
        functools.partial(_mix_mlp_kernel, chunk=_tile(d_ff, 1024)),
        grid=(b, t // tm),
        in_specs=[tok(n_hy), tok(n_na), tok(n_mla), tok(d),
                  modspec(2), modspec(3), modspec(4), modspec(5),
                  pl.BlockSpec((1, d), const),
                  _resident((1, n_hy, d), lay),
                  _resident((1, n_na, d), lambda i, j: (layer, 1, 0)),
                  _resident((1, n_mla, d), lambda i, j: (layer, 1, 0)),
                  _resident((1, d, d_ff), lay), pl.BlockSpec((1, d_ff), const),
                  _resident((1, d_ff, d), lay), pl.BlockSpec((1, d), const)],
        out_specs=tok(d),
        out_shape=jax.ShapeDtypeStruct((b, t, d), F32),
        compiler_params=_params(2),
    )(o_hy, o_na, o_mla, x, mod, mod, mod, mod, g, w_out, w_out, w_out, w1, b1, w2, b2)


def kernel(x, c, ctx, c_ctx, w_mod, b_mod, g_norm1, w_in, hy_conv_w, hy_conv_b, hy_f_w1, hy_f_b1, hy_f_w2, hy_f_b2, hy_f_w3, hy_f_b3, hy_freq, hy_bias, na_g_q, na_g_k, na_rpb, mla_g_qa, mla_g_kva, mla_w_q_up, mla_w_kv_up, mla_g_q, mla_g_k, w_out, g_norm2, w_ff1, b_ff1, w_ff2, b_ff2):
    b, s, d = x.shape
    lc = ctx.shape[1]
    depth = w_mod.shape[0]
    in_hn = 3 * HY_DIM + 3 * NA_DIM
    swap = jnp.array(_ROPE_SWAP)

    n_cond = -(-(b + 1) // 8) * 8
    cond = jnp.zeros((n_cond, d), F32).at[:b].set(c).at[b].set(c_ctx)
    mods = _modulation(cond, w_mod, b_mod).reshape(depth, n_cond, 6, 1, d)

    rope_tabs = _rope_tables(s)
    cs_x = _dft_matrices(s)
    cs_c = _dft_matrices(lc)
    filt = (hy_f_w1, hy_f_b1, hy_f_w2, hy_f_b2, hy_f_w3, hy_f_b3, hy_freq)
    spec_x = _hyena_spectra(s, cs_x, *filt)
    spec_c = _hyena_spectra(lc, cs_c, *filt)
    bias_tab = _na_bias_table(na_rpb, s // GRID_W)
    hy_bias = hy_bias.reshape(2 * depth, 1, HY_DIM)

    w_in_b, w_out_b = w_in.astype(BF16), w_out.astype(BF16)
    w_ff1_b, w_ff2_b = w_ff1.astype(BF16), w_ff2.astype(BF16)
    mla_w = _mla_weights(mla_w_q_up, mla_w_kv_up, mla_g_q, mla_g_k)
    w_kr = w_in[:, :, in_hn + MLA_Q_RANK + MLA_KV_RANK:]
    w_m_b = jnp.concatenate([w_in[:, :, in_hn:in_hn + MLA_Q_RANK + MLA_KV_RANK], _slot(None, w_kr),
                             _slot(None, w_kr[:, :, swap])], axis=-1).astype(BF16)

    cx = ctx
    for i in range(depth):
        last = i == depth - 1
        mod_x = mods[i, :b]
        mod_c = jnp.broadcast_to(mods[i, b], (b, 6, 1, d))
        g1 = g_norm1[i][None, :]
        prep = (i, na_g_q[i], na_g_k[i], mla_g_qa[i], mla_g_kva[i], mla_w)
        px_hy, qx, kx, vtx, mqx, mkx, mvx = _project(x, mod_x, g1, w_in_b, w_m_b, *prep, rope_tabs,
                                                     PROJECT_TOKEN_TILE)
        pc_hy, qc, kc, vtc, mqc, mkc, mvc = _project(cx, mod_c, g1, w_in_b, w_m_b, *prep, None,
                                                     PROJECT_TOKEN_TILE)

        o_na = _na_latent(qx, kx, vtx, kc, vtc, bias_tab, i)
        o_mla = _attention(mqx, mkx, mvx, mkc, mvc, q_blocks=MLA_QUERY_BLOCKS)

        o_hy = _hyena(px_hy, hy_conv_w[i], hy_conv_b[i][None, :], cs_x, *spec_x, hy_bias, i)

        g2 = g_norm2[i][None, :]
        b1, b2 = b_ff1[i][None, :], b_ff2[i][None, :]
        x = _mix_mlp(o_hy, o_na, o_mla, x, mod_x, w_out_b, g2, w_ff1_b, b1, w_ff2_b, b2, i, MLP_TOKEN_TILE)

        if not last:
            oc_hy = _hyena(pc_hy, hy_conv_w[i], hy_conv_b[i][None, :], cs_c, *spec_c, hy_bias, i)
            oc_na = _attention(qc, kc, vtc)
            oc_mla = _attention(mqc, mkc, mvc)
            cx = _mix_mlp(oc_hy, oc_na, oc_mla, cx, mod_c, w_out_b, g2, w_ff1_b, b1, w_ff2_b, b2, i,
                          MLP_TOKEN_TILE)
    return x
```

```python
import functools
import math

import jax
import jax.numpy as jnp
from jax import lax
from jax.experimental import pallas as pl
from jax.experimental.pallas import tpu as pltpu

F32 = jnp.float32
BF16 = jnp.bfloat16
HIGHEST = lax.Precision.HIGHEST

EPS = 1e-6
NEG_INF = -1e9
GRID_W = 64

HY_DIM = 256
HY_BANDS = 8
HY_DECAY_TARGET = 1e-2
HY_FAST = 0.3
HY_SLOW = 1.5

NA_HEADS = 4
NA_HEAD_DIM = 64
NA_DIM = NA_HEADS * NA_HEAD_DIM
NA_WIN_ROWS = 8
NA_WIN_COLS = 16

MLA_HEADS = 8
MLA_Q_RANK = 256
MLA_KV_RANK = 128
MLA_NOPE = 64
MLA_ROPE = 32
MLA_V = 64
MLA_QK = MLA_NOPE + MLA_ROPE
MLA_SLOT = 128
ROPE_BASE = 10000.0
ROPE_FREQS = MLA_ROPE // 4

VMEM_LIMIT_BYTES = 56 * 1024 * 1024
HYENA_VMEM_LIMIT_BYTES = 60 * 1024 * 1024
MLA_QUERY_BLOCKS = 2
PROJECT_TOKEN_TILE = 1024
MLP_TOKEN_TILE = 1024

def _params(n_grid_dims):
    return pltpu.CompilerParams(dimension_semantics=("arbitrary",) * n_grid_dims,
                                vmem_limit_bytes=VMEM_LIMIT_BYTES)


def _tile(total, preferred):
    t = min(total, preferred)
    while total % t:
        t //= 2
    return t


def _resident(shape, index_map):
    return pl.BlockSpec(shape, index_map, pipeline_mode=pl.Buffered(1))


def _dot(a, b):
    return jnp.dot(a, b, preferred_element_type=F32)


def _split_bf16(x):
    hi = x.astype(BF16)
    return hi, (x - hi.astype(F32)).astype(BF16)


def _dot_3x(a, b):
    a_hi, a_lo = _split_bf16(a)
    b_hi, b_lo = _split_bf16(b)
    return _dot(a_hi, b_hi) + (_dot(a_hi, b_lo) + _dot(a_lo, b_hi))


def _dot_hi(a, b):
    return jnp.dot(a, b, preferred_element_type=F32, precision=HIGHEST)


def _dot_nt(a, b):
    return lax.dot_general(a, b, (((1,), (1,)), ((), ())), preferred_element_type=F32)


def _mod_kernel(cond_ref, w_ref, b_ref, o_ref):
    a = cond_ref[...]
    a = a / (1.0 + jnp.exp(-a))
    n = a.shape[0]
    a_hi, a_lo = _split_bf16(a)
    w_hi, w_lo = _split_bf16(w_ref[0])
    y = _dot(jnp.concatenate([a_hi, a_lo], axis=0), w_hi)
    o_ref[0] = y[:n] + y[n:] + _dot(a_hi, w_lo) + b_ref[0]


def _modulation(cond, w_mod, b_mod):
    depth, d, d6 = w_mod.shape
    n = cond.shape[0]
    tn = _tile(d6, 1536)
    return pl.pallas_call(
        _mod_kernel,
        grid=(depth, d6 // tn),
        in_specs=[pl.BlockSpec((n, d), lambda l, j: (0, 0)),
                  pl.BlockSpec((1, d, tn), lambda l, j: (l, 0, j)),
                  pl.BlockSpec((1, 1, tn), lambda l, j: (l, 0, j))],
        out_specs=pl.BlockSpec((1, n, tn), lambda l, j: (l, 0, j)),
        out_shape=jax.ShapeDtypeStruct((depth, n, d6), F32),
        compiler_params=_params(2),
    )(cond, w_mod, b_mod.reshape(depth, 1, d6))


def _adaln(x, g, shift, scale):
    y = x * lax.rsqrt(jnp.mean(x * x, axis=-1, keepdims=True) + EPS) * g
    return y * (1.0 + scale) + shift


def _dft_matrices(length):
    n = 2 * length
    blk = min(128, length)
    n_blk = length // blk
    f = jnp.arange(length, dtype=jnp.int32)[:, None]
    t0 = (blk * jnp.arange(n_blk, dtype=jnp.int32))[None, :]
    dt = jnp.arange(blk, dtype=jnp.int32)[None, :]
    a = ((f * t0) % n).astype(F32) * (2.0 * math.pi / n)
    b = ((f * dt) % n).astype(F32) * (2.0 * math.pi / n)
    rows = _tile(length, 256)

    def combine(ca_ref, sa_ref, cb_ref, sb_ref, c_ref, s_ref):
        cb, sb = cb_ref[...], sb_ref[...]
        for j in range(n_blk):
            ca, sa = ca_ref[:, j:j + 1], sa_ref[:, j:j + 1]
            c_ref[:, j * blk:(j + 1) * blk] = (ca * cb - sa * sb).astype(BF16)
            s_ref[:, j * blk:(j + 1) * blk] = (sa * cb + ca * sb).astype(BF16)

    coarse = pl.BlockSpec((rows, n_blk), lambda i: (i, 0))
    fine = pl.BlockSpec((rows, blk), lambda i: (i, 0))
    full = pl.BlockSpec((rows, length), lambda i: (i, 0))
    out = jax.ShapeDtypeStruct((length, length), BF16)
    return pl.pallas_call(
        combine,
        grid=(length // rows,),
        in_specs=[coarse, coarse, fine, fine],
        out_specs=[full, full],
        out_shape=[out, out],
        compiler_params=_params(1),
    )(jnp.cos(a), jnp.sin(a), jnp.cos(b), jnp.sin(b))


def _filt_kernel(z_ref, dec_ref, w1_ref, b1_ref, w2_ref, b2_ref, w3_ref, b3_ref, fr_ref, hsum_ref, hdiff_ref):
    z = z_ref[...]
    fr = fr_ref[0]
    a = jnp.sin(fr[0:1] * (_dot_3x(z, w1_ref[0]) + b1_ref[0]))
    a = jnp.sin(fr[1:2] * (_dot_3x(a, w2_ref[0]) + b2_ref[0]))
    h = _dot_3x(a, w3_ref[0]) + b3_ref[0]
    dec = dec_ref[...]
    c = HY_DIM
    row0 = lax.broadcasted_iota(jnp.int32, dec.shape, 0) == 0
    for o in range(2):
        hf = h[:, (2 * o) * c:(2 * o + 1) * c] * dec
        hb = h[:, (2 * o + 1) * c:(2 * o + 2) * c] * dec
        nrm = (jnp.sum(jnp.abs(hf), axis=0, keepdims=True)
               + jnp.sum(jnp.abs(hb), axis=0, keepdims=True) + EPS)
        hf = hf / nrm
        hb = jnp.where(row0, 0.0, hb / nrm)
        hsum_ref[0, :, o * c:(o + 1) * c] = hf + hb
        hdiff_ref[0, :, o * c:(o + 1) * c] = hf - hb


def _alternating_sign(shape):
    row = lax.broadcasted_iota(jnp.int32, shape, 0)
    return (1 - 2 * (row % 2)).astype(F32)


def _spec_kernel(hsum_ref, hdiff_ref, c_ref, s_ref, hc_ref, hs_ref, hn_ref):
    length = hsum_ref.shape[1]
    hsum = hsum_ref[0]
    hdiff = hdiff_ref[0]

    row = lax.broadcasted_iota(jnp.int32, hsum.shape, 0)
    wf = jnp.where(row == 0, 0.5 / length, 1.0 / length)
    hc_ref[0] = _dot(c_ref[...], hsum.astype(BF16)) * wf
    hs_ref[0] = _dot(s_ref[...], hdiff.astype(BF16)) * wf
    hn_ref[0] = jnp.sum(_alternating_sign(hsum.shape) * hsum, axis=0, keepdims=True) * (0.5 / length)


def _hyena_spectra(length, cs, f_w1, f_b1, f_w2, f_b2, f_w3, f_b3, freq):
    depth = f_w1.shape[0]
    pos_dim = f_w1.shape[1]
    hid = f_w1.shape[2]
    kpad = 32
    t = jnp.linspace(0.0, 1.0, length, dtype=F32)[:, None]
    w = (2.0 * math.pi / length) * jnp.arange(length, dtype=F32)[:, None]
    bands = jnp.linspace(1e-4, HY_BANDS - 1, HY_BANDS, dtype=F32)
    z = jnp.concatenate([t, jnp.cos(bands * w), -jnp.sin(bands * w)], axis=-1)
    z = jnp.pad(z, ((0, 0), (0, kpad - pos_dim)))
    w1 = jnp.pad(f_w1, ((0, 0), (0, kpad - pos_dim), (0, 0)))
    deltas = jnp.linspace(math.log(HY_DECAY_TARGET) / HY_SLOW, math.log(HY_DECAY_TARGET) / HY_FAST,
                          HY_DIM, dtype=F32)
    decay = jnp.exp(-t * jnp.abs(deltas))
    c2 = 2 * HY_DIM
    c4 = 4 * HY_DIM
    lay3 = lambda l: (l, 0, 0)
    hsum, hdiff = pl.pallas_call(
        _filt_kernel,
        grid=(depth,),
        in_specs=[pl.BlockSpec((length, kpad), lambda l: (0, 0)),
                  pl.BlockSpec((length, HY_DIM), lambda l: (0, 0)),
                  pl.BlockSpec((1, kpad, hid), lay3),
                  pl.BlockSpec((1, 1, hid), lay3),
                  pl.BlockSpec((1, hid, hid), lay3),
                  pl.BlockSpec((1, 1, hid), lay3),
                  pl.BlockSpec((1, hid, c4), lay3),
                  pl.BlockSpec((1, 1, c4), lay3),
                  pl.BlockSpec((1, 2, hid), lay3)],
        out_specs=[pl.BlockSpec((1, length, c2), lay3),
                   pl.BlockSpec((1, length, c2), lay3)],
        out_shape=[jax.ShapeDtypeStruct((depth, length, c2), F32),
                   jax.ShapeDtypeStruct((depth, length, c2), F32)],
        compiler_params=_params(1),
    )(z, decay, w1, f_b1.reshape(depth, 1, hid), f_w2, f_b2.reshape(depth, 1, hid),
      f_w3, f_b3.reshape(depth, 1, c4), freq)
    blk_in = pl.BlockSpec((1, length, c2), lay3, pipeline_mode=pl.Buffered(1))
    blk_out = pl.BlockSpec((1, length, c2), lay3)
    return pl.pallas_call(
        _spec_kernel,
        grid=(depth,),
        in_specs=[blk_in, blk_in, _resident((length, length), lambda l: (0, 0)),
                  _resident((length, length), lambda l: (0, 0))],
        out_specs=[blk_out, blk_out, pl.BlockSpec((1, 1, c2), lay3)],
        out_shape=[jax.ShapeDtypeStruct((depth, length, c2), F32),
                   jax.ShapeDtypeStruct((depth, length, c2), F32),
                   jax.ShapeDtypeStruct((depth, 1, c2), F32)],
        compiler_params=pltpu.CompilerParams(dimension_semantics=("arbitrary",),
                                             vmem_limit_bytes=HYENA_VMEM_LIMIT_BYTES),
    )(hsum, hdiff, *cs)


HY_ROWS = 256
HY_HALO = 8


HY_PAIR = 2


def _hyena_kernel(pv_ref, pg_ref, wv_ref, bv_ref, wg_ref, bg_ref, c_ref, s_ref, hc_ref, hs_ref, hn_ref, bias_ref,
                  o_ref, z_scr, xc_scr, xs_scr, yc_scr, ys_scr):
    length = pv_ref.shape[1]
    c = HY_DIM
    rows = min(HY_ROWS, length)
    chunks = [(r0, rows) for r0 in range(0, length, rows)]
    halves = [(i, slice(i * c, (i + 1) * c)) for i in range(HY_PAIR)]

    def short_conv_rows(p_ref, w_ref, b_ref, i, r0):
        lo = max(r0 - HY_HALO, 0)
        hi = min(r0 + rows + HY_HALO, length)
        u = p_ref[i, lo:hi, :].astype(F32)
        row = lax.broadcasted_iota(jnp.int32, u.shape, 0) + lo
        prev = jnp.where(row == 0, 0.0, pltpu.roll(u, 1, 0))
        nxt = jnp.where(row == length - 1, 0.0, pltpu.roll(u, hi - lo - 1, 0))
        y = prev * w_ref[0:1, :] + u * w_ref[1:2, :] + nxt * w_ref[2:3, :] + b_ref[...]
        return y[r0 - lo:r0 - lo + rows]

    @pl.when(pl.program_id(1) == 0)
    def _():
        for r0, n in chunks:
            for i, lanes in halves:
                z_scr[r0:r0 + n, lanes] = short_conv_rows(pv_ref, wv_ref, bv_ref, i, r0)

    sign = _alternating_sign((rows, c))
    zb = z_scr[...].astype(BF16)
    xc_scr[...] = _dot(c_ref[...], zb)
    xs_scr[...] = _dot(s_ref[...], zb)
    xn = [jnp.zeros((1, c), F32) for _ in halves]
    for r0, n in chunks:
        hc, hs = hc_ref[0, r0:r0 + n, :], hs_ref[0, r0:r0 + n, :]
        for i, lanes in halves:
            xc, xs = xc_scr[r0:r0 + n, lanes], xs_scr[r0:r0 + n, lanes]
            yc_scr[r0:r0 + n, lanes] = (xc * hc - xs * hs).astype(BF16)
            ys_scr[r0:r0 + n, lanes] = (xc * hs + xs * hc).astype(BF16)
            xn[i] = xn[i] + jnp.sum(sign * z_scr[r0:r0 + n, lanes], axis=0, keepdims=True)
    xc_scr[...] = _dot(c_ref[...], yc_scr[...])
    xs_scr[...] = _dot(s_ref[...], ys_scr[...])
    bias = bias_ref[0]
    for r0, n in chunks:
        for i, lanes in halves:
            z = z_scr[r0:r0 + n, lanes]
            conv = xc_scr[r0:r0 + n, lanes] + xs_scr[r0:r0 + n, lanes] + sign * (xn[i] * hn_ref[0])
            z = short_conv_rows(pg_ref, wg_ref, bg_ref, i, r0) * (conv + z * bias)
            z_scr[r0:r0 + n, lanes] = z
            o_ref[i, r0:r0 + n, :] = z.astype(o_ref.dtype)


def _hyena(p_hy, conv_w, conv_b, cs, hc, hs, hn, bias, layer):
    b, length, _ = p_hy.shape
    assert b % HY_PAIR == 0
    c = HY_DIM
    wide = HY_PAIR * c
    return pl.pallas_call(
        _hyena_kernel,
        grid=(b // HY_PAIR, 2),
        in_specs=[pl.BlockSpec((HY_PAIR, length, c), lambda i, o: (i, 0, 0), pipeline_mode=pl.Buffered(1)),
                  pl.BlockSpec((HY_PAIR, length, c), lambda i, o: (i, 0, o + 1)),
                  pl.BlockSpec((3, c), lambda i, o: (0, 0)),
                  pl.BlockSpec((1, c), lambda i, o: (0, 0)),
                  pl.BlockSpec((3, c), lambda i, o: (0, o + 1)),
                  pl.BlockSpec((1, c), lambda i, o: (0, o + 1)),
                  _resident((length, length), lambda i, o: (0, 0)),
                  _resident((length, length), lambda i, o: (0, 0)),
                  pl.BlockSpec((1, length, c), lambda i, o: (layer, 0, o), pipeline_mode=pl.Buffered(1)),
                  pl.BlockSpec((1, length, c), lambda i, o: (layer, 0, o), pipeline_mode=pl.Buffered(1)),
                  pl.BlockSpec((1, 1, c), lambda i, o: (layer, 0, o)),
                  pl.BlockSpec((1, 1, c), lambda i, o: (2 * layer + o, 0, 0))],
        out_specs=pl.BlockSpec((HY_PAIR, length, c), lambda i, o: (i, 0, 0)),
        out_shape=jax.ShapeDtypeStruct((b, length, c), BF16),
        scratch_shapes=[pltpu.VMEM((length, wide), F32), pltpu.VMEM((length, wide), F32),
                        pltpu.VMEM((length, wide), F32), pltpu.VMEM((length, wide), BF16),
                        pltpu.VMEM((length, wide), BF16)],
        compiler_params=pltpu.CompilerParams(dimension_semantics=("arbitrary", "arbitrary"),
                                             vmem_limit_bytes=HYENA_VMEM_LIMIT_BYTES),
    )(p_hy, p_hy, conv_w, conv_b, conv_w, conv_b, *cs, hc, hs, hn, bias)


def _na_prep_math(p, gq, gk, bd, qo_ref, ko_ref, vo_ref):
    def head_rms(x, g):
        sq = x * x
        hi = sq.astype(BF16)
        lo = (sq - hi.astype(F32)).astype(BF16)
        ms = (_dot(hi, bd) + _dot(lo, bd)) * (1.0 / NA_HEAD_DIM)
        return x * lax.rsqrt(ms + EPS) * g

    q = (head_rms(p[:, :NA_DIM], gq) * (NA_HEAD_DIM ** -0.5 * LOG2E)).astype(BF16)
    k = head_rms(p[:, NA_DIM:2 * NA_DIM], gk).astype(BF16)
    for h in range(NA_HEADS):
        sl = slice(h * NA_HEAD_DIM, (h + 1) * NA_HEAD_DIM)
        qo_ref[0, h] = q[:, sl]
        ko_ref[0, h] = k[:, sl]
    vo_ref[0] = p[:, 2 * NA_DIM:].T.astype(BF16)


NA_QROWS = 8
NA_KROWS = NA_QROWS + NA_WIN_ROWS
NA_BATCH_PAIR = 4


def _rpb_expand_kernel(r_ref, oh_ref, m_ref, o_ref):
    o_ref[...] = (_dot_hi(r_ref[...], oh_ref[...]) + m_ref[...]) * LOG2E


def _na_first_key_row(rb, rows):
    return jnp.clip(NA_QROWS * rb - NA_WIN_ROWS // 2, 0, rows - NA_KROWS)


def _na_block_layouts(rows):
    nrb = rows // NA_QROWS

    def layout(rb):
        w0 = min(max(NA_QROWS * rb - NA_WIN_ROWS // 2, 0), rows - NA_KROWS)
        out = []
        for j in range(NA_QROWS):
            r = NA_QROWS * rb + j
            row0 = min(max(r - NA_WIN_ROWS // 2, 0), rows - NA_WIN_ROWS)
            assert w0 <= row0 and row0 + NA_WIN_ROWS <= w0 + NA_KROWS
            out.append((r - w0, row0 - w0))
        return tuple(out)

    assert all(layout(rb) == layout(1) for rb in range(1, nrb - 1))
    return [layout(0), layout(min(1, nrb - 1)), layout(nrb - 1)]


def _na_table_kernel(t_ref, o_ref, *, layouts):
    neg = jnp.full((GRID_W, GRID_W), NEG_INF, F32)
    for pos, layout in enumerate(layouts):
        for i in range(NA_KROWS):
            pieces = [t_ref[0, i - r_rel + NA_WIN_ROWS - 1] if row0_rel <= i < row0_rel + NA_WIN_ROWS else neg
                      for r_rel, row0_rel in layout]
            o_ref[0, pos, i * GRID_W:(i + 1) * GRID_W, :] = jnp.concatenate(pieces, axis=1)


def _na_bias_table(rpb, rows):
    depth, heads, n_dr, n_dc = rpb.shape
    kc = jnp.arange(GRID_W)[:, None]
    qc = jnp.arange(GRID_W)[None, :]
    dc = jnp.clip(kc - qc + NA_WIN_COLS - 1, 0, n_dc - 1)
    cstart = jnp.clip(qc - NA_WIN_COLS // 2, 0, GRID_W - NA_WIN_COLS)
    valid = ((kc >= cstart) & (kc < cstart + NA_WIN_COLS)).reshape(1, GRID_W * GRID_W)
    n_pad = 32
    onehot = ((dc.reshape(1, -1) == jnp.arange(n_pad)[:, None]) & valid).astype(F32)
    mask = jnp.where(valid, 0.0, NEG_INF).astype(F32)
    rows = depth * heads * n_dr
    rpb2 = jnp.pad(rpb.reshape(rows, n_dc).astype(F32), ((0, 0), (0, n_pad - n_dc)))
    t1 = pl.pallas_call(
        _rpb_expand_kernel,
        out_shape=jax.ShapeDtypeStruct((rows, GRID_W * GRID_W), F32),
        compiler_params=pltpu.CompilerParams(vmem_limit_bytes=VMEM_LIMIT_BYTES),
    )(rpb2, onehot, mask).reshape(depth * heads, n_dr, GRID_W, GRID_W)
    nk, nq = NA_KROWS * GRID_W, NA_QROWS * GRID_W
    tab = pl.pallas_call(
        functools.partial(_na_table_kernel, layouts=_na_block_layouts(rows)),
        grid=(depth * heads,),
        in_specs=[pl.BlockSpec((1, n_dr, GRID_W, GRID_W), lambda g: (g, 0, 0, 0))],
        out_specs=pl.BlockSpec((1, 3, nk, nq), lambda g: (g, 0, 0, 0)),
        out_shape=jax.ShapeDtypeStruct((depth * heads, 3, nk, nq), F32),
        compiler_params=_params(1),
    )(t1)
    return tab.reshape(depth, heads, 3, nk, nq)


ATTN_CHUNK_ELEMS = 64 * 1024
ATTN_ONES_ROWS = 16
LOG2E = math.log2(math.e)


def _attend_heads(heads, q_of, sets_of, s_scrs):
    row_max = [None] * heads
    outs = [None] * heads

    def score_phase(h):
        q = q_of(h)
        scr = s_scrs[h % 2]
        off, m = 0, None
        for n, k_fn, _, bias_fn in sets_of(h):
            s = _dot_nt(k_fn(), q)
            if bias_fn is not None:
                s = s + bias_fn()
            scr[off:off + n, :] = s
            mj = jnp.max(s, axis=0, keepdims=True)
            m = mj if m is None else jnp.maximum(m, mj)
            off += n
            yield
        row_max[h] = m

    def value_phase(h):
        scr = s_scrs[h % 2]
        m = row_max[h]
        base, acc = 0, None
        for n_set, _, vt_fn, _ in sets_of(h):
            for off, n in _key_chunks(n_set, ATTN_CHUNK_ELEMS // scr.shape[1]):
                p = jnp.exp2(scr[base + off:base + off + n, :] - m).astype(BF16)
                vt = vt_fn(off, n)
                lhs = jnp.concatenate([vt, jnp.ones((ATTN_ONES_ROWS, n), BF16)], axis=0)
                oj = _dot(lhs, p)
                acc = oj if acc is None else acc + oj
                yield
            base += n_set
        dv = acc.shape[0] - ATTN_ONES_ROWS
        outs[h] = acc[:dv] / acc[dv:dv + 1]

    for _ in score_phase(0):
        pass
    for h in range(heads):
        nxt = score_phase(h + 1) if h + 1 < heads else iter(())
        cur = value_phase(h)
        done_n = done_c = False
        while not (done_n and done_c):
            if not done_n:
                done_n = next(nxt, "end") == "end"
            if not done_c:
                done_c = next(cur, "end") == "end"
    return jnp.concatenate(outs, axis=0)


def _key_chunks(total, chunk):
    sizes = [chunk] * (total // chunk)
    if total % chunk:
        sizes.append(total % chunk)
    offs = [sum(sizes[:i]) for i in range(len(sizes))]
    return list(zip(offs, sizes))


def _na_kernel(q_ref, k_ref, vt_ref, kc_ref, vct_ref, b_ref, o_ref, s0_scr, s1_scr, *, rows):
    start = pl.multiple_of(_na_first_key_row(pl.program_id(0), rows) * GRID_W, NA_WIN_ROWS // 2 * GRID_W)
    nk = NA_KROWS * GRID_W
    dh = NA_HEAD_DIM

    def sets_of(u):
        i, h = divmod(u, NA_HEADS)
        hs = slice(h * dh, (h + 1) * dh)
        window = (nk,
                  lambda: k_ref[i, h, pl.ds(start, nk), :],
                  lambda off, n: vt_ref[i, hs, pl.ds(pl.multiple_of(start + off, 128), n)],
                  lambda: b_ref[0, h, 0])
        context = (kc_ref.shape[2],
                   lambda: kc_ref[i, h],
                   lambda off, n: vct_ref[i, hs, off:off + n],
                   None)
        return [window, context]

    n_pair = q_ref.shape[0]
    o_t = _attend_heads(n_pair * NA_HEADS, lambda u: q_ref[u // NA_HEADS, u % NA_HEADS], sets_of,
                        (s0_scr, s1_scr))
    for i in range(n_pair):
        o_ref[i] = o_t[i * NA_DIM:(i + 1) * NA_DIM].T.astype(o_ref.dtype)


def _na_latent(q, k, vt, kc, vct, bias_tab, layer):
    b, heads, s, dh = q.shape
    ctx = kc.shape[2]
    rows = s // GRID_W
    assert rows % NA_QROWS == 0 and rows >= NA_KROWS and (rows - NA_KROWS) % (NA_WIN_ROWS // 2) == 0
    nrb = rows // NA_QROWS
    tq = NA_QROWS * GRID_W

    def bias_index(rb, i):
        return (layer, 0, jnp.where(rb == 0, 0, jnp.where(rb == nrb - 1, 2, 1)), 0, 0)

    pair = NA_BATCH_PAIR if b % NA_BATCH_PAIR == 0 else 1
    full4 = lambda rb, i: (i, 0, 0, 0)
    full3 = lambda rb, i: (i, 0, 0)
    return pl.pallas_call(
        functools.partial(_na_kernel, rows=rows),
        grid=(nrb, b // pair),
        in_specs=[pl.BlockSpec((pair, heads, tq, dh), lambda rb, i: (i, 0, rb, 0)),
                  pl.BlockSpec((pair, heads, s, dh), full4),
                  pl.BlockSpec((pair, heads * dh, s), full3),
                  pl.BlockSpec((pair, heads, ctx, dh), full4),
                  pl.BlockSpec((pair, heads * dh, ctx), full3),
                  pl.BlockSpec((1, heads, 1, NA_KROWS * GRID_W, tq), bias_index)],
        out_specs=pl.BlockSpec((pair, tq, heads * dh), lambda rb, i: (i, rb, 0)),
        out_shape=jax.ShapeDtypeStruct((b, s, heads * dh), BF16),
        scratch_shapes=[pltpu.VMEM((NA_KROWS * GRID_W + ctx, tq), F32)] * 2,
        compiler_params=_params(2),
    )(q, k, vt, kc, vct, bias_tab)


def _attn_kernel(*refs, heads, dv, two_sets, tq):
    if two_sets:
        q_ref, k1_ref, vt1_ref, k2_ref, vt2_ref, o_ref, s0_scr, s1_scr = refs
        key_sets = ((k1_ref, vt1_ref), (k2_ref, vt2_ref))
    else:
        q_ref, k1_ref, vt1_ref, o_ref, s0_scr, s1_scr = refs
        key_sets = ((k1_ref, vt1_ref),)
    q_blocks = q_ref.shape[2] // tq

    def sets_of(u):
        h = u % heads
        hs = slice(h * dv, (h + 1) * dv)
        return [(k_ref.shape[2],
                 lambda k_ref=k_ref: k_ref[0, h],
                 lambda off, n, vt_ref=vt_ref: vt_ref[0, hs, off:off + n],
                 None) for k_ref, vt_ref in key_sets]

    def q_of(u):
        blk, h = divmod(u, heads)
        return q_ref[0, h, blk * tq:(blk + 1) * tq, :]

    o_t = _attend_heads(q_blocks * heads, q_of, sets_of, (s0_scr, s1_scr))
    for blk in range(q_blocks):
        o_ref[0, blk * tq:(blk + 1) * tq, :] = o_t[blk * heads * dv:(blk + 1) * heads * dv].T.astype(o_ref.dtype)


def _attention(q, k1, vt1, k2=None, vt2=None, tq=512, q_blocks=1):
    b, heads, t, dq = q.shape
    dv = vt1.shape[1] // heads
    tq = _tile(t, tq)
    q_blocks = q_blocks if t % (tq * q_blocks) == 0 else 1
    tstep = tq * q_blocks
    two_sets = k2 is not None
    n_keys = k1.shape[2] + (k2.shape[2] if two_sets else 0)
    full4 = lambda i, j: (i, 0, 0, 0)
    full3 = lambda i, j: (i, 0, 0)
    in_specs = [pl.BlockSpec((1, heads, tstep, dq), lambda i, j: (i, 0, j, 0)),
                pl.BlockSpec((1,) + k1.shape[1:], full4),
                pl.BlockSpec((1,) + vt1.shape[1:], full3)]
    args = [q, k1, vt1]
    if two_sets:
        in_specs += [pl.BlockSpec((1,) + k2.shape[1:], full4),
                     pl.BlockSpec((1,) + vt2.shape[1:], full3)]
        args += [k2, vt2]
    return pl.pallas_call(
        functools.partial(_attn_kernel, heads=heads, dv=dv, two_sets=two_sets, tq=tq),
        grid=(b, t // tstep),
        in_specs=in_specs,
        out_specs=pl.BlockSpec((1, tstep, heads * dv), lambda i, j: (i, j, 0)),
        out_shape=jax.ShapeDtypeStruct((b, t, heads * dv), BF16),
        scratch_shapes=[pltpu.VMEM((n_keys, tq), F32)] * 2,
        compiler_params=_params(2),
    )(*args)


def _mla_prep_math(p, gqa, gkva, wq_ref, wkn_ref, wv_ref, g, rope_refs, qo_ref, ko_ref, vto_ref):
    rope = rope_refs is not None
    a, b_ = MLA_Q_RANK, MLA_Q_RANK + MLA_KV_RANK
    cq, ckv = p[:, :a], p[:, a:b_]
    kr, krs = p[:, b_:b_ + MLA_SLOT], p[:, b_ + MLA_SLOT:b_ + 2 * MLA_SLOT]

    def rms(x, g_):
        return x * lax.rsqrt(jnp.mean(x * x, axis=-1, keepdims=True) + EPS) * g_

    cqn = rms(cq, gqa).astype(BF16)
    ckvn = rms(ckv, gkva).astype(BF16)
    qa = _dot(cqn, wq_ref[0])
    kn = _dot(ckvn, wkn_ref[0])
    aq, ak = g[0:1], g[2:3]
    if rope:
        wqs_ref, cos_ref, sin_ref = rope_refs
        qs = _dot(cqn, wqs_ref[0])
        cos_t, sin_t = cos_ref[...], sin_ref[...]
        aq, bq = aq * cos_t, g[1:2] * sin_t
        ak, k_rot = ak * cos_t, krs * (g[3:4] * sin_t)

    def inv_rms(x):
        return lax.rsqrt(jnp.sum(x * x, axis=-1, keepdims=True) * (1.0 / MLA_QK) + EPS)

    for h in range(MLA_HEADS):
        sl = slice(h * MLA_SLOT, (h + 1) * MLA_SLOT)
        xq = qa[:, sl]
        yq = xq * aq
        if rope:
            yq = yq + qs[:, sl] * bq
        qo_ref[0, h] = (yq * (inv_rms(xq) * (MLA_QK ** -0.5 * LOG2E))).astype(BF16)
        xk = kn[:, sl] + kr
        yk = xk * ak
        if rope:
            yk = yk + k_rot
        ko_ref[0, h] = (yk * inv_rms(xk)).astype(BF16)
    vto_ref[0] = _dot_nt(wv_ref[0], ckvn).astype(BF16)


_ROPE_SWAP = tuple(list(range(8, 16)) + list(range(0, 8)) + list(range(24, 32)) + list(range(16, 24)))


def _slot(nope, rope_part):
    lead = (nope if nope is not None else rope_part).shape[:-1]
    dt = (nope if nope is not None else rope_part).dtype
    z = lambda n: jnp.zeros(lead + (n,), dt)
    return jnp.concatenate([nope if nope is not None else z(MLA_NOPE),
                            rope_part if rope_part is not None else z(MLA_ROPE),
                            z(MLA_SLOT - MLA_QK)], axis=-1)


def _rope_tables(s):
    t = jnp.arange(s)
    pos = jnp.stack([t // GRID_W, t % GRID_W], axis=-1).astype(F32)
    inv = ROPE_BASE ** (-jnp.arange(ROPE_FREQS, dtype=F32) / ROPE_FREQS)
    ang = pos[:, :, None] * inv
    cos, sin = jnp.cos(ang), jnp.sin(ang)
    cos_t = jnp.concatenate([cos[:, 0], cos[:, 0], cos[:, 1], cos[:, 1]], axis=-1)
    sin_t = jnp.concatenate([-sin[:, 0], sin[:, 0], -sin[:, 1], sin[:, 1]], axis=-1)
    return _slot(jnp.ones((s, MLA_NOPE), F32), cos_t), _slot(None, sin_t)


def _mla_weights(w_q_up, w_kv_up, g_q, g_k):
    depth = w_q_up.shape[0]
    swap = jnp.array(_ROPE_SWAP)
    wq = w_q_up.reshape(depth, MLA_Q_RANK, MLA_HEADS, MLA_QK)
    wkv = w_kv_up.reshape(depth, MLA_KV_RANK, MLA_HEADS, MLA_NOPE + MLA_V)
    flat = lambda w: w.reshape(depth, w.shape[1], -1).astype(BF16)
    wq_slot = flat(_slot(wq[..., :MLA_NOPE], wq[..., MLA_NOPE:]))
    wqs_slot = flat(_slot(None, wq[..., MLA_NOPE:][..., swap]))
    wkn_slot = flat(_slot(wkv[..., :MLA_NOPE], None))
    wv = jnp.swapaxes(flat(wkv[..., MLA_NOPE:]), 1, 2)
    gains = jnp.stack([_slot(g_q[:, :MLA_NOPE], g_q[:, MLA_NOPE:]), _slot(None, g_q[:, MLA_NOPE:][:, swap]),
                       _slot(g_k[:, :MLA_NOPE], g_k[:, MLA_NOPE:]), _slot(None, g_k[:, MLA_NOPE:][:, swap])],
                      axis=1)
    return wq_slot, wkn_slot, wv, gains, wqs_slot


def _project_kernel(*refs, rope):
    (x_ref, sh_ref, sc_ref, g_ref, w_hy_ref, w_na_ref, w_m_ref, gq_na_ref, gk_na_ref, bd_ref,
     gqa_ref, gkva_ref, wq_ref, wkn_ref, wv_ref, gains_ref) = refs[:16]
    rope_refs = refs[16:19] if rope else None
    o_hy_ref, qn_ref, kn_ref, vtn_ref, qm_ref, km_ref, vtm_ref = refs[-7:]
    h = _adaln(x_ref[0], g_ref[...], sh_ref[0, 0], sc_ref[0, 0]).astype(BF16)
    _mla_prep_math(_dot(h, w_m_ref[0]), gqa_ref[...], gkva_ref[...], wq_ref, wkn_ref, wv_ref, gains_ref[0],
                   rope_refs, qm_ref, km_ref, vtm_ref)
    _na_prep_math(_dot(h, w_na_ref[0]), gq_na_ref[...], gk_na_ref[...], bd_ref[...], qn_ref, kn_ref, vtn_ref)
    o_hy_ref[0] = _dot(h, w_hy_ref[0]).astype(BF16)


def _project(x, mod, g, w_in, w_m, layer, na_g_q, na_g_k, g_qa, g_kva, mla_w, rope_tabs, tm):
    b, t, d = x.shape
    tm = _tile(t, tm)
    n_hy, n_na, n_m = 3 * HY_DIM, 3 * NA_DIM, w_m.shape[2]
    assert n_hy == n_na
    wq_slot, wkn_slot, wv, gains, wqs_slot = mla_w
    rope = rope_tabs is not None
    gq = jnp.tile(na_g_q, NA_HEADS)[None, :]
    gk = jnp.tile(na_g_k, NA_HEADS)[None, :]
    head = jnp.arange(NA_DIM) // NA_HEAD_DIM
    bd = (head[:, None] == head[None, :]).astype(BF16)
    const = lambda i, j: (0, 0)
    per_layer = lambda w: pl.BlockSpec((1,) + w.shape[1:], lambda i, j: (layer, 0, 0))
    in_specs = [pl.BlockSpec((1, tm, d), lambda i, j: (i, j, 0)),
                pl.BlockSpec((1, 1, 1, d), lambda i, j: (i, 0, 0, 0)),
                pl.BlockSpec((1, 1, 1, d), lambda i, j: (i, 1, 0, 0)),
                pl.BlockSpec((1, d), const),
                _resident((1, d, n_hy), lambda i, j: (layer, 0, 0)),
                _resident((1, d, n_na), lambda i, j: (layer, 0, 1)),
                _resident((1, d, n_m), lambda i, j: (layer, 0, 0)),
                pl.BlockSpec((1, NA_DIM), const), pl.BlockSpec((1, NA_DIM), const),
                pl.BlockSpec((NA_DIM, NA_DIM), const),
                pl.BlockSpec((1, MLA_Q_RANK), const), pl.BlockSpec((1, MLA_KV_RANK), const),
                per_layer(wq_slot), per_layer(wkn_slot), per_layer(wv), per_layer(gains)]
    args = [x, mod, mod, g, w_in, w_in, w_m, gq, gk, bd, g_qa[None, :], g_kva[None, :],
            wq_slot, wkn_slot, wv, gains]
    if rope:
        in_specs += [per_layer(wqs_slot),
                     pl.BlockSpec((tm, MLA_SLOT), lambda i, j: (j, 0)),
                     pl.BlockSpec((tm, MLA_SLOT), lambda i, j: (j, 0))]
        args += [wqs_slot] + list(rope_tabs)
    tok_major = lambda heads, width: (jax.ShapeDtypeStruct((b, heads, t, width), BF16),
                                      pl.BlockSpec((1, heads, tm, width), lambda i, j: (i, 0, j, 0)))
    transposed = lambda rows: (jax.ShapeDtypeStruct((b, rows, t), BF16),
                               pl.BlockSpec((1, rows, tm), lambda i, j: (i, 0, j)))
    outs = [(jax.ShapeDtypeStruct((b, t, n_hy), BF16), pl.BlockSpec((1, tm, n_hy), lambda i, j: (i, j, 0))),
            tok_major(NA_HEADS, NA_HEAD_DIM), tok_major(NA_HEADS, NA_HEAD_DIM), transposed(NA_DIM),
            tok_major(MLA_HEADS, MLA_SLOT), tok_major(MLA_HEADS, MLA_SLOT), transposed(MLA_HEADS * MLA_V)]
    return pl.pallas_call(
        functools.partial(_project_kernel, rope=rope),
        grid=(b, t // tm),
        in_specs=in_specs,
        out_specs=[spec for _, spec in outs],
        out_shape=[shape for shape, _ in outs],
        compiler_params=_params(2),
    )(*args)


def _mix_mlp_kernel(hy_ref, na_ref, mla_ref, x_ref, gate1_ref, sh_ref, sc_ref, gate2_ref, g_ref,
                    w_hy_ref, w_na_ref, w_mla_ref, w1_ref, b1_ref, w2_ref, b2_ref, o_ref, *, chunk):
    mix = (_dot(hy_ref[0], w_hy_ref[0]) + _dot(na_ref[0], w_na_ref[0]) + _dot(mla_ref[0], w_mla_ref[0]))
    x = x_ref[0] + gate1_ref[0, 0] * mix
    h = _adaln(x, g_ref[...], sh_ref[0, 0], sc_ref[0, 0]).astype(BF16)
    d_ff = w1_ref.shape[2]
    acc = jnp.zeros(x.shape, F32)
    for c0 in range(0, d_ff, chunk):
        a = jnp.maximum(_dot(h, w1_ref[0, :, c0:c0 + chunk]) + b1_ref[:, c0:c0 + chunk], 0.0)
        acc = acc + _dot((a * a).astype(BF16), w2_ref[0, c0:c0 + chunk, :])
    o_ref[0] = x + gate2_ref[0, 0] * (acc + b2_ref[...])


def _mix_mlp(o_hy, o_na, o_mla, x, mod, w_out, g, w1, b1, w2, b2, layer, tm):
    b, t, d = x.shape
    d_ff = w1.shape[2]
    n_hy, n_na, n_mla = o_hy.shape[-1], o_na.shape[-1], o_mla.shape[-1]
    assert n_hy == n_na and n_mla == n_hy + n_na
    tm = _tile(t, tm)
    tok = lambda n: pl.BlockSpec((1, tm, n), lambda i, j: (i, j, 0))
    modspec = lambda k: pl.BlockSpec((1, 1, 1, d), lambda i, j: (i, k, 0, 0))
    const = lambda i, j: (0, 0)
    lay = lambda i, j: (layer, 0, 0)
    return pl.pallas_call(
        functools.partial(_mix_mlp_kernel, chunk=_tile(d_ff, 1024)),
        grid=(b, t // tm),
        in_specs=[tok(n_hy), tok(n_na), tok(n_mla), tok(d),
                  modspec(2), modspec(3), modspec(4), modspec(5),
                  pl.BlockSpec((1, d), const),
                  _resident((1, n_hy, d), lay),
                  _resident((1, n_na, d), lambda i, j: (layer, 1, 0)),
                  _resident((1, n_mla, d), lambda i, j: (layer, 1, 0)),
                  _resident((1, d, d_ff), lay), pl.BlockSpec((1, d_ff), const),
                  _resident((1, d_ff, d), lay), pl.BlockSpec((1, d), const)],
        out_specs=tok(d),
        out_shape=jax.ShapeDtypeStruct((b, t, d), F32),
        compiler_params=_params(2),
    )(o_hy, o_na, o_mla, x, mod, mod, mod, mod, g, w_out, w_out, w_out, w1, b1, w2, b2)


def kernel(x, c, ctx, c_ctx, w_mod, b_mod, g_norm1, w_in, hy_conv_w, hy_conv_b, hy_f_w1, hy_f_b1, hy_f_w2, hy_f_b2, hy_f_w3, hy_f_b3, hy_freq, hy_bias, na_g_q, na_g_k, na_rpb, mla_g_qa, mla_g_kva, mla_w_q_up, mla_w_kv_up, mla_g_q, mla_g_k, w_out, g_norm2, w_ff1, b_ff1, w_ff2, b_ff2):
    b, s, d = x.shape
    lc = ctx.shape[1]
    depth = w_mod.shape[0]
    in_hn = 3 * HY_DIM + 3 * NA_DIM
    swap = jnp.array(_ROPE_SWAP)

    n_cond = -(-(b + 1) // 8) * 8
    cond = jnp.zeros((n_cond, d), F32).at[:b].set(c).at[b].set(c_ctx)
    mods = _modulation(cond, w_mod, b_mod).reshape(depth, n_cond, 6, 1, d)

    rope_tabs = _rope_tables(s)
    cs_x = _dft_matrices(s)
    cs_c = _dft_matrices(lc)
    filt = (hy_f_w1, hy_f_b1, hy_f_w2, hy_f_b2, hy_f_w3, hy_f_b3, hy_freq)
    spec_x = _hyena_spectra(s, cs_x, *filt)
    spec_c = _hyena_spectra(lc, cs_c, *filt)
    bias_tab = _na_bias_table(na_rpb, s // GRID_W)
    hy_bias = hy_bias.reshape(2 * depth, 1, HY_DIM)

    w_in_b, w_out_b = w_in.astype(BF16), w_out.astype(BF16)
    w_ff1_b, w_ff2_b = w_ff1.astype(BF16), w_ff2.astype(BF16)
    mla_w = _mla_weights(mla_w_q_up, mla_w_kv_up, mla_g_q, mla_g_k)
    w_kr = w_in[:, :, in_hn + MLA_Q_RANK + MLA_KV_RANK:]
    w_m_b = jnp.concatenate([w_in[:, :, in_hn:in_hn + MLA_Q_RANK + MLA_KV_RANK], _slot(None, w_kr),
                             _slot(None, w_kr[:, :, swap])], axis=-1).astype(BF16)

    cx = ctx
    for i in range(depth):
        last = i == depth - 1
        mod_x = mods[i, :b]
        mod_c = jnp.broadcast_to(mods[i, b], (b, 6, 1, d))
        g1 = g_norm1[i][None, :]
        prep = (i, na_g_q[i], na_g_k[i], mla_g_qa[i], mla_g_kva[i], mla_w)
        px_hy, qx, kx, vtx, mqx, mkx, mvx = _project(x, mod_x, g1, w_in_b, w_m_b, *prep, rope_tabs,
                                                     PROJECT_TOKEN_TILE)
        pc_hy, qc, kc, vtc, mqc, mkc, mvc = _project(cx, mod_c, g1, w_in_b, w_m_b, *prep, None,
                                                     PROJECT_TOKEN_TILE)

        o_na = _na_latent(qx, kx, vtx, kc, vtc, bias_tab, i)
        o_mla = _attention(mqx, mkx, mvx, mkc, mvc, q_blocks=MLA_QUERY_BLOCKS)

        o_hy = _hyena(px_hy, hy_conv_w[i], hy_conv_b[i][None, :], cs_x, *spec_x, hy_bias, i)

        g2 = g_norm2[i][None, :]
        b1, b2 = b_ff1[i][None, :], b_ff2[i][None, :]
        x = _mix_mlp(o_hy, o_na, o_mla, x, mod_x, w_out_b, g2, w_ff1_b, b1, w_ff2_b, b2, i, MLP_TOKEN_TILE)

        if not last:
            oc_hy = _hyena(pc_hy, hy_conv_w[i], hy_conv_b[i][None, :], cs_c, *spec_c, hy_bias, i)
            oc_na = _attention(qc, kc, vtc)
            oc_mla = _attention(mqc, mkc, mvc)
            cx = _mix_mlp(oc_hy, oc_na, oc_mla, cx, mod_c, w_out_b, g2, w_ff1_b, b1, w_ff2_b, b2, i,
                          MLP_TOKEN_TILE)
    return x
```

```python
import functools
import math

import jax
import jax.numpy as jnp
from jax import lax
from jax.experimental import pallas as pl
from jax.experimental.pallas import tpu as pltpu

F32 = jnp.float32
BF16 = jnp.bfloat16
HIGHEST = lax.Precision.HIGHEST

EPS = 1e-6
NEG_INF = -1e9
GRID_W = 64

HY_DIM = 256
HY_BANDS = 8
HY_DECAY_TARGET = 1e-2
HY_FAST = 0.3
HY_SLOW = 1.5

NA_HEADS = 4
NA_HEAD_DIM = 64
NA_DIM = NA_HEADS * NA_HEAD_DIM
NA_WIN_ROWS = 8
NA_WIN_COLS = 16

MLA_HEADS = 8
MLA_Q_RANK = 256
MLA_KV_RANK = 128
MLA_NOPE = 64
MLA_ROPE = 32
MLA_V = 64
MLA_QK = MLA_NOPE + MLA_ROPE
MLA_SLOT = 128
ROPE_BASE = 10000.0
ROPE_FREQS = MLA_ROPE // 4

VMEM_LIMIT_BYTES = 56 * 1024 * 1024
HYENA_VMEM_LIMIT_BYTES = 60 * 1024 * 1024
MLA_QUERY_BLOCKS = 2
PROJECT_TOKEN_TILE = 1024
MLP_TOKEN_TILE = 1024

def _params(n_grid_dims):
    return pltpu.CompilerParams(dimension_semantics=("arbitrary",) * n_grid_dims,
                                vmem_limit_bytes=VMEM_LIMIT_BYTES)


def _tile(total, preferred):
    t = min(total, preferred)
    while total % t:
        t //= 2
    return t


def _resident(shape, index_map):
    return pl.BlockSpec(shape, index_map, pipeline_mode=pl.Buffered(1))


def _dot(a, b):
    return jnp.dot(a, b, preferred_element_type=F32)


def _split_bf16(x):
    hi = x.astype(BF16)
    return hi, (x - hi.astype(F32)).astype(BF16)


def _dot_3x(a, b):
    a_hi, a_lo = _split_bf16(a)
    b_hi, b_lo = _split_bf16(b)
    return _dot(a_hi, b_hi) + (_dot(a_hi, b_lo) + _dot(a_lo, b_hi))


def _dot_hi(a, b):
    return jnp.dot(a, b, preferred_element_type=F32, precision=HIGHEST)


def _dot_nt(a, b):
    return lax.dot_general(a, b, (((1,), (1,)), ((), ())), preferred_element_type=F32)


def _mod_kernel(cond_ref, w_ref, b_ref, o_ref):
    a = cond_ref[...]
    a = a / (1.0 + jnp.exp(-a))
    n = a.shape[0]
    a_hi, a_lo = _split_bf16(a)
    w_hi, w_lo = _split_bf16(w_ref[0])
    y = _dot(jnp.concatenate([a_hi, a_lo], axis=0), w_hi)
    o_ref[0] = y[:n] + y[n:] + _dot(a_hi, w_lo) + b_ref[0]


def _modulation(cond, w_mod, b_mod):
    depth, d, d6 = w_mod.shape
    n = cond.shape[0]
    tn = _tile(d6, 1536)
    return pl.pallas_call(
        _mod_kernel,
        grid=(depth, d6 // tn),
        in_specs=[pl.BlockSpec((n, d), lambda l, j: (0, 0)),
                  pl.BlockSpec((1, d, tn), lambda l, j: (l, 0, j)),
                  pl.BlockSpec((1, 1, tn), lambda l, j: (l, 0, j))],
        out_specs=pl.BlockSpec((1, n, tn), lambda l, j: (l, 0, j)),
        out_shape=jax.ShapeDtypeStruct((depth, n, d6), F32),
        compiler_params=_params(2),
    )(cond, w_mod, b_mod.reshape(depth, 1, d6))


def _adaln(x, g, shift, scale):
    y = x * lax.rsqrt(jnp.mean(x * x, axis=-1, keepdims=True) + EPS) * g
    return y * (1.0 + scale) + shift


def _dft_matrices(length):
    n = 2 * length
    blk = min(128, length)
    n_blk = length // blk
    f = jnp.arange(length, dtype=jnp.int32)[:, None]
    t0 = (blk * jnp.arange(n_blk, dtype=jnp.int32))[None, :]
    dt = jnp.arange(blk, dtype=jnp.int32)[None, :]
    a = ((f * t0) % n).astype(F32) * (2.0 * math.pi / n)
    b = ((f * dt) % n).astype(F32) * (2.0 * math.pi / n)
    rows = _tile(length, 256)

    def combine(ca_ref, sa_ref, cb_ref, sb_ref, c_ref, s_ref):
        cb, sb = cb_ref[...], sb_ref[...]
        for j in range(n_blk):
            ca, sa = ca_ref[:, j:j + 1], sa_ref[:, j:j + 1]
            c_ref[:, j * blk:(j + 1) * blk] = (ca * cb - sa * sb).astype(BF16)
            s_ref[:, j * blk:(j + 1) * blk] = (sa * cb + ca * sb).astype(BF16)

    coarse = pl.BlockSpec((rows, n_blk), lambda i: (i, 0))
    fine = pl.BlockSpec((rows, blk), lambda i: (i, 0))
    full = pl.BlockSpec((rows, length), lambda i: (i, 0))
    out = jax.ShapeDtypeStruct((length, length), BF16)
    return pl.pallas_call(
        combine,
        grid=(length // rows,),
        in_specs=[coarse, coarse, fine, fine],
        out_specs=[full, full],
        out_shape=[out, out],
        compiler_params=_params(1),
    )(jnp.cos(a), jnp.sin(a), jnp.cos(b), jnp.sin(b))


def _filt_kernel(z_ref, dec_ref, w1_ref, b1_ref, w2_ref, b2_ref, w3_ref, b3_ref, fr_ref, hsum_ref, hdiff_ref):
    z = z_ref[...]
    fr = fr_ref[0]
    a = jnp.sin(fr[0:1] * (_dot_3x(z, w1_ref[0]) + b1_ref[0]))
    a = jnp.sin(fr[1:2] * (_dot_3x(a, w2_ref[0]) + b2_ref[0]))
    h = _dot_3x(a, w3_ref[0]) + b3_ref[0]
    dec = dec_ref[...]
    c = HY_DIM
    row0 = lax.broadcasted_iota(jnp.int32, dec.shape, 0) == 0
    for o in range(2):
        hf = h[:, (2 * o) * c:(2 * o + 1) * c] * dec
        hb = h[:, (2 * o + 1) * c:(2 * o + 2) * c] * dec
        nrm = (jnp.sum(jnp.abs(hf), axis=0, keepdims=True)
               + jnp.sum(jnp.abs(hb), axis=0, keepdims=True) + EPS)
        hf = hf / nrm
        hb = jnp.where(row0, 0.0, hb / nrm)
        hsum_ref[0, :, o * c:(o + 1) * c] = hf + hb
        hdiff_ref[0, :, o * c:(o + 1) * c] = hf - hb


def _alternating_sign(shape):
    row = lax.broadcasted_iota(jnp.int32, shape, 0)
    return (1 - 2 * (row % 2)).astype(F32)


def _spec_kernel(hsum_ref, hdiff_ref, c_ref, s_ref, hc_ref, hs_ref, hn_ref):
    length = hsum_ref.shape[1]
    hsum = hsum_ref[0]
    hdiff = hdiff_ref[0]

    row = lax.broadcasted_iota(jnp.int32, hsum.shape, 0)
    wf = jnp.where(row == 0, 0.5 / length, 1.0 / length)
    hc_ref[0] = _dot(c_ref[...], hsum.astype(BF16)) * wf
    hs_ref[0] = _dot(s_ref[...], hdiff.astype(BF16)) * wf
    hn_ref[0] = jnp.sum(_alternating_sign(hsum.shape) * hsum, axis=0, keepdims=True) * (0.5 / length)


def _hyena_spectra(length, cs, f_w1, f_b1, f_w2, f_b2, f_w3, f_b3, freq):
    depth = f_w1.shape[0]
    pos_dim = f_w1.shape[1]
    hid = f_w1.shape[2]
    kpad = 32
    t = jnp.linspace(0.0, 1.0, length, dtype=F32)[:, None]
    w = (2.0 * math.pi / length) * jnp.arange(length, dtype=F32)[:, None]
    bands = jnp.linspace(1e-4, HY_BANDS - 1, HY_BANDS, dtype=F32)
    z = jnp.concatenate([t, jnp.cos(bands * w), -jnp.sin(bands * w)], axis=-1)
    z = jnp.pad(z, ((0, 0), (0, kpad - pos_dim)))
    w1 = jnp.pad(f_w1, ((0, 0), (0, kpad - pos_dim), (0, 0)))
    deltas = jnp.linspace(math.log(HY_DECAY_TARGET) / HY_SLOW, math.log(HY_DECAY_TARGET) / HY_FAST,
                          HY_DIM, dtype=F32)
    decay = jnp.exp(-t * jnp.abs(deltas))
    c2 = 2 * HY_DIM
    c4 = 4 * HY_DIM
    lay3 = lambda l: (l, 0, 0)
    hsum, hdiff = pl.pallas_call(
        _filt_kernel,
        grid=(depth,),
        in_specs=[pl.BlockSpec((length, kpad), lambda l: (0, 0)),
                  pl.BlockSpec((length, HY_DIM), lambda l: (0, 0)),
                  pl.BlockSpec((1, kpad, hid), lay3),
                  pl.BlockSpec((1, 1, hid), lay3),
                  pl.BlockSpec((1, hid, hid), lay3),
                  pl.BlockSpec((1, 1, hid), lay3),
                  pl.BlockSpec((1, hid, c4), lay3),
                  pl.BlockSpec((1, 1, c4), lay3),
                  pl.BlockSpec((1, 2, hid), lay3)],
        out_specs=[pl.BlockSpec((1, length, c2), lay3),
                   pl.BlockSpec((1, length, c2), lay3)],
        out_shape=[jax.ShapeDtypeStruct((depth, length, c2), F32),
                   jax.ShapeDtypeStruct((depth, length, c2), F32)],
        compiler_params=_params(1),
    )(z, decay, w1, f_b1.reshape(depth, 1, hid), f_w2, f_b2.reshape(depth, 1, hid),
      f_w3, f_b3.reshape(depth, 1, c4), freq)
    blk_in = pl.BlockSpec((1, length, c2), lay3, pipeline_mode=pl.Buffered(1))
    blk_out = pl.BlockSpec((1, length, c2), lay3)
    return pl.pallas_call(
        _spec_kernel,
        grid=(depth,),
        in_specs=[blk_in, blk_in, _resident((length, length), lambda l: (0, 0)),
                  _resident((length, length), lambda l: (0, 0))],
        out_specs=[blk_out, blk_out, pl.BlockSpec((1, 1, c2), lay3)],
        out_shape=[jax.ShapeDtypeStruct((depth, length, c2), F32),
                   jax.ShapeDtypeStruct((depth, length, c2), F32),
                   jax.ShapeDtypeStruct((depth, 1, c2), F32)],
        compiler_params=pltpu.CompilerParams(dimension_semantics=("arbitrary",),
                                             vmem_limit_bytes=HYENA_VMEM_LIMIT_BYTES),
    )(hsum, hdiff, *cs)


HY_ROWS = 256
HY_HALO = 8


HY_PAIR = 2


def _hyena_kernel(pv_ref, pg_ref, wv_ref, bv_ref, wg_ref, bg_ref, c_ref, s_ref, hc_ref, hs_ref, hn_ref, bias_ref,
                  o_ref, z_scr, xc_scr, xs_scr, yc_scr, ys_scr):
    length = pv_ref.shape[1]
    c = HY_DIM
    rows = min(HY_ROWS, length)
    chunks = [(r0, rows) for r0 in range(0, length, rows)]
    halves = [(i, slice(i * c, (i + 1) * c)) for i in range(HY_PAIR)]

    def short_conv_rows(p_ref, w_ref, b_ref, i, r0):
        lo = max(r0 - HY_HALO, 0)
        hi = min(r0 + rows + HY_HALO, length)
        u = p_ref[i, lo:hi, :].astype(F32)
        row = lax.broadcasted_iota(jnp.int32, u.shape, 0) + lo
        prev = jnp.where(row == 0, 0.0, pltpu.roll(u, 1, 0))
        nxt = jnp.where(row == length - 1, 0.0, pltpu.roll(u, hi - lo - 1, 0))
        y = prev * w_ref[0:1, :] + u * w_ref[1:2, :] + nxt * w_ref[2:3, :] + b_ref[...]
        return y[r0 - lo:r0 - lo + rows]

    @pl.when(pl.program_id(1) == 0)
    def _():
        for r0, n in chunks:
            for i, lanes in halves:
                z_scr[r0:r0 + n, lanes] = short_conv_rows(pv_ref, wv_ref, bv_ref, i, r0)

    sign = _alternating_sign((rows, c))
    zb = z_scr[...].astype(BF16)
    xc_scr[...] = _dot(c_ref[...], zb)
    xs_scr[...] = _dot(s_ref[...], zb)
    xn = [jnp.zeros((1, c), F32) for _ in halves]
    for r0, n in chunks:
        hc, hs = hc_ref[0, r0:r0 + n, :], hs_ref[0, r0:r0 + n, :]
        for i, lanes in halves:
            xc, xs = xc_scr[r0:r0 + n, lanes], xs_scr[r0:r0 + n, lanes]
            yc_scr[r0:r0 + n, lanes] = (xc * hc - xs * hs).astype(BF16)
            ys_scr[r0:r0 + n, lanes] = (xc * hs + xs * hc).astype(BF16)
            xn[i] = xn[i] + jnp.sum(sign * z_scr[r0:r0 + n, lanes], axis=0, keepdims=True)
    xc_scr[...] = _dot(c_ref[...], yc_scr[...])
    xs_scr[...] = _dot(s_ref[...], ys_scr[...])
    bias = bias_ref[0]
    for r0, n in chunks:
        for i, lanes in halves:
            z = z_scr[r0:r0 + n, lanes]
            conv = xc_scr[r0:r0 + n, lanes] + xs_scr[r0:r0 + n, lanes] + sign * (xn[i] * hn_ref[0])
            z = short_conv_rows(pg_ref, wg_ref, bg_ref, i, r0) * (conv + z * bias)
            z_scr[r0:r0 + n, lanes] = z
            o_ref[i, r0:r0 + n, :] = z.astype(o_ref.dtype)


def _hyena(p_hy, conv_w, conv_b, cs, hc, hs, hn, bias, layer):
    b, length, _ = p_hy.shape
    assert b % HY_PAIR == 0
    c = HY_DIM
    wide = HY_PAIR * c
    return pl.pallas_call(
        _hyena_kernel,
        grid=(b // HY_PAIR, 2),
        in_specs=[pl.BlockSpec((HY_PAIR, length, c), lambda i, o: (i, 0, 0)),
                  pl.BlockSpec((HY_PAIR, length, c), lambda i, o: (i, 0, o + 1)),
                  pl.BlockSpec((3, c), lambda i, o: (0, 0)),
                  pl.BlockSpec((1, c), lambda i, o: (0, 0)),
                  pl.BlockSpec((3, c), lambda i, o: (0, o + 1)),
                  pl.BlockSpec((1, c), lambda i, o: (0, o + 1)),
                  _resident((length, length), lambda i, o: (0, 0)),
                  _resident((length, length), lambda i, o: (0, 0)),
                  pl.BlockSpec((1, length, c), lambda i, o: (layer, 0, o)),
                  pl.BlockSpec((1, length, c), lambda i, o: (layer, 0, o)),
                  pl.BlockSpec((1, 1, c), lambda i, o: (layer, 0, o)),
                  pl.BlockSpec((1, 1, c), lambda i, o: (2 * layer + o, 0, 0))],
        out_specs=pl.BlockSpec((HY_PAIR, length, c), lambda i, o: (i, 0, 0)),
        out_shape=jax.ShapeDtypeStruct((b, length, c), BF16),
        scratch_shapes=[pltpu.VMEM((length, wide), F32), pltpu.VMEM((length, wide), F32),
                        pltpu.VMEM((length, wide), F32), pltpu.VMEM((length, wide), BF16),
                        pltpu.VMEM((length, wide), BF16)],
        compiler_params=pltpu.CompilerParams(dimension_semantics=("arbitrary", "arbitrary"),
                                             vmem_limit_bytes=HYENA_VMEM_LIMIT_BYTES),
    )(p_hy, p_hy, conv_w, conv_b, conv_w, conv_b, *cs, hc, hs, hn, bias)


def _na_prep_math(p, gq, gk, bd, qo_ref, ko_ref, vo_ref):
    def head_rms(x, g):
        sq = x * x
        hi = sq.astype(BF16)
        lo = (sq - hi.astype(F32)).astype(BF16)
        ms = (_dot(hi, bd) + _dot(lo, bd)) * (1.0 / NA_HEAD_DIM)
        return x * lax.rsqrt(ms + EPS) * g

    q = (head_rms(p[:, :NA_DIM], gq) * (NA_HEAD_DIM ** -0.5 * LOG2E)).astype(BF16)
    k = head_rms(p[:, NA_DIM:2 * NA_DIM], gk).astype(BF16)
    for h in range(NA_HEADS):
        sl = slice(h * NA_HEAD_DIM, (h + 1) * NA_HEAD_DIM)
        qo_ref[0, h] = q[:, sl]
        ko_ref[0, h] = k[:, sl]
    vo_ref[0] = p[:, 2 * NA_DIM:].T.astype(BF16)


NA_QROWS = 8
NA_KROWS = NA_QROWS + NA_WIN_ROWS
NA_BATCH_PAIR = 4


def _rpb_expand_kernel(r_ref, oh_ref, m_ref, o_ref):
    o_ref[...] = (_dot_hi(r_ref[...], oh_ref[...]) + m_ref[...]) * LOG2E


def _na_first_key_row(rb, rows):
    return jnp.clip(NA_QROWS * rb - NA_WIN_ROWS // 2, 0, rows - NA_KROWS)


def _na_block_layouts(rows):
    nrb = rows // NA_QROWS

    def layout(rb):
        w0 = min(max(NA_QROWS * rb - NA_WIN_ROWS // 2, 0), rows - NA_KROWS)
        out = []
        for j in range(NA_QROWS):
            r = NA_QROWS * rb + j
            row0 = min(max(r - NA_WIN_ROWS // 2, 0), rows - NA_WIN_ROWS)
            assert w0 <= row0 and row0 + NA_WIN_ROWS <= w0 + NA_KROWS
            out.append((r - w0, row0 - w0))
        return tuple(out)

    assert all(layout(rb) == layout(1) for rb in range(1, nrb - 1))
    return [layout(0), layout(min(1, nrb - 1)), layout(nrb - 1)]


def _na_table_kernel(t_ref, o_ref, *, layouts):
    neg = jnp.full((GRID_W, GRID_W), NEG_INF, F32)
    for pos, layout in enumerate(layouts):
        for i in range(NA_KROWS):
            pieces = [t_ref[0, i - r_rel + NA_WIN_ROWS - 1] if row0_rel <= i < row0_rel + NA_WIN_ROWS else neg
                      for r_rel, row0_rel in layout]
            o_ref[0, pos, i * GRID_W:(i + 1) * GRID_W, :] = jnp.concatenate(pieces, axis=1)


def _na_bias_table(rpb, rows):
    depth, heads, n_dr, n_dc = rpb.shape
    kc = jnp.arange(GRID_W)[:, None]
    qc = jnp.arange(GRID_W)[None, :]
    dc = jnp.clip(kc - qc + NA_WIN_COLS - 1, 0, n_dc - 1)
    cstart = jnp.clip(qc - NA_WIN_COLS // 2, 0, GRID_W - NA_WIN_COLS)
    valid = ((kc >= cstart) & (kc < cstart + NA_WIN_COLS)).reshape(1, GRID_W * GRID_W)
    n_pad = 32
    onehot = ((dc.reshape(1, -1) == jnp.arange(n_pad)[:, None]) & valid).astype(F32)
    mask = jnp.where(valid, 0.0, NEG_INF).astype(F32)
    rows = depth * heads * n_dr
    rpb2 = jnp.pad(rpb.reshape(rows, n_dc).astype(F32), ((0, 0), (0, n_pad - n_dc)))
    t1 = pl.pallas_call(
        _rpb_expand_kernel,
        out_shape=jax.ShapeDtypeStruct((rows, GRID_W * GRID_W), F32),
        compiler_params=pltpu.CompilerParams(vmem_limit_bytes=VMEM_LIMIT_BYTES),
    )(rpb2, onehot, mask).reshape(depth * heads, n_dr, GRID_W, GRID_W)
    nk, nq = NA_KROWS * GRID_W, NA_QROWS * GRID_W
    tab = pl.pallas_call(
        functools.partial(_na_table_kernel, layouts=_na_block_layouts(rows)),
        grid=(depth * heads,),
        in_specs=[pl.BlockSpec((1, n_dr, GRID_W, GRID_W), lambda g: (g, 0, 0, 0))],
        out_specs=pl.BlockSpec((1, 3, nk, nq), lambda g: (g, 0, 0, 0)),
        out_shape=jax.ShapeDtypeStruct((depth * heads, 3, nk, nq), F32),
        compiler_params=_params(1),
    )(t1)
    return tab.reshape(depth, heads, 3, nk, nq)


ATTN_CHUNK_ELEMS = 64 * 1024
ATTN_ONES_ROWS = 16
LOG2E = math.log2(math.e)


def _attend_heads(heads, q_of, sets_of, s_scrs):
    row_max = [None] * heads
    outs = [None] * heads

    def score_phase(h):
        q = q_of(h)
        scr = s_scrs[h % 2]
        off, m = 0, None
        for n, k_fn, _, bias_fn in sets_of(h):
            s = _dot_nt(k_fn(), q)
            if bias_fn is not None:
                s = s + bias_fn()
            scr[off:off + n, :] = s
            mj = jnp.max(s, axis=0, keepdims=True)
            m = mj if m is None else jnp.maximum(m, mj)
            off += n
            yield
        row_max[h] = m

    def value_phase(h):
        scr = s_scrs[h % 2]
        m = row_max[h]
        base, acc = 0, None
        for n_set, _, vt_fn, _ in sets_of(h):
            for off, n in _key_chunks(n_set, ATTN_CHUNK_ELEMS // scr.shape[1]):
                p = jnp.exp2(scr[base + off:base + off + n, :] - m).astype(BF16)
                vt = vt_fn(off, n)
                lhs = jnp.concatenate([vt, jnp.ones((ATTN_ONES_ROWS, n), BF16)], axis=0)
                oj = _dot(lhs, p)
                acc = oj if acc is None else acc + oj
                yield
            base += n_set
        dv = acc.shape[0] - ATTN_ONES_ROWS
        outs[h] = acc[:dv] / acc[dv:dv + 1]

    for _ in score_phase(0):
        pass
    for h in range(heads):
        nxt = score_phase(h + 1) if h + 1 < heads else iter(())
        cur = value_phase(h)
        done_n = done_c = False
        while not (done_n and done_c):
            if not done_n:
                done_n = next(nxt, "end") == "end"
            if not done_c:
                done_c = next(cur, "end") == "end"
    return jnp.concatenate(outs, axis=0)


def _key_chunks(total, chunk):
    sizes = [chunk] * (total // chunk)
    if total % chunk:
        sizes.append(total % chunk)
    offs = [sum(sizes[:i]) for i in range(len(sizes))]
    return list(zip(offs, sizes))


def _na_kernel(q_ref, k_ref, vt_ref, kc_ref, vct_ref, b_ref, o_ref, s0_scr, s1_scr, *, rows):
    start = pl.multiple_of(_na_first_key_row(pl.program_id(0), rows) * GRID_W, NA_WIN_ROWS // 2 * GRID_W)
    nk = NA_KROWS * GRID_W
    dh = NA_HEAD_DIM

    def sets_of(u):
        i, h = divmod(u, NA_HEADS)
        hs = slice(h * dh, (h + 1) * dh)
        window = (nk,
                  lambda: k_ref[i, h, pl.ds(start, nk), :],
                  lambda off, n: vt_ref[i, hs, pl.ds(pl.multiple_of(start + off, 128), n)],
                  lambda: b_ref[0, h, 0])
        context = (kc_ref.shape[2],
                   lambda: kc_ref[i, h],
                   lambda off, n: vct_ref[i, hs, off:off + n],
                   None)
        return [window, context]

    n_pair = q_ref.shape[0]
    o_t = _attend_heads(n_pair * NA_HEADS, lambda u: q_ref[u // NA_HEADS, u % NA_HEADS], sets_of,
                        (s0_scr, s1_scr))
    for i in range(n_pair):
        o_ref[i] = o_t[i * NA_DIM:(i + 1) * NA_DIM].T.astype(o_ref.dtype)


def _na_latent(q, k, vt, kc, vct, bias_tab, layer):
    b, heads, s, dh = q.shape
    ctx = kc.shape[2]
    rows = s // GRID_W
    assert rows % NA_QROWS == 0 and rows >= NA_KROWS and (rows - NA_KROWS) % (NA_WIN_ROWS // 2) == 0
    nrb = rows // NA_QROWS
    tq = NA_QROWS * GRID_W

    def bias_index(rb, i):
        return (layer, 0, jnp.where(rb == 0, 0, jnp.where(rb == nrb - 1, 2, 1)), 0, 0)

    pair = NA_BATCH_PAIR if b % NA_BATCH_PAIR == 0 else 1
    full4 = lambda rb, i: (i, 0, 0, 0)
    full3 = lambda rb, i: (i, 0, 0)
    return pl.pallas_call(
        functools.partial(_na_kernel, rows=rows),
        grid=(nrb, b // pair),
        in_specs=[pl.BlockSpec((pair, heads, tq, dh), lambda rb, i: (i, 0, rb, 0)),
                  pl.BlockSpec((pair, heads, s, dh), full4),
                  pl.BlockSpec((pair, heads * dh, s), full3),
                  pl.BlockSpec((pair, heads, ctx, dh), full4),
                  pl.BlockSpec((pair, heads * dh, ctx), full3),
                  pl.BlockSpec((1, heads, 1, NA_KROWS * GRID_W, tq), bias_index)],
        out_specs=pl.BlockSpec((pair, tq, heads * dh), lambda rb, i: (i, rb, 0)),
        out_shape=jax.ShapeDtypeStruct((b, s, heads * dh), BF16),
        scratch_shapes=[pltpu.VMEM((NA_KROWS * GRID_W + ctx, tq), F32)] * 2,
        compiler_params=_params(2),
    )(q, k, vt, kc, vct, bias_tab)


def _attn_kernel(*refs, heads, dv, two_sets, tq):
    if two_sets:
        q_ref, k1_ref, vt1_ref, k2_ref, vt2_ref, o_ref, s0_scr, s1_scr = refs
        key_sets = ((k1_ref, vt1_ref), (k2_ref, vt2_ref))
    else:
        q_ref, k1_ref, vt1_ref, o_ref, s0_scr, s1_scr = refs
        key_sets = ((k1_ref, vt1_ref),)
    q_blocks = q_ref.shape[2] // tq

    def sets_of(u):
        h = u % heads
        hs = slice(h * dv, (h + 1) * dv)
        return [(k_ref.shape[2],
                 lambda k_ref=k_ref: k_ref[0, h],
                 lambda off, n, vt_ref=vt_ref: vt_ref[0, hs, off:off + n],
                 None) for k_ref, vt_ref in key_sets]

    def q_of(u):
        blk, h = divmod(u, heads)
        return q_ref[0, h, blk * tq:(blk + 1) * tq, :]

    o_t = _attend_heads(q_blocks * heads, q_of, sets_of, (s0_scr, s1_scr))
    for blk in range(q_blocks):
        o_ref[0, blk * tq:(blk + 1) * tq, :] = o_t[blk * heads * dv:(blk + 1) * heads * dv].T.astype(o_ref.dtype)


def _attention(q, k1, vt1, k2=None, vt2=None, tq=512, q_blocks=1):
    b, heads, t, dq = q.shape
    dv = vt1.shape[1] // heads
    tq = _tile(t, tq)
    q_blocks = q_blocks if t % (tq * q_blocks) == 0 else 1
    tstep = tq * q_blocks
    two_sets = k2 is not None
    n_keys = k1.shape[2] + (k2.shape[2] if two_sets else 0)
    full4 = lambda i, j: (i, 0, 0, 0)
    full3 = lambda i, j: (i, 0, 0)
    in_specs = [pl.BlockSpec((1, heads, tstep, dq), lambda i, j: (i, 0, j, 0)),
                pl.BlockSpec((1,) + k1.shape[1:], full4),
                pl.BlockSpec((1,) + vt1.shape[1:], full3)]
    args = [q, k1, vt1]
    if two_sets:
        in_specs += [pl.BlockSpec((1,) + k2.shape[1:], full4),
                     pl.BlockSpec((1,) + vt2.shape[1:], full3)]
        args += [k2, vt2]
    return pl.pallas_call(
        functools.partial(_attn_kernel, heads=heads, dv=dv, two_sets=two_sets, tq=tq),
        grid=(b, t // tstep),
        in_specs=in_specs,
        out_specs=pl.BlockSpec((1, tstep, heads * dv), lambda i, j: (i, j, 0)),
        out_shape=jax.ShapeDtypeStruct((b, t, heads * dv), BF16),
        scratch_shapes=[pltpu.VMEM((n_keys, tq), F32)] * 2,
        compiler_params=_params(2),
    )(*args)


def _mla_prep_math(p, gqa, gkva, wq_ref, wkn_ref, wv_ref, g, rope_refs, qo_ref, ko_ref, vto_ref):
    rope = rope_refs is not None
    a, b_ = MLA_Q_RANK, MLA_Q_RANK + MLA_KV_RANK
    cq, ckv = p[:, :a], p[:, a:b_]
    kr, krs = p[:, b_:b_ + MLA_SLOT], p[:, b_ + MLA_SLOT:b_ + 2 * MLA_SLOT]

    def rms(x, g_):
        return x * lax.rsqrt(jnp.mean(x * x, axis=-1, keepdims=True) + EPS) * g_

    cqn = rms(cq, gqa).astype(BF16)
    ckvn = rms(ckv, gkva).astype(BF16)
    qa = _dot(cqn, wq_ref[0])
    kn = _dot(ckvn, wkn_ref[0])
    aq, ak = g[0:1], g[2:3]
    if rope:
        wqs_ref, cos_ref, sin_ref = rope_refs
        qs = _dot(cqn, wqs_ref[0])
        cos_t, sin_t = cos_ref[...], sin_ref[...]
        aq, bq = aq * cos_t, g[1:2] * sin_t
        ak, k_rot = ak * cos_t, krs * (g[3:4] * sin_t)

    def inv_rms(x):
        return lax.rsqrt(jnp.sum(x * x, axis=-1, keepdims=True) * (1.0 / MLA_QK) + EPS)

    for h in range(MLA_HEADS):
        sl = slice(h * MLA_SLOT, (h + 1) * MLA_SLOT)
        xq = qa[:, sl]
        yq = xq * aq
        if rope:
            yq = yq + qs[:, sl] * bq
        qo_ref[0, h] = (yq * (inv_rms(xq) * (MLA_QK ** -0.5 * LOG2E))).astype(BF16)
        xk = kn[:, sl] + kr
        yk = xk * ak
        if rope:
            yk = yk + k_rot
        ko_ref[0, h] = (yk * inv_rms(xk)).astype(BF16)
    vto_ref[0] = _dot_nt(wv_ref[0], ckvn).astype(BF16)


_ROPE_SWAP = tuple(list(range(8, 16)) + list(range(0, 8)) + list(range(24, 32)) + list(range(16, 24)))


def _slot(nope, rope_part):
    lead = (nope if nope is not None else rope_part).shape[:-1]
    dt = (nope if nope is not None else rope_part).dtype
    z = lambda n: jnp.zeros(lead + (n,), dt)
    return jnp.concatenate([nope if nope is not None else z(MLA_NOPE),
                            rope_part if rope_part is not None else z(MLA_ROPE),
                            z(MLA_SLOT - MLA_QK)], axis=-1)


def _rope_tables(s):
    t = jnp.arange(s)
    pos = jnp.stack([t // GRID_W, t % GRID_W], axis=-1).astype(F32)
    inv = ROPE_BASE ** (-jnp.arange(ROPE_FREQS, dtype=F32) / ROPE_FREQS)
    ang = pos[:, :, None] * inv
    cos, sin = jnp.cos(ang), jnp.sin(ang)
    cos_t = jnp.concatenate([cos[:, 0], cos[:, 0], cos[:, 1], cos[:, 1]], axis=-1)
    sin_t = jnp.concatenate([-sin[:, 0], sin[:, 0], -sin[:, 1], sin[:, 1]], axis=-1)
    return _slot(jnp.ones((s, MLA_NOPE), F32), cos_t), _slot(None, sin_t)


def _mla_weights(w_q_up, w_kv_up, g_q, g_k):
    depth = w_q_up.shape[0]
    swap = jnp.array(_ROPE_SWAP)
    wq = w_q_up.reshape(depth, MLA_Q_RANK, MLA_HEADS, MLA_QK)
    wkv = w_kv_up.reshape(depth, MLA_KV_RANK, MLA_HEADS, MLA_NOPE + MLA_V)
    flat = lambda w: w.reshape(depth, w.shape[1], -1).astype(BF16)
    wq_slot = flat(_slot(wq[..., :MLA_NOPE], wq[..., MLA_NOPE:]))
    wqs_slot = flat(_slot(None, wq[..., MLA_NOPE:][..., swap]))
    wkn_slot = flat(_slot(wkv[..., :MLA_NOPE], None))
    wv = jnp.swapaxes(flat(wkv[..., MLA_NOPE:]), 1, 2)
    gains = jnp.stack([_slot(g_q[:, :MLA_NOPE], g_q[:, MLA_NOPE:]), _slot(None, g_q[:, MLA_NOPE:][:, swap]),
                       _slot(g_k[:, :MLA_NOPE], g_k[:, MLA_NOPE:]), _slot(None, g_k[:, MLA_NOPE:][:, swap])],
                      axis=1)
    return wq_slot, wkn_slot, wv, gains, wqs_slot


def _project_kernel(*refs, rope):
    (x_ref, sh_ref, sc_ref, g_ref, w_hy_ref, w_na_ref, w_m_ref, gq_na_ref, gk_na_ref, bd_ref,
     gqa_ref, gkva_ref, wq_ref, wkn_ref, wv_ref, gains_ref) = refs[:16]
    rope_refs = refs[16:19] if rope else None
    o_hy_ref, qn_ref, kn_ref, vtn_ref, qm_ref, km_ref, vtm_ref = refs[-7:]
    h = _adaln(x_ref[0], g_ref[...], sh_ref[0, 0], sc_ref[0, 0]).astype(BF16)
    _mla_prep_math(_dot(h, w_m_ref[0]), gqa_ref[...], gkva_ref[...], wq_ref, wkn_ref, wv_ref, gains_ref[0],
                   rope_refs, qm_ref, km_ref, vtm_ref)
    _na_prep_math(_dot(h, w_na_ref[0]), gq_na_ref[...], gk_na_ref[...], bd_ref[...], qn_ref, kn_ref, vtn_ref)
    o_hy_ref[0] = _dot(h, w_hy_ref[0]).astype(BF16)


def _project(x, mod, g, w_in, w_m, layer, na_g_q, na_g_k, g_qa, g_kva, mla_w, rope_tabs, tm):
    b, t, d = x.shape
    tm = _tile(t, tm)
    n_hy, n_na, n_m = 3 * HY_DIM, 3 * NA_DIM, w_m.shape[2]
    assert n_hy == n_na
    wq_slot, wkn_slot, wv, gains, wqs_slot = mla_w
    rope = rope_tabs is not None
    gq = jnp.tile(na_g_q, NA_HEADS)[None, :]
    gk = jnp.tile(na_g_k, NA_HEADS)[None, :]
    head = jnp.arange(NA_DIM) // NA_HEAD_DIM
    bd = (head[:, None] == head[None, :]).astype(BF16)
    const = lambda i, j: (0, 0)
    per_layer = lambda w: pl.BlockSpec((1,) + w.shape[1:], lambda i, j: (layer, 0, 0))
    in_specs = [pl.BlockSpec((1, tm, d), lambda i, j: (i, j, 0)),
                pl.BlockSpec((1, 1, 1, d), lambda i, j: (i, 0, 0, 0)),
                pl.BlockSpec((1, 1, 1, d), lambda i, j: (i, 1, 0, 0)),
                pl.BlockSpec((1, d), const),
                _resident((1, d, n_hy), lambda i, j: (layer, 0, 0)),
                _resident((1, d, n_na), lambda i, j: (layer, 0, 1)),
                _resident((1, d, n_m), lambda i, j: (layer, 0, 0)),
                pl.BlockSpec((1, NA_DIM), const), pl.BlockSpec((1, NA_DIM), const),
                pl.BlockSpec((NA_DIM, NA_DIM), const),
                pl.BlockSpec((1, MLA_Q_RANK), const), pl.BlockSpec((1, MLA_KV_RANK), const),
                per_layer(wq_slot), per_layer(wkn_slot), per_layer(wv), per_layer(gains)]
    args = [x, mod, mod, g, w_in, w_in, w_m, gq, gk, bd, g_qa[None, :], g_kva[None, :],
            wq_slot, wkn_slot, wv, gains]
    if rope:
        in_specs += [per_layer(wqs_slot),
                     pl.BlockSpec((tm, MLA_SLOT), lambda i, j: (j, 0)),
                     pl.BlockSpec((tm, MLA_SLOT), lambda i, j: (j, 0))]
        args += [wqs_slot] + list(rope_tabs)
    tok_major = lambda heads, width: (jax.ShapeDtypeStruct((b, heads, t, width), BF16),
                                      pl.BlockSpec((1, heads, tm, width), lambda i, j: (i, 0, j, 0)))
    transposed = lambda rows: (jax.ShapeDtypeStruct((b, rows, t), BF16),
                               pl.BlockSpec((1, rows, tm), lambda i, j: (i, 0, j)))
    outs = [(jax.ShapeDtypeStruct((b, t, n_hy), BF16), pl.BlockSpec((1, tm, n_hy), lambda i, j: (i, j, 0))),
            tok_major(NA_HEADS, NA_HEAD_DIM), tok_major(NA_HEADS, NA_HEAD_DIM), transposed(NA_DIM),
            tok_major(MLA_HEADS, MLA_SLOT), tok_major(MLA_HEADS, MLA_SLOT), transposed(MLA_HEADS * MLA_V)]
    return pl.pallas_call(
        functools.partial(_project_kernel, rope=rope),
        grid=(b, t // tm),
        in_specs=in_specs,
        out_specs=[spec for _, spec in outs],
        out_shape=[shape for shape, _ in outs],
        compiler_params=_params(2),
    )(*args)


def _mix_mlp_kernel(hy_ref, na_ref, mla_ref, x_ref, gate1_ref, sh_ref, sc_ref, gate2_ref, g_ref,
                    w_hy_ref, w_na_ref, w_mla_ref, w1_ref, b1_ref, w2_ref, b2_ref, o_ref, *, chunk):
    mix = (_dot(hy_ref[0], w_hy_ref[0]) + _dot(na_ref[0], w_na_ref[0]) + _dot(mla_ref[0], w_mla_ref[0]))
    x = x_ref[0] + gate1_ref[0, 0] * mix
    h = _adaln(x, g_ref[...], sh_ref[0, 0], sc_ref[0, 0]).astype(BF16)
    d_ff = w1_ref.shape[2]
    acc = jnp.zeros(x.shape, F32)
    for c0 in range(0, d_ff, chunk):
        a = jnp.maximum(_dot(h, w1_ref[0, :, c0:c0 + chunk]) + b1_ref[:, c0:c0 + chunk], 0.0)
        acc = acc + _dot((a * a).astype(BF16), w2_ref[0, c0:c0 + chunk, :])
    o_ref[0] = x + gate2_ref[0, 0] * (acc + b2_ref[...])


def _mix_mlp(o_hy, o_na, o_mla, x, mod, w_out, g, w1, b1, w2, b2, layer, tm):
    b, t, d = x.shape
    d_ff = w1.shape[2]
    n_hy, n_na, n_mla = o_hy.shape[-1], o_na.shape[-1], o_mla.shape[-1]
    assert n_hy == n_na and n_mla == n_hy + n_na
    tm = _tile(t, tm)
    tok = lambda n: pl.BlockSpec((1, tm, n), lambda i, j: (i, j, 0))
    modspec = lambda k: pl.BlockSpec((1, 1, 1, d), lambda i, j: (i, k, 0, 0))
    const = lambda i, j: (0, 0)
    lay = lambda i, j: (layer, 0, 0)
    return pl.pallas_call(
        functools.partial(_mix_mlp_kernel, chunk=_tile(d_ff, 1024)),
        grid=(b, t // tm),
        in_specs=[tok(n_hy), tok(n_na), tok(n_mla), tok(d),
                  modspec(2), modspec(3), modspec(4), modspec(5),
                  pl.BlockSpec((1, d), const),
                  _resident((1, n_hy, d), lay),
                  _resident((1, n_na, d), lambda i, j: (layer, 1, 0)),
                  _resident((1, n_mla, d), lambda i, j: (layer, 1, 0)),
                  _resident((1, d, d_ff), lay), pl.BlockSpec((1, d_ff), const),
                  _resident((1, d_ff, d), lay), pl.BlockSpec((1, d), const)],
        out_specs=tok(d),
        out_shape=jax.ShapeDtypeStruct((b, t, d), F32),
        compiler_params=_params(2),
    )(o_hy, o_na, o_mla, x, mod, mod, mod, mod, g, w_out, w_out, w_out, w1, b1, w2, b2)


def kernel(x, c, ctx, c_ctx, w_mod, b_mod, g_norm1, w_in, hy_conv_w, hy_conv_b, hy_f_w1, hy_f_b1, hy_f_w2, hy_f_b2, hy_f_w3, hy_f_b3, hy_freq, hy_bias, na_g_q, na_g_k, na_rpb, mla_g_qa, mla_g_kva, mla_w_q_up, mla_w_kv_up, mla_g_q, mla_g_k, w_out, g_norm2, w_ff1, b_ff1, w_ff2, b_ff2):
    b, s, d = x.shape
    lc = ctx.shape[1]
    depth = w_mod.shape[0]
    in_hn = 3 * HY_DIM + 3 * NA_DIM
    swap = jnp.array(_ROPE_SWAP)

    n_cond = -(-(b + 1) // 8) * 8
    cond = jnp.zeros((n_cond, d), F32).at[:b].set(c).at[b].set(c_ctx)
    mods = _modulation(cond, w_mod, b_mod).reshape(depth, n_cond, 6, 1, d)

    rope_tabs = _rope_tables(s)
    cs_x = _dft_matrices(s)
    cs_c = _dft_matrices(lc)
    filt = (hy_f_w1, hy_f_b1, hy_f_w2, hy_f_b2, hy_f_w3, hy_f_b3, hy_freq)
    spec_x = _hyena_spectra(s, cs_x, *filt)
    spec_c = _hyena_spectra(lc, cs_c, *filt)
    bias_tab = _na_bias_table(na_rpb, s // GRID_W)
    hy_bias = hy_bias.reshape(2 * depth, 1, HY_DIM)

    w_in_b, w_out_b = w_in.astype(BF16), w_out.astype(BF16)
    w_ff1_b, w_ff2_b = w_ff1.astype(BF16), w_ff2.astype(BF16)
    mla_w = _mla_weights(mla_w_q_up, mla_w_kv_up, mla_g_q, mla_g_k)
    w_kr = w_in[:, :, in_hn + MLA_Q_RANK + MLA_KV_RANK:]
    w_m_b = jnp.concatenate([w_in[:, :, in_hn:in_hn + MLA_Q_RANK + MLA_KV_RANK], _slot(None, w_kr),
                             _slot(None, w_kr[:, :, swap])], axis=-1).astype(BF16)

    cx = ctx
    for i in range(depth):
        last = i == depth - 1
        mod_x = mods[i, :b]
        mod_c = jnp.broadcast_to(mods[i, b], (b, 6, 1, d))
        g1 = g_norm1[i][None, :]
        prep = (i, na_g_q[i], na_g_k[i], mla_g_qa[i], mla_g_kva[i], mla_w)
        px_hy, qx, kx, vtx, mqx, mkx, mvx = _project(x, mod_x, g1, w_in_b, w_m_b, *prep, rope_tabs,
                                                     PROJECT_TOKEN_TILE)
        pc_hy, qc, kc, vtc, mqc, mkc, mvc = _project(cx, mod_c, g1, w_in_b, w_m_b, *prep, None,
                                                     PROJECT_TOKEN_TILE)

        o_na = _na_latent(qx, kx, vtx, kc, vtc, bias_tab, i)
        o_mla = _attention(mqx, mkx, mvx, mkc, mvc, q_blocks=MLA_QUERY_BLOCKS)

        o_hy = _hyena(px_hy, hy_conv_w[i], hy_conv_b[i][None, :], cs_x, *spec_x, hy_bias, i)

        g2 = g_norm2[i][None, :]
        b1, b2 = b_ff1[i][None, :], b_ff2[i][None, :]
        x = _mix_mlp(o_hy, o_na, o_mla, x, mod_x, w_out_b, g2, w_ff1_b, b1, w_ff2_b, b2, i, MLP_TOKEN_TILE)

        if not last:
            oc_hy = _hyena(pc_hy, hy_conv_w[i], hy_conv_b[i][None, :], cs_c, *spec_c, hy_bias, i)
            oc_na = _attention(qc, kc, vtc)
            oc_mla = _attention(mqc, mkc, mvc)
            cx = _mix_mlp(oc_hy, oc_na, oc_mla, cx, mod_c, w_out_b, g2, w_ff1_b, b1, w_ff2_b, b2, i,
                          MLP_TOKEN_TILE)
    return x
```
